```python
import math
import jax
import jax.numpy as jnp
from jax import lax
import numpy as np

D_MODEL = 1024
BATCH = 4
SEQ = 4096
DEPTH = 2

N_EVEN = (DEPTH + 1) // 2
N_ODD = DEPTH // 2
MIX_WIDTH = D_MODEL
D_FF = 4 * D_MODEL
DEEPNORM_ALPHA = (2.0 * DEPTH) ** 0.25
DEEPNORM_BETA = (8.0 * DEPTH) ** -0.25
LN_EPS = 1e-5
RMS_EPS = 1e-6

HY_DIM = MIX_WIDTH // 2
HY_ORDER = 2
HY_SHORT = 3
HY_EMB = 33
HY_BANDS = (HY_EMB - 1) // 2
HY_FILT_HID = 64
HY_DECAY_TARGET = 1e-2
HY_FAST_DECAY_PCT = 0.3
HY_SLOW_DECAY_PCT = 1.5

GDN_HEADS = 4
GDN_DK = 128
GDN_DV = (MIX_WIDTH - HY_DIM) // GDN_HEADS
GDN_CONV = 5
GDN_CHUNK = 64

HG_HEADS = 4
HG_DK = 128
HG_DV = 128
HG_CHUNK = 64

RW_HEADS = 8
RW_HD = 64
RW_DIM = RW_HEADS * RW_HD
RW_W_LORA = 64
RW_A_LORA = 64
RW_G_LORA = 128
RW_GN_EPS = 64e-5

EVEN_SPLITS = (3 * HY_DIM, GDN_HEADS * GDN_DK, GDN_HEADS * GDN_DK, GDN_HEADS * GDN_DV,
               GDN_HEADS * GDN_DV, 2 * GDN_HEADS, GDN_HEADS)
EVEN_IN = sum(EVEN_SPLITS)
RW_SPLITS = (RW_DIM, RW_DIM, RW_DIM, RW_W_LORA, RW_A_LORA, RW_G_LORA)
RW_IN = sum(RW_SPLITS)
ODD_SPLITS = (HG_HEADS * HG_DK, 2 * HG_HEADS * HG_DK, HG_HEADS * HG_DV, HG_HEADS * HG_DV, RW_IN)
ODD_IN = sum(ODD_SPLITS)

kernel_name = 'hybrid_bidir_hyena_gdn_hgrn2_rwkv7'


def split_cols(y, sizes):
    return jnp.split(y, [int(s) for s in np.cumsum(sizes)[:-1]], axis=-1)


def layer_norm(x, g, b):
    xf = x.astype(jnp.float32)
    xc = xf - jnp.mean(xf, -1, keepdims=True)
    var = jnp.mean(xc * xc, -1, keepdims=True)
    return (xc * lax.rsqrt(var + LN_EPS) * g + b).astype(x.dtype)


def rms_norm(x, g):
    return x * lax.rsqrt(jnp.mean(x * x, -1, keepdims=True) + RMS_EPS) * g


def l2_normalize(x):
    return x * lax.rsqrt(jnp.sum(x * x, -1, keepdims=True) + 1e-6)


def centred_dwconv(u, w):
    width = w.shape[0]
    return lax.conv_general_dilated(u, w[:, None, :].astype(u.dtype), (1,), [(width // 2, width // 2)],
                                    dimension_numbers=('NWC', 'WIO', 'NWC'),
                                    feature_group_count=u.shape[-1])


def to_chunks(a, chunk):
    z, b, t, h = a.shape[:4]
    a = a.reshape((z, b, t // chunk, chunk, h) + a.shape[4:])
    return jnp.swapaxes(a, 3, 4)


def from_chunks(a):
    a = jnp.swapaxes(a, 3, 4)
    z, b, n, c, h = a.shape[:5]
    return a.reshape((z, b, n * c, h) + a.shape[5:])


def both_directions(a):
    return jnp.stack([a, jnp.flip(a, 1)], 0)


def merge_directions(o):
    return o[0] + jnp.flip(o[1], 1)


def hyena_filters(seq_len, w1, b1, w2, b2, w3, b3, w4, freq):
    pos = jnp.arange(seq_len, dtype=jnp.float32)[:, None]
    t = pos / max(seq_len - 1, 1)
    bands = jnp.linspace(1e-4, HY_BANDS - 1, HY_BANDS, dtype=jnp.float32)[None, :]
    ang = bands * (2.0 * math.pi / seq_len) * pos
    z = jnp.concatenate([t, jnp.cos(ang), -jnp.sin(ang)], -1)
    h = jnp.sin(freq * (z @ w1 + b1))
    h = jnp.sin(freq * (h @ w2 + b2))
    h = jnp.sin(freq * (h @ w3 + b3))
    h = (h @ w4).reshape(seq_len, HY_ORDER, 2, HY_DIM)
    max_decay = math.log(HY_DECAY_TARGET) / HY_FAST_DECAY_PCT
    min_decay = math.log(HY_DECAY_TARGET) / HY_SLOW_DECAY_PCT
    deltas = jnp.abs(jnp.linspace(min_decay, max_decay, HY_DIM, dtype=jnp.float32))
    window = jnp.exp(-t * deltas)
    return h * window[:, None, None, :]


def hyena_long_conv(u, h_fwd, h_bwd, skip):
    seq_len = u.shape[1]
    h_two = jnp.concatenate([h_fwd.at[0].add(h_bwd[0]), jnp.zeros_like(h_fwd[:1]), h_bwd[:0:-1]], axis=0)
    u_f = jnp.fft.rfft(u, n=2 * seq_len, axis=1)
    h_f = jnp.fft.rfft(h_two, n=2 * seq_len, axis=0)
    y = jnp.fft.irfft(u_f * h_f[None], n=2 * seq_len, axis=1)[:, :seq_len]
    return y + u * skip


def hyena_mixer(u, conv_w, conv_b, filt, skip):
    u = centred_dwconv(u, conv_w) + conv_b
    x1, x2, v = jnp.split(u, 3, axis=-1)
    z = v
    for o, gate in enumerate((x1, x2)):
        z = gate * hyena_long_conv(z, filt[:, o, 0], filt[:, o, 1], skip[o])
    return z


def gated_delta_chunked(q, k, v, beta, log_g):
    c = GDN_CHUNK
    q, k, v = (to_chunks(a, c) for a in (q, k, v))
    beta, log_g = (to_chunks(a, c) for a in (beta, log_g))
    dv = v.shape[-1]
    gam = jnp.cumsum(log_g, axis=-1)
    causal = jnp.tril(jnp.ones((c, c), dtype=bool))
    strict = jnp.tril(jnp.ones((c, c), dtype=bool), -1)
    decay = jnp.exp(jnp.where(causal, gam[..., :, None] - gam[..., None, :], -jnp.inf))
    kk = jnp.einsum('zbnhtd,zbnhsd->zbnhts', k, k)
    a_mat = jnp.where(strict, kk * decay * beta[..., :, None], 0.0)
    rhs = jnp.concatenate([v * beta[..., None], k * (beta * jnp.exp(gam))[..., None]], -1)
    sol = lax.linalg.triangular_solve(a_mat + jnp.eye(c, dtype=a_mat.dtype), rhs,
                                      left_side=True, lower=True, unit_diagonal=True)
    u, w = sol[..., :dv], sol[..., dv:]
    attn = jnp.einsum('zbnhtd,zbnhsd->zbnhts', q, k) * decay
    q_dec = q * jnp.exp(gam)[..., None]
    g_last = gam[..., -1]
    k_dec = k * jnp.exp(g_last[..., None] - gam)[..., None]

    def step(state, inp):
        q_c, a_c, u_c, w_c, k_c, g_c = inp
        v_new = u_c - jnp.einsum('zbhcd,zbhde->zbhce', w_c, state)
        out = jnp.einsum('zbhcd,zbhde->zbhce', q_c, state) + jnp.einsum('zbhts,zbhse->zbhte', a_c, v_new)
        state = state * jnp.exp(g_c)[..., None, None] + jnp.einsum('zbhcd,zbhce->zbhde', k_c, v_new)
        return state, out

    xs = tuple(jnp.moveaxis(a, 2, 0) for a in (q_dec, attn, u, w, k_dec, g_last))
    z, b, _, h, _, dk = q.shape
    state0 = jnp.zeros((z, b, h, dk, dv), q.dtype)
    _, out = lax.scan(step, state0, xs)
    return from_chunks(jnp.moveaxis(out, 0, 2))


def gdn_mixer(q, k, v, zgate, a, b, conv_w, a_log, dt_bias, norm_w):
    bsz, seq = q.shape[:2]
    qkv = jax.nn.silu(centred_dwconv(jnp.concatenate([q, k, v], -1), conv_w))
    q, k, v = split_cols(qkv, (GDN_HEADS * GDN_DK, GDN_HEADS * GDN_DK, GDN_HEADS * GDN_DV))
    q = l2_normalize(q.reshape(bsz, seq, GDN_HEADS, GDN_DK)) * GDN_DK ** -0.5
    k = l2_normalize(k.reshape(bsz, seq, GDN_HEADS, GDN_DK))
    v = v.reshape(bsz, seq, GDN_HEADS, GDN_DV)
    beta = jax.nn.sigmoid(b)
    log_g = -jnp.exp(a_log) * jax.nn.softplus(a.reshape(bsz, seq, 2, GDN_HEADS) + dt_bias)
    log_g = jnp.stack([log_g[:, :, 0], jnp.flip(log_g[:, :, 1], 1)], 0)
    o = gated_delta_chunked(both_directions(q), both_directions(k), both_directions(v),
                            both_directions(beta), log_g)
    o = merge_directions(o)
    o = rms_norm(o, norm_w) * jax.nn.silu(zgate.reshape(bsz, seq, GDN_HEADS, GDN_DV))
    return o.reshape(bsz, seq, GDN_HEADS * GDN_DV)


def hgrn2_chunked(q, k, v, log_f):
    c = HG_CHUNK
    q, k, v, log_f = (to_chunks(a, c) for a in (q, k, v, log_f))
    gam = jnp.cumsum(log_f, axis=-2)
    causal = jnp.tril(jnp.ones((c, c), dtype=bool))[..., None]

    def step(state, inp):
        q_c, k_c, v_c, g_c = inp
        inter = jnp.einsum('zbhcd,zbhde->zbhce', q_c * jnp.exp(g_c), state)
        dec = jnp.exp(jnp.where(causal, g_c[..., :, None, :] - g_c[..., None, :, :], -jnp.inf))
        attn = jnp.einsum('zbhtd,zbhsd,zbhtsd->zbhts', q_c, k_c, dec)
        out = inter + jnp.einsum('zbhts,zbhse->zbhte', attn, v_c)
        g_last = g_c[..., -1:, :]
        state = (state * jnp.exp(g_last)[..., 0, :, None]
                 + jnp.einsum('zbhcd,zbhce->zbhde', k_c * jnp.exp(g_last - g_c), v_c))
        return state, out

    xs = tuple(jnp.moveaxis(a, 2, 0) for a in (q, k, v, gam))
    z, b, _, h, _, dk = q.shape
    state0 = jnp.zeros((z, b, h, dk, v.shape[-1]), q.dtype)
    _, out = lax.scan(step, state0, xs)
    return from_chunks(jnp.moveaxis(out, 0, 2))


def hgrn2_mixer(q, f, i, g, lower_bound, norm_w):
    bsz, seq = q.shape[:2]

    def heads(a, d):
        return a.reshape(bsz, seq, HG_HEADS, d)

    q = heads(jax.nn.silu(q), HG_DK)
    fg = lower_bound + (1.0 - lower_bound) * jax.nn.sigmoid(f.reshape(bsz, seq, 2, HG_HEADS * HG_DK))
    fg = fg.reshape(bsz, seq, 2, HG_HEADS, HG_DK)
    fz = jnp.stack([fg[:, :, 0], jnp.flip(fg[:, :, 1], 1)], 0)
    o = hgrn2_chunked(both_directions(q), 1.0 - fz, both_directions(heads(i, HG_DV)), jnp.log(fz))
    o = merge_directions(o)
    o = rms_norm(o, norm_w) * jax.nn.silu(heads(g, HG_DV))
    return o.reshape(bsz, seq, HG_HEADS * HG_DV)


def rwkv7_mixer(u, mu, w0, w2, a0, a2, g2, k_k, k_a, r_k, lnx_w, lnx_b):
    bsz, seq = u.shape[:2]
    prev = jnp.pad(u, ((0, 0), (1, 0), (0, 0)))[:, :-1]
    nxt = jnp.pad(u, ((0, 0), (0, 1), (0, 0)))[:, 1:]
    u = u + mu[0] * (prev - u) + mu[1] * (nxt - u)
    r, k, v, w_lo, a_lo, g_lo = split_cols(u, RW_SPLITS)
    w_raw = w0[:, None, None, :] + jnp.einsum('btr,zrc->zbtc', jnp.tanh(w_lo), w2)
    log_w = -jnp.exp(-jax.nn.softplus(-w_raw) - 0.5)
    a = jax.nn.sigmoid(a0 + a_lo @ a2)
    g = jax.nn.sigmoid(g_lo) @ g2

    def heads(t):
        return t.reshape(t.shape[:-1] + (RW_HEADS, RW_HD))

    kk = l2_normalize(heads(k * k_k))
    k = k * (1.0 + (a - 1.0) * k_a)
    r, k, v, a = heads(r), heads(k), heads(v), heads(a)
    decay = jnp.exp(heads(log_w))
    decay = jnp.stack([decay[0], jnp.flip(decay[1], 1)], 0)
    xs = (both_directions(r), decay, both_directions(k), both_directions(v),
          both_directions(kk), both_directions(kk * a))
    xs = tuple(jnp.moveaxis(t, 2, 0) for t in xs)

    def step(state, inp):
        r_t, w_t, k_t, v_t, kk_t, kka_t = inp
        sa = jnp.einsum('zbhvk,zbhk->zbhv', state, kk_t)
        state = (state * w_t[..., None, :] - sa[..., :, None] * kka_t[..., None, :]
                 + v_t[..., :, None] * k_t[..., None, :])
        return state, jnp.einsum('zbhvk,zbhk->zbhv', state, r_t)

    state0 = jnp.zeros((2, bsz, RW_HEADS, RW_HD, RW_HD), r.dtype)
    _, y = lax.scan(step, state0, xs)
    y = merge_directions(jnp.moveaxis(y, 0, 2))
    yc = y - jnp.mean(y, -1, keepdims=True)
    y = yc * lax.rsqrt(jnp.mean(yc * yc, -1, keepdims=True) + RW_GN_EPS)
    y = y.reshape(bsz, seq, RW_DIM) * lnx_w + lnx_b
    bonus = jnp.sum(r * k * r_k, -1, keepdims=True) * v
    return (y + bonus.reshape(bsz, seq, RW_DIM)) * g


def even_layer_mixer(x, w_in, hy_conv_w, hy_conv_b, f_w1, f_b1, f_w2, f_b2, f_w3, f_b3, f_w4, f_freq,
                     hy_skip, gdn_conv_w, gdn_a_log, gdn_dt_bias, gdn_norm_w, w_out):
    proj = (x @ w_in).astype(jnp.float32)
    hy_in, q, k, v, zg, a, b = split_cols(proj, EVEN_SPLITS)
    filt = hyena_filters(x.shape[1], f_w1, f_b1, f_w2, f_b2, f_w3, f_b3, f_w4, f_freq)
    y_a = hyena_mixer(hy_in, hy_conv_w, hy_conv_b, filt, hy_skip)
    y_b = gdn_mixer(q, k, v, zg, a, b, gdn_conv_w, gdn_a_log, gdn_dt_bias, gdn_norm_w)
    return jnp.concatenate([y_a, y_b], -1).astype(x.dtype) @ w_out


def odd_layer_mixer(x, w_in, lower_bound, hg_norm_w, rw_mu, rw_w0, rw_w2, rw_a0, rw_a2, rw_g2,
                    rw_k_k, rw_k_a, rw_r_k, rw_lnx_w, rw_lnx_b, w_out):
    proj = (x @ w_in).astype(jnp.float32)
    q, f, i, g, rw_in = split_cols(proj, ODD_SPLITS)
    y_c = hgrn2_mixer(q, f, i, g, lower_bound, hg_norm_w)
    y_d = rwkv7_mixer(rw_in, rw_mu, rw_w0, rw_w2, rw_a0, rw_a2, rw_g2, rw_k_k, rw_k_a, rw_r_k,
                      rw_lnx_w, rw_lnx_b)
    return jnp.concatenate([y_c, y_d], -1).astype(x.dtype) @ w_out


def sq_relu_mlp(x, w1, w2):
    return jnp.square(jax.nn.relu(x @ w1)) @ w2


def setup_inputs(seed: int = 0) -> dict:
    key = jax.random.key(seed)
    keys = list(jax.random.split(key, 48))

    def nrm(shape, scale):
        return jax.random.normal(keys.pop(), shape, jnp.float32) * scale

    def uni(shape, lo, hi):
        return jax.random.uniform(keys.pop(), shape, jnp.float32, lo, hi)

    beta = DEEPNORM_BETA
    ev_scale = jnp.concatenate([
        jnp.ones((2 * HY_DIM,)), jnp.full((HY_DIM,), beta),
        jnp.ones((2 * GDN_HEADS * GDN_DK,)), jnp.full((GDN_HEADS * GDN_DV,), beta),
        jnp.ones((GDN_HEADS * GDN_DV + 3 * GDN_HEADS,))])
    od_scale = jnp.concatenate([
        jnp.ones((3 * HG_HEADS * HG_DK,)), jnp.full((HG_HEADS * HG_DV,), beta),
        jnp.ones((HG_HEADS * HG_DV + 2 * RW_DIM,)), jnp.full((RW_DIM,), beta),
        jnp.ones((RW_W_LORA + RW_A_LORA + RW_G_LORA,))])
    dt = jnp.exp(uni((N_EVEN, 2, GDN_HEADS), math.log(1e-3), math.log(1e-1)))
    return {
        'x': nrm((BATCH, SEQ, D_MODEL), 1.0),
        'ev_w_in': nrm((N_EVEN, D_MODEL, EVEN_IN), D_MODEL ** -0.5) * ev_scale,
        'hy_conv_w': nrm((N_EVEN, HY_SHORT, 3 * HY_DIM), HY_SHORT ** -0.5),
        'hy_conv_b': nrm((N_EVEN, 3 * HY_DIM), 0.02),
        'hy_filt_w1': nrm((N_EVEN, HY_EMB, HY_FILT_HID), HY_EMB ** -0.5),
        'hy_filt_b1': nrm((N_EVEN, HY_FILT_HID), 0.1),
        'hy_filt_w2': nrm((N_EVEN, HY_FILT_HID, HY_FILT_HID), HY_FILT_HID ** -0.5),
        'hy_filt_b2': nrm((N_EVEN, HY_FILT_HID), 0.1),
        'hy_filt_w3': nrm((N_EVEN, HY_FILT_HID, HY_FILT_HID), HY_FILT_HID ** -0.5),
        'hy_filt_b3': nrm((N_EVEN, HY_FILT_HID), 0.1),
        'hy_filt_w4': nrm((N_EVEN, HY_FILT_HID, HY_ORDER * 2 * HY_DIM), 0.1 * HY_FILT_HID ** -0.5),
        'hy_filt_freq': 1.0 + nrm((N_EVEN, HY_FILT_HID), 0.1),
        'hy_skip': nrm((N_EVEN, HY_ORDER, HY_DIM), 0.5),
        'gdn_conv_w': nrm((N_EVEN, GDN_CONV, 2 * GDN_HEADS * GDN_DK + GDN_HEADS * GDN_DV), GDN_CONV ** -0.5),
        'gdn_a_log': jnp.log(uni((N_EVEN, 2, GDN_HEADS), 1.0, 16.0)),
        'gdn_dt_bias': dt + jnp.log(-jnp.expm1(-dt)),
        'gdn_norm_w': 1.0 + nrm((N_EVEN, GDN_DV), 0.1),
        'ev_w_out': nrm((N_EVEN, MIX_WIDTH, D_MODEL), MIX_WIDTH ** -0.5 * beta),
        'od_w_in': nrm((N_ODD, D_MODEL, ODD_IN), D_MODEL ** -0.5) * od_scale,
        'hg_lower': nrm((DEPTH, HG_HEADS * HG_DK), 1.0),
        'hg_norm_w': 1.0 + nrm((N_ODD, HG_DV), 0.1),
        'rw_mu': uni((N_ODD, 2, RW_IN), 0.0, 0.5),
        'rw_w0': uni((N_ODD, 2, RW_DIM), -6.0, -1.0),
        'rw_w2': nrm((N_ODD, 2, RW_W_LORA, RW_DIM), 0.1 * RW_W_LORA ** -0.5),
        'rw_a0': nrm((N_ODD, RW_DIM), 0.1),
        'rw_a2': nrm((N_ODD, RW_A_LORA, RW_DIM), RW_A_LORA ** -0.5),
        'rw_g2': nrm((N_ODD, RW_G_LORA, RW_DIM), RW_G_LORA ** -0.5),
        'rw_k_k': 0.85 + nrm((N_ODD, RW_DIM), 0.05),
        'rw_k_a': 1.0 + nrm((N_ODD, RW_DIM), 0.05),
        'rw_r_k': nrm((N_ODD, RW_HEADS, RW_HD), 0.1),
        'rw_lnx_w': 1.0 + nrm((N_ODD, RW_DIM), 0.1),
        'rw_lnx_b': nrm((N_ODD, RW_DIM), 0.05),
        'od_w_out': nrm((N_ODD, MIX_WIDTH, D_MODEL), MIX_WIDTH ** -0.5 * beta),
        'ln_g': 1.0 + nrm((DEPTH, 2, D_MODEL), 0.05),
        'ln_b': nrm((DEPTH, 2, D_MODEL), 0.02),
        'mlp_w1': nrm((DEPTH, D_MODEL, D_FF), D_MODEL ** -0.5 * beta),
        'mlp_w2': nrm((DEPTH, D_FF, D_MODEL), D_FF ** -0.5 * beta),
    }


def reference(x, ev_w_in, hy_conv_w, hy_conv_b, hy_filt_w1, hy_filt_b1, hy_filt_w2, hy_filt_b2,
              hy_filt_w3, hy_filt_b3, hy_filt_w4, hy_filt_freq, hy_skip, gdn_conv_w, gdn_a_log,
              gdn_dt_bias, gdn_norm_w, ev_w_out, od_w_in, hg_lower, hg_norm_w, rw_mu, rw_w0, rw_w2,
              rw_a0, rw_a2, rw_g2, rw_k_k, rw_k_a, rw_r_k, rw_lnx_w, rw_lnx_b, od_w_out, ln_g, ln_b,
              mlp_w1, mlp_w2):
    lb_all = jnp.cumsum(jax.nn.softmax(hg_lower.astype(jnp.float32), axis=0), axis=0)
    lb_all = lb_all - lb_all[0]
    for layer in range(DEPTH):
        idx = layer // 2
        if layer % 2 == 0:
            mix = even_layer_mixer(x, ev_w_in[idx], hy_conv_w[idx], hy_conv_b[idx], hy_filt_w1[idx],
                                   hy_filt_b1[idx], hy_filt_w2[idx], hy_filt_b2[idx], hy_filt_w3[idx],
                                   hy_filt_b3[idx], hy_filt_w4[idx], hy_filt_freq[idx], hy_skip[idx],
                                   gdn_conv_w[idx], gdn_a_log[idx], gdn_dt_bias[idx], gdn_norm_w[idx],
                                   ev_w_out[idx])
        else:
            mix = odd_layer_mixer(x, od_w_in[idx], lb_all[layer], hg_norm_w[idx], rw_mu[idx], rw_w0[idx],
                                  rw_w2[idx], rw_a0[idx], rw_a2[idx], rw_g2[idx], rw_k_k[idx], rw_k_a[idx],
                                  rw_r_k[idx], rw_lnx_w[idx], rw_lnx_b[idx], od_w_out[idx])
        x = layer_norm(DEEPNORM_ALPHA * x + mix, ln_g[layer, 0], ln_b[layer, 0])
        x = layer_norm(DEEPNORM_ALPHA * x + sq_relu_mlp(x, mlp_w1[layer], mlp_w2[layer]),
                       ln_g[layer, 1], ln_b[layer, 1])
    return x
```

```python
import functools
import math

import numpy as np
import jax
import jax.numpy as jnp
from jax import lax
from jax.experimental import pallas as pl
from jax.experimental.pallas import tpu as pltpu

F32 = jnp.float32
BF16 = jnp.bfloat16

D_MODEL = 1024
DEPTH = 2
D_FF = 4 * D_MODEL
DEEPNORM_ALPHA = (2.0 * DEPTH) ** 0.25
LN_EPS = 1e-5
RMS_EPS = 1e-6
L2_EPS = 1e-6

HY_DIM = 512
HY_EMB = 33
HY_BANDS = 16
HY_FILT_HID = 64
HY_DECAY_TARGET = 1e-2
HY_FAST_DECAY_PCT = 0.3
HY_SLOW_DECAY_PCT = 1.5

GDN_HEADS = 4
GDN_DK = 128
GDN_CHUNK = 64

HG_HEADS = 4
HG_DK = 128
HG_CHUNK = 16

RW_HEADS = 8
RW_HD = 64
RW_DIM = 512
RW_CHUNK = 64
RW_GN_EPS = 64e-5

LANES = 128
SUBLANES = 8
VMEM_LIMIT = 56 * 1024 * 1024

_NN = (((1,), (0,)), ((), ()))
_NT = (((1,), (1,)), ((), ()))
_TN = (((0,), (0,)), ((), ()))


def _mm(a, b, dims=_NN, exact=False):
    if exact:
        return lax.dot_general(a, b, dims, precision=lax.Precision.HIGHEST,
                               preferred_element_type=F32)
    return lax.dot_general(a.astype(BF16), b.astype(BF16), dims, preferred_element_type=F32)


def _params(*sem):
    return pltpu.CompilerParams(dimension_semantics=sem, vmem_limit_bytes=VMEM_LIMIT)


def _sigmoid(x):
    return 1.0 / (1.0 + jnp.exp(-x))


def _silu(x):
    return x * _sigmoid(x)


def _softplus(x):
    return jnp.maximum(x, 0.0) + jnp.log(1.0 + jnp.exp(-jnp.abs(x)))


def _layer_norm(v, g, b):
    mean = jnp.mean(v, axis=-1, keepdims=True)
    vc = v - mean
    var = jnp.mean(vc * vc, axis=-1, keepdims=True)
    return vc * lax.rsqrt(var + LN_EPS) * g + b


def _col(x, idx, lane):
    return jnp.sum(jnp.where(lane == idx, x, 0.0), axis=1, keepdims=True)


def _unit_tri_inv(a, eye):
    c = a.shape[0]
    p = -a
    t = eye + p
    k = 2
    while k < c:
        p = _mm(p, p, exact=True)
        t = t + _mm(t, p, exact=True)
        k *= 2
    return t


def _proj_kernel(x_ref, w_ref, o_ref):
    o_ref[...] = _mm(x_ref[...], w_ref[...])


def _proj_t_kernel(x_ref, w_ref, wt_ref, o_ref, ot_ref):
    xb = x_ref[...].astype(BF16)
    o_ref[...] = _mm(xb, w_ref[...])
    ot_ref[...] = _mm(wt_ref[...], xb, _NT)


def _project(x2, w, wt=None, tm=256):
    n, d = x2.shape
    m = w.shape[1]
    grid = (n // tm,)
    x_spec = pl.BlockSpec((tm, d), lambda i: (i, 0))
    w_spec = pl.BlockSpec((d, m), lambda i: (0, 0))
    o_spec = pl.BlockSpec((tm, m), lambda i: (i, 0))
    if wt is None:
        return pl.pallas_call(
            _proj_kernel, grid=grid, in_specs=[x_spec, w_spec], out_specs=o_spec,
            out_shape=jax.ShapeDtypeStruct((n, m), F32), compiler_params=_params("parallel"),
            name="proj")(x2, w)
    mt = wt.shape[0]
    return pl.pallas_call(
        _proj_t_kernel, grid=grid,
        in_specs=[x_spec, w_spec, pl.BlockSpec((mt, d), lambda i: (0, 0))],
        out_specs=[o_spec, pl.BlockSpec((mt, tm), lambda i: (0, i))],
        out_shape=[jax.ShapeDtypeStruct((n, m), F32), jax.ShapeDtypeStruct((mt, n), F32)],
        compiler_params=_params("parallel"), name="proj_t")(x2, w, wt)


def _mlp_kernel(x_ref, w1_ref, w2_ref, g_ref, b_ref, o_ref, acc_ref):
    j = pl.program_id(1)
    x = x_ref[...]
    h = jnp.maximum(_mm(x, w1_ref[...]), 0.0)
    part = _mm(h * h, w2_ref[...])

    @pl.when(j == 0)
    def _():
        acc_ref[...] = part

    @pl.when(j > 0)
    def _():
        acc_ref[...] += part

    @pl.when(j == pl.num_programs(1) - 1)
    def _():
        o_ref[...] = _layer_norm(DEEPNORM_ALPHA * x + acc_ref[...], g_ref[...], b_ref[...])


def _mlp_block(x2, w1, w2, g, b, tm=512, tf=512):
    n, d = x2.shape
    f = w1.shape[1]
    return pl.pallas_call(
        _mlp_kernel, grid=(n // tm, f // tf),
        in_specs=[pl.BlockSpec((tm, d), lambda i, j: (i, 0)),
                  pl.BlockSpec((d, tf), lambda i, j: (0, j)),
                  pl.BlockSpec((tf, d), lambda i, j: (j, 0)),
                  pl.BlockSpec((1, d), lambda i, j: (0, 0)),
                  pl.BlockSpec((1, d), lambda i, j: (0, 0))],
        out_specs=pl.BlockSpec((tm, d), lambda i, j: (i, 0)),
        out_shape=jax.ShapeDtypeStruct((n, d), F32),
        scratch_shapes=[pltpu.VMEM((tm, d), F32)],
        compiler_params=_params("parallel", "arbitrary"), name="mlp")(x2, w1, w2, g, b)


def _fill_halo(scr, cur, prev8, next8, first, last, halo):
    tb = cur.shape[0]
    scr[pl.ds(SUBLANES, tb), :] = cur
    scr[pl.ds(0, SUBLANES), :] = jnp.where(first, 0.0, prev8)
    scr[pl.ds(SUBLANES + tb, SUBLANES), :] = jnp.where(last, 0.0, next8)


def _halo_specs(width, col_block, tb, t_len, bsz):
    nb = t_len // tb
    r8 = tb // SUBLANES
    tot8 = t_len // SUBLANES

    def cur(b, i):
        return (b * nb + i, col_block)

    def prev(b, i):
        return (jnp.maximum(b * tot8 + i * r8 - 1, 0), col_block)

    def nxt(b, i):
        return (jnp.minimum(b * tot8 + (i + 1) * r8, tot8 * bsz - 1), col_block)

    return [pl.BlockSpec((tb, width), cur), pl.BlockSpec((SUBLANES, width), prev),
            pl.BlockSpec((SUBLANES, width), nxt)]


def _gdn_pre_kernel(cur_ref, prev_ref, next_ref, ab_ref, cw_ref, alog_ref, dtb_ref,
                    q_ref, k_ref, v_ref, g_ref, scr):
    i = pl.program_id(1)
    tb = cur_ref.shape[0]
    _fill_halo(scr, cur_ref[...], prev_ref[...], next_ref[...], i == 0,
               i == pl.num_programs(1) - 1, 2)
    width = cw_ref.shape[0]
    acc = None
    for j in range(width):
        term = scr[pl.ds(SUBLANES - width // 2 + j, tb), :] * cw_ref[pl.ds(j, 1), :]
        acc = term if acc is None else acc + term
    qkv = _silu(acc)
    hd = GDN_HEADS * GDN_DK
    for h in range(GDN_HEADS):
        sl = slice(h * GDN_DK, (h + 1) * GDN_DK)
        q = qkv[:, sl]
        q_ref[:, sl] = q * lax.rsqrt(jnp.sum(q * q, -1, keepdims=True) + L2_EPS) * GDN_DK ** -0.5
        k = qkv[:, hd + h * GDN_DK: hd + (h + 1) * GDN_DK]
        k_ref[:, sl] = k * lax.rsqrt(jnp.sum(k * k, -1, keepdims=True) + L2_EPS)
    v_ref[...] = qkv[:, 2 * hd:]
    ab = ab_ref[...]
    lane = lax.broadcasted_iota(jnp.int32, ab.shape, 1)
    log_g = -jnp.exp(alog_ref[...]) * _softplus(ab + dtb_ref[...])
    g_ref[...] = jnp.where(lane < 2 * GDN_HEADS, log_g, _sigmoid(ab))


def _gdn_pre(proj, conv_w, a_log_row, dt_bias_row, bsz, t_len, tb=256):
    n = proj.shape[0]
    wq = 3 * GDN_HEADS * GDN_DK
    nb = t_len // tb
    specs = _halo_specs(wq, 0, tb, t_len, bsz)
    ab_col = (wq + GDN_HEADS * GDN_DK) // LANES
    out_spec = pl.BlockSpec((tb, 512), lambda b, i: (b * nb + i, 0))
    return pl.pallas_call(
        _gdn_pre_kernel, grid=(bsz, nb),
        in_specs=specs + [pl.BlockSpec((tb, LANES), lambda b, i: (b * nb + i, ab_col)),
                          pl.BlockSpec(conv_w.shape, lambda b, i: (0, 0)),
                          pl.BlockSpec((1, LANES), lambda b, i: (0, 0)),
                          pl.BlockSpec((1, LANES), lambda b, i: (0, 0))],
        out_specs=[out_spec, out_spec, out_spec,
                   pl.BlockSpec((tb, LANES), lambda b, i: (b * nb + i, 0))],
        out_shape=[jax.ShapeDtypeStruct((n, 512), F32)] * 3 + [jax.ShapeDtypeStruct((n, LANES), F32)],
        scratch_shapes=[pltpu.VMEM((tb + 2 * SUBLANES, wq), F32)],
        compiler_params=_params("parallel", "parallel"), name="gdn_pre",
    )(proj, proj, proj, proj, conv_w, a_log_row, dt_bias_row)


def _gdn_scan_kernel(qf, kf, vf, gf, qb, kb, vb, gb, of, ob, state):
    n = pl.program_id(1)

    @pl.when(n == 0)
    def _():
        state[...] = jnp.zeros_like(state)

    c = GDN_CHUNK
    row = lax.broadcasted_iota(jnp.int32, (c, c), 0)
    col = lax.broadcasted_iota(jnp.int32, (c, c), 1)
    lane = lax.broadcasted_iota(jnp.int32, (c, LANES), 1)
    eye = (row == col).astype(F32)
    dirs = ((qf, kf, vf, gf, of), (qb, kb, vb, gb, ob))
    for d, (q_ref, k_ref, v_ref, g_ref, o_ref) in enumerate(dirs):
        incl = (col <= row) if d == 0 else (col >= row)
        strict = (col < row) if d == 0 else (col > row)
        gates = g_ref[...]
        gam_all = _mm(incl.astype(F32), gates, exact=True)
        for h in range(GDN_HEADS):
            sl = slice(h * GDN_DK, (h + 1) * GDN_DK)
            q, k, v = q_ref[:, sl], k_ref[:, sl], v_ref[:, sl]
            gam = _col(gam_all, d * GDN_HEADS + h, lane)
            beta = _col(gates, 2 * GDN_HEADS + h, lane)
            g_last = gam[c - 1:c, :] if d == 0 else gam[0:1, :]
            g1 = jnp.where(lane == 0, gam, jnp.where(lane == 1, 1.0, 0.0))
            g2 = jnp.where(lane == 0, 1.0, jnp.where(lane == 1, -gam, 0.0))
            diff = _mm(g1, g2, _NT, exact=True)
            decay = jnp.where(incl, jnp.exp(jnp.where(incl, diff, 0.0)), 0.0)
            e_gam = jnp.exp(gam)
            a_mat = jnp.where(strict, _mm(k, k, _NT) * decay * beta, 0.0)
            t_inv = _unit_tri_inv(a_mat, eye)
            rhs = jnp.concatenate([v * beta, k * (beta * e_gam)], axis=1)
            sol = _mm(t_inv, rhs, exact=True)
            u, w = sol[:, :GDN_DK], sol[:, GDN_DK:]
            attn = _mm(q, k, _NT) * decay
            s = state[d, h]
            v_new = u - _mm(w, s)
            o_ref[:, sl] = _mm(q * e_gam, s) + _mm(attn, v_new)
            k_dec = k * jnp.exp(g_last - gam)
            state[d, h] = s * jnp.exp(g_last) + _mm(k_dec, v_new, _TN)


def _gdn_scan(q, k, v, gates, bsz, t_len):
    n = q.shape[0]
    c = GDN_CHUNK
    nc = t_len // c

    def fwd(b, i):
        return (b * nc + i, 0)

    def bwd(b, i):
        return (b * nc + nc - 1 - i, 0)

    def specs(idx):
        wide = pl.BlockSpec((c, 512), idx)
        return [wide, wide, wide, pl.BlockSpec((c, LANES), idx)]

    return pl.pallas_call(
        _gdn_scan_kernel, grid=(bsz, nc),
        in_specs=specs(fwd) + specs(bwd),
        out_specs=[pl.BlockSpec((c, 512), fwd), pl.BlockSpec((c, 512), bwd)],
        out_shape=[jax.ShapeDtypeStruct((n, 512), F32)] * 2,
        scratch_shapes=[pltpu.VMEM((2, GDN_HEADS, GDN_DK, GDN_DK), F32)],
        compiler_params=_params("parallel", "arbitrary"), name="gdn_scan",
    )(q, k, v, gates, q, k, v, gates)


HY_FFT_EXACT = True


def _hy_filter_kernel(z_ref, w1_ref, b1_ref, w2_ref, b2_ref, w3_ref, b3_ref, fr_ref, w4t_ref,
                      dl_ref, o_ref, *, t_len):
    i = pl.program_id(0)
    tl = o_ref.shape[1]
    fr = fr_ref[...]
    h = jnp.sin(fr * (_mm(z_ref[...], w1_ref[...], exact=True) + b1_ref[...]))
    h = jnp.sin(fr * (_mm(h, w2_ref[...], exact=True) + b2_ref[...]))
    h = jnp.sin(fr * (_mm(h, w3_ref[...], exact=True) + b3_ref[...]))
    ht = _mm(w4t_ref[...], h, _NT, exact=True)
    pos = (lax.broadcasted_iota(jnp.int32, (1, tl), 1) + i * tl).astype(F32)
    t_norm = pos / float(max(t_len - 1, 1))
    o_ref[...] = ht * jnp.exp(-t_norm * dl_ref[...])


def _hy_filters(z, w1p, b1, w2, b2, w3, b3, freq, w4t, deltas, t_len):
    tl = min(512, t_len)
    nch = w4t.shape[0]
    full = lambda a: pl.BlockSpec(a.shape, lambda i: (0,) * a.ndim)
    return pl.pallas_call(
        functools.partial(_hy_filter_kernel, t_len=t_len), grid=(t_len // tl,),
        in_specs=[pl.BlockSpec((tl, z.shape[1]), lambda i: (i, 0))]
        + [full(a) for a in (w1p, b1, w2, b2, w3, b3, freq, w4t, deltas)],
        out_specs=pl.BlockSpec((nch, tl), lambda i: (0, i)),
        out_shape=jax.ShapeDtypeStruct((nch, t_len), F32),
        compiler_params=_params("parallel"), name="hy_filter",
    )(z, w1p, b1, w2, b2, w3, b3, freq, w4t, deltas)


def _fft_fwd(x3, fst, twr, twi, wbig):
    cb = x3.shape[0]
    n1 = twr.shape[0]
    ys = [_mm(fst, x3[c], exact=HY_FFT_EXACT) for c in range(cb)]
    yr = jnp.stack([y[:n1] for y in ys])
    yi = jnp.stack([y[n1:] for y in ys])
    tr = yr * twr - yi * twi
    ti = yr * twi + yi * twr
    cat = jnp.concatenate([tr, ti], axis=-1).reshape(cb * n1, 2 * LANES)
    z = _mm(cat, wbig, exact=HY_FFT_EXACT).reshape(cb, n1, 2 * LANES)
    return z[..., :LANES], z[..., LANES:]


def _fft_inv(zr, zi, gst, twr, twi, wbigc):
    cb, n1 = zr.shape[0], zr.shape[1]
    cat = jnp.concatenate([zr, zi], axis=-1).reshape(cb * n1, 2 * LANES)
    a = _mm(cat, wbigc, exact=HY_FFT_EXACT).reshape(cb, n1, 2 * LANES)
    ar, ai = a[..., :LANES], a[..., LANES:]
    tr = ar * twr + ai * twi
    ti = ai * twr - ar * twi
    outs = [_mm(gst, jnp.concatenate([tr[c], ti[c]], axis=0), exact=HY_FFT_EXACT)
            for c in range(cb)]
    return jnp.stack(outs)


def _hy_spec_kernel(hf_ref, hb_ref, fst_ref, twr_ref, twi_ref, wbig_ref, hr_ref, hi_ref):
    args = (fst_ref[...], twr_ref[...], twi_ref[...], wbig_ref[...])
    fr, fi = _fft_fwd(hf_ref[...], *args)
    br, bi = _fft_fwd(hb_ref[...], *args)
    hr_ref[...] = fr + br
    hi_ref[...] = fi - bi


def _hy_spectra(filt3, consts, cb=8):
    fst, gst, twr, twi, wbig, wbigc = consts
    r = filt3.shape[1]
    n1 = twr.shape[0]
    nj = HY_DIM // cb
    full = lambda a: pl.BlockSpec(a.shape, lambda o, j: (0,) * a.ndim)
    out_spec = pl.BlockSpec((cb, n1, LANES), lambda o, j: (o * nj + j, 0, 0))
    return pl.pallas_call(
        _hy_spec_kernel, grid=(2, nj),
        in_specs=[pl.BlockSpec((cb, r, LANES), lambda o, j: (2 * o * nj + j, 0, 0)),
                  pl.BlockSpec((cb, r, LANES), lambda o, j: ((2 * o + 1) * nj + j, 0, 0)),
                  full(fst), full(twr), full(twi), full(wbig)],
        out_specs=[out_spec, out_spec],
        out_shape=[jax.ShapeDtypeStruct((2 * HY_DIM, n1, LANES), F32)] * 2,
        compiler_params=_params("parallel", "parallel"), name="hy_spec",
    )(filt3, filt3, fst, twr, twi, wbig)


def _tshift(x3, back, lane, rowi):
    r = x3.shape[1]
    if back:
        rl = pltpu.roll(x3, 1, 2)
        rr = pltpu.roll(rl, 1, 1)
        edge = lane == 0
        out = jnp.where(edge, rr, rl)
        return jnp.where(edge & (rowi == 0), 0.0, out)
    rl = pltpu.roll(x3, LANES - 1, 2)
    rr = pltpu.roll(rl, r - 1, 1)
    edge = lane == LANES - 1
    out = jnp.where(edge, rr, rl)
    return jnp.where(edge & (rowi == r - 1), 0.0, out)


def _hy_conv_kernel(x1_ref, x2_ref, v_ref, p1_ref, p2_ref, pv_ref, sk_ref, h1r_ref, h1i_ref,
                    h2r_ref, h2i_ref, fst_ref, gst_ref, twr_ref, twi_ref, wbig_ref, wbigc_ref,
                    o_ref):
    shape = x1_ref.shape
    lane = lax.broadcasted_iota(jnp.int32, shape, 2)
    rowi = lax.broadcasted_iota(jnp.int32, shape, 1)

    def short_conv(x_ref, p_ref):
        x = x_ref[...]
        p = p_ref[...]
        return (p[:, 0:1, :] * _tshift(x, True, lane, rowi) + p[:, 1:2, :] * x
                + p[:, 2:3, :] * _tshift(x, False, lane, rowi) + p[:, 3:4, :])

    x1 = short_conv(x1_ref, p1_ref)
    x2 = short_conv(x2_ref, p2_ref)
    z = short_conv(v_ref, pv_ref)
    sk = sk_ref[...]
    fwd = (fst_ref[...], twr_ref[...], twi_ref[...], wbig_ref[...])
    inv = (gst_ref[...], twr_ref[...], twi_ref[...], wbigc_ref[...])
    for o, (gate, hr_ref, hi_ref) in enumerate(((x1, h1r_ref, h1i_ref), (x2, h2r_ref, h2i_ref))):
        zr, zi = _fft_fwd(z, *fwd)
        hr, hi = hr_ref[...], hi_ref[...]
        y = _fft_inv(zr * hr - zi * hi, zr * hi + zi * hr, *inv)
        z = gate * (y + sk[:, o:o + 1, :] * z)
    o_ref[...] = z


def _hy_conv(projt4, pp, ps, hr, hi, consts, cb=8):
    fst, gst, twr, twi, wbig, wbigc = consts
    _, bsz, r, _ = projt4.shape
    n1 = twr.shape[0]
    nj = HY_DIM // cb
    full = lambda a: pl.BlockSpec(a.shape, lambda j, b: (0,) * a.ndim)

    def xs(off):
        return pl.BlockSpec((cb, None, r, LANES), lambda j, b: (off * nj + j, b, 0, 0))

    def ps_(off):
        return pl.BlockSpec((cb, SUBLANES, LANES), lambda j, b: (off * nj + j, 0, 0))

    def hs(off):
        return pl.BlockSpec((cb, n1, LANES), lambda j, b: (off * nj + j, 0, 0))

    return pl.pallas_call(
        _hy_conv_kernel, grid=(nj, bsz),
        in_specs=[xs(0), xs(1), xs(2), ps_(0), ps_(1), ps_(2), ps_(0), hs(0), hs(0), hs(1), hs(1),
                  full(fst), full(gst), full(twr), full(twi), full(wbig), full(wbigc)],
        out_specs=pl.BlockSpec((cb, None, r, LANES), lambda j, b: (j, b, 0, 0)),
        out_shape=jax.ShapeDtypeStruct((HY_DIM, bsz, r, LANES), F32),
        compiler_params=_params("parallel", "arbitrary"), name="hy_conv",
    )(projt4, projt4, projt4, pp, pp, pp, ps, hr, hi, hr, hi, fst, gst, twr, twi, wbig, wbigc)


def _fft_consts(t_len):
    n = 2 * t_len
    n1 = n // LANES
    r = t_len // LANES
    k1 = np.arange(n1)[:, None]
    th1 = 2.0 * np.pi * k1 * np.arange(r)[None, :] / n1
    fst = np.concatenate([np.cos(th1), -np.sin(th1)], axis=0)
    gst = np.concatenate([np.cos(th1).T, -np.sin(th1).T], axis=1) / n
    tht = 2.0 * np.pi * k1 * np.arange(LANES)[None, :] / n
    twr, twi = np.cos(tht), -np.sin(tht)
    th2 = 2.0 * np.pi * np.arange(LANES)[:, None] * np.arange(LANES)[None, :] / LANES
    cr, ci = np.cos(th2), -np.sin(th2)
    wbig = np.block([[cr, ci], [-ci, cr]])
    wbigc = np.block([[cr, -ci], [ci, cr]])
    return tuple(jnp.asarray(a, F32) for a in (fst, gst, twr, twi, wbig, wbigc))


def _hy_positions(t_len):
    pos = np.arange(t_len, dtype=np.float32)[:, None]
    t = pos / np.float32(max(t_len - 1, 1))
    bands = np.linspace(1e-4, HY_BANDS - 1, HY_BANDS, dtype=np.float32)[None, :]
    ang = bands * np.float32(2.0 * math.pi / t_len) * pos
    z = np.concatenate([t, np.cos(ang), -np.sin(ang)], -1).astype(np.float32)
    return jnp.asarray(np.pad(z, ((0, 0), (0, HY_FILT_HID - HY_EMB))))


def _hy_deltas():
    max_decay = math.log(HY_DECAY_TARGET) / HY_FAST_DECAY_PCT
    min_decay = math.log(HY_DECAY_TARGET) / HY_SLOW_DECAY_PCT
    d = np.abs(np.linspace(min_decay, max_decay, HY_DIM, dtype=np.float32))
    return jnp.asarray(np.tile(d, 4)[:, None])


def _lane_rep(a, rows=SUBLANES):
    ch, k = a.shape
    a = jnp.pad(a, ((0, 0), (0, rows - k)))
    return jnp.broadcast_to(a[:, :, None], (ch, rows, LANES))


def _out_even_kernel(x_ref, yat_ref, of_ref, ob_ref, zg_ref, nw_ref, wa_ref, wb_ref, g_ref, b_ref,
                     o_ref):
    mix = _mm(yat_ref[...], wa_ref[...], _TN)
    nw = nw_ref[...]
    for h in range(GDN_HEADS):
        sl = slice(h * GDN_DK, (h + 1) * GDN_DK)
        o = of_ref[:, sl] + ob_ref[:, sl]
        y = o * lax.rsqrt(jnp.mean(o * o, -1, keepdims=True) + RMS_EPS) * nw * _silu(zg_ref[:, sl])
        mix = mix + _mm(y, wb_ref[sl, :])
    o_ref[...] = _layer_norm(DEEPNORM_ALPHA * x_ref[...] + mix, g_ref[...], b_ref[...])


def _out_even(x2, yat, of, ob, projm, norm_w, wa, wb, g, b, tm=256):
    n, d = x2.shape
    row = lambda w: pl.BlockSpec((tm, w), lambda i: (i, 0))
    full = lambda a: pl.BlockSpec(a.shape, lambda i: (0,) * a.ndim)
    return pl.pallas_call(
        _out_even_kernel, grid=(n // tm,),
        in_specs=[row(d), pl.BlockSpec((HY_DIM, tm), lambda i: (0, i)), row(512), row(512),
                  pl.BlockSpec((tm, 512), lambda i: (i, 3)),
                  full(norm_w), full(wa), full(wb), full(g), full(b)],
        out_specs=row(d), out_shape=jax.ShapeDtypeStruct((n, d), F32),
        compiler_params=_params("parallel"), name="out_even",
    )(x2, yat, of, ob, projm, norm_w, wa, wb, g, b)


def _even_layer(x2, bsz, t_len, w_in, hy_conv_w, hy_conv_b, f_w1, f_b1, f_w2, f_b2, f_w3, f_b3,
                f_w4, f_freq, hy_skip, gdn_conv_w, gdn_a_log, gdn_dt_bias, gdn_norm_w, w_out,
                ln_g, ln_b):
    d = x2.shape[1]
    hyw = 3 * HY_DIM
    wm = jnp.concatenate([w_in[:, hyw:], jnp.zeros((d, LANES - 12), F32)], axis=1).astype(BF16)
    wt = w_in[:, :hyw].T.astype(BF16)
    projm, projt = _project(x2, wm, wt)
    consts = _fft_consts(t_len)
    w1p = jnp.pad(f_w1, ((0, HY_FILT_HID - HY_EMB), (0, 0)))
    filt = _hy_filters(_hy_positions(t_len), w1p, f_b1[None], f_w2, f_b2[None], f_w3, f_b3[None],
                       f_freq[None], f_w4.T, _hy_deltas(), t_len)
    r = t_len // LANES
    hr, hi = _hy_spectra(filt.reshape(4 * HY_DIM, r, LANES), consts)
    pp = _lane_rep(jnp.concatenate([hy_conv_w.T, hy_conv_b[:, None]], axis=1))
    ps = _lane_rep(hy_skip.T)
    yat = _hy_conv(projt.reshape(hyw, bsz, r, LANES), pp, ps, hr, hi, consts)
    yat = yat.reshape(HY_DIM, bsz * t_len)
    alog = jnp.zeros((1, LANES), F32).at[0, :8].set(gdn_a_log.reshape(-1))
    dtb = jnp.zeros((1, LANES), F32).at[0, :8].set(gdn_dt_bias.reshape(-1))
    qn, kn, vn, gates = _gdn_pre(projm, gdn_conv_w, alog, dtb, bsz, t_len)
    of, ob = _gdn_scan(qn, kn, vn, gates, bsz, t_len)
    return _out_even(x2, yat, of, ob, projm, gdn_norm_w[None], w_out[:HY_DIM].astype(BF16),
                     w_out[HY_DIM:].astype(BF16), ln_g[None], ln_b[None])


def _hg_scan_kernel(qf_ref, ff_ref, vf_ref, qb_ref, fb_ref, vb_ref, low_ref, of_ref, ob_ref,
                    state, q_s, k_s, g_s, *, layer):
    n = pl.program_id(1)

    @pl.when(n == 0)
    def _():
        state[...] = jnp.zeros_like(state)

    tb = qf_ref.shape[0]
    c = HG_CHUNK
    nsub = tb // c
    low = low_ref[...]
    e = jnp.exp(low - jnp.max(low, axis=0, keepdims=True))
    p = e / jnp.sum(e, axis=0, keepdims=True)
    lb = jnp.sum(p[1:layer + 1], axis=0, keepdims=True)
    rb = lax.broadcasted_iota(jnp.int32, (tb, tb), 0)
    cb = lax.broadcasted_iota(jnp.int32, (tb, tb), 1)
    same = (rb // c) == (cb // c)
    rowc = lax.broadcasted_iota(jnp.int32, (c, HG_DK), 0)
    dirs = ((qf_ref, ff_ref, vf_ref, of_ref), (qb_ref, fb_ref, vb_ref, ob_ref))
    for d, (q_ref, f_ref, v_ref, o_ref) in enumerate(dirs):
        fg = lb + (1.0 - lb) * _sigmoid(f_ref[...])
        tri = (same & ((cb <= rb) if d == 0 else (cb >= rb))).astype(F32)
        g_s[...] = _mm(tri, jnp.log(fg), exact=True)
        k_s[...] = 1.0 - fg
        q_s[...] = _silu(q_ref[...])

        def body(s, carry):
            sub = s if d == 0 else nsub - 1 - s
            rows = pl.ds(pl.multiple_of(sub * c, c), c)
            for h in range(HG_HEADS):
                sl = slice(h * HG_DK, (h + 1) * HG_DK)
                qc, kc, gc, vc = q_s[rows, sl], k_s[rows, sl], g_s[rows, sl], v_ref[rows, sl]
                g_last = gc[c - 1:c, :] if d == 0 else gc[0:1, :]
                st = state[d, h]
                out = _mm(qc * jnp.exp(gc), st, _NT)
                for j in range(c):
                    mask = (rowc >= j) if d == 0 else (rowc <= j)
                    dec = jnp.where(mask, jnp.exp(gc - gc[j:j + 1, :]), 0.0)
                    a_col = jnp.sum(qc * kc[j:j + 1, :] * dec, axis=1, keepdims=True)
                    out = out + a_col * vc[j:j + 1, :]
                o_ref[rows, sl] = out
                k_dec = kc * jnp.exp(g_last - gc)
                state[d, h] = st * jnp.exp(g_last) + _mm(vc, k_dec, _TN)
            return carry

        lax.fori_loop(0, nsub, body, 0)


def _hg_scan(proj, hg_lower, layer, bsz, t_len, tb=128):
    n = proj.shape[0]
    nb = t_len // tb

    def spec(colb, rev):
        if rev:
            return pl.BlockSpec((tb, 512), lambda b, i: (b * nb + nb - 1 - i, colb))
        return pl.BlockSpec((tb, 512), lambda b, i: (b * nb + i, colb))

    return pl.pallas_call(
        functools.partial(_hg_scan_kernel, layer=layer), grid=(bsz, nb),
        in_specs=[spec(0, False), spec(1, False), spec(3, False),
                  spec(0, True), spec(2, True), spec(3, True),
                  pl.BlockSpec(hg_lower.shape, lambda b, i: (0, 0))],
        out_specs=[spec(0, False), spec(0, True)],
        out_shape=[jax.ShapeDtypeStruct((n, 512), F32)] * 2,
        scratch_shapes=[pltpu.VMEM((2, HG_HEADS, HG_DK, HG_DK), F32)]
        + [pltpu.VMEM((tb, 512), F32)] * 3,
        compiler_params=_params("parallel", "arbitrary"), name="hg_scan",
    )(proj, proj, proj, proj, proj, proj, hg_lower)


def _rw_pre_kernel(rc, rp, rn, kc, kp, kn, vc, vp, vn, lc, lp, ln, mu_ref, w0_ref, w2_ref,
                   a0_ref, a2_ref, g2_ref, kk_ref, ka_ref, bd_ref,
                   r_o, k_o, v_o, kk_o, kka_o, lwf_o, lwb_o, g_o, scr):
    i = pl.program_id(1)
    first = i == 0
    last = i == pl.num_programs(1) - 1
    tb = rc.shape[0]

    def shifted(cur_ref, prev_ref, next_ref, col0):
        w = cur_ref.shape[1]
        cur = cur_ref[...]
        scr[pl.ds(SUBLANES, tb), pl.ds(0, w)] = cur
        scr[pl.ds(0, SUBLANES), pl.ds(0, w)] = jnp.where(first, 0.0, prev_ref[...])
        scr[pl.ds(SUBLANES + tb, SUBLANES), pl.ds(0, w)] = jnp.where(last, 0.0, next_ref[...])
        prev = scr[pl.ds(SUBLANES - 1, tb), pl.ds(0, w)]
        nxt = scr[pl.ds(SUBLANES + 1, tb), pl.ds(0, w)]
        mu = mu_ref[:, col0:col0 + w]
        return cur + mu[0:1] * (prev - cur) + mu[1:2] * (nxt - cur)

    r = shifted(rc, rp, rn, 0)
    k = shifted(kc, kp, kn, RW_DIM)
    v = shifted(vc, vp, vn, 2 * RW_DIM)
    lo = shifted(lc, lp, ln, 3 * RW_DIM)
    lo_wa, lo_g = lo[:, :LANES], lo[:, LANES:]
    th = jnp.tanh(lo_wa)
    for d, o_ref in enumerate((lwf_o, lwb_o)):
        w_raw = w0_ref[d:d + 1, :] + _mm(th, w2_ref[d])
        o_ref[...] = -jnp.exp(-_softplus(-w_raw) - 0.5)
    a = _sigmoid(a0_ref[...] + _mm(lo_wa, a2_ref[...]))
    g_o[...] = _mm(_sigmoid(lo_g), g2_ref[...])
    kx = k * kk_ref[...]
    kk = kx * lax.rsqrt(_mm(kx * kx, bd_ref[...], exact=True) + L2_EPS)
    r_o[...] = r
    v_o[...] = v
    k_o[...] = k * (1.0 + (a - 1.0) * ka_ref[...])
    kk_o[...] = kk
    kka_o[...] = kk * a


def _rw_pre(proj, mu, w0, w2p, a0, a2p, g2, k_k, k_a, bd, bsz, t_len, tb=256):
    n = proj.shape[0]
    nb = t_len // tb
    base = 2560
    specs = []
    for off, w in ((0, 512), (512, 512), (1024, 512), (1536, 256)):
        specs += _halo_specs(w, (base + off) // w, tb, t_len, bsz)
    full = lambda a: pl.BlockSpec(a.shape, lambda b, i: (0,) * a.ndim)
    out_spec = pl.BlockSpec((tb, RW_DIM), lambda b, i: (b * nb + i, 0))
    consts = (mu, w0, w2p, a0, a2p, g2, k_k, k_a, bd)
    return pl.pallas_call(
        _rw_pre_kernel, grid=(bsz, nb),
        in_specs=specs + [full(a) for a in consts],
        out_specs=[out_spec] * 8,
        out_shape=[jax.ShapeDtypeStruct((n, RW_DIM), F32)] * 8,
        scratch_shapes=[pltpu.VMEM((tb + 2 * SUBLANES, RW_DIM), F32)],
        compiler_params=_params("parallel", "parallel"), name="rw_pre",
    )(*([proj] * 12), *consts)


def _rw_scan_kernel(rf, kf, vf, bf, af, wf, rb, kb, vb, bb, ab, wb, yf, yb, state):
    n = pl.program_id(1)

    @pl.when(n == 0)
    def _():
        state[...] = jnp.zeros_like(state)

    c = RW_CHUNK
    row = lax.broadcasted_iota(jnp.int32, (c, c), 0)
    col = lax.broadcasted_iota(jnp.int32, (c, c), 1)
    eye = (row == col).astype(F32)
    lane = lax.broadcasted_iota(jnp.int32, (c, LANES), 1)
    head0 = lane < RW_HD
    hmask = (head0.astype(F32), 1.0 - head0.astype(F32))
    r2 = lax.broadcasted_iota(jnp.int32, (LANES, LANES), 0)
    c2 = lax.broadcasted_iota(jnp.int32, (LANES, LANES), 1)
    blockdiag = ((r2 < RW_HD) == (c2 < RW_HD)).astype(F32)
    dirs = ((rf, kf, vf, bf, af, wf, yf), (rb, kb, vb, bb, ab, wb, yb))
    for d, (r_ref, k_ref, v_ref, b_ref, a_ref, w_ref, y_ref) in enumerate(dirs):
        incl = (col <= row) if d == 0 else (col >= row)
        strict = (col < row) if d == 0 else (col > row)
        lw = w_ref[...]
        g_in = _mm(incl.astype(F32), lw, exact=True)
        g_ex = g_in - lw
        g_end = g_in[c - 1:c, :] if d == 0 else g_in[0:1, :]
        e_neg = jnp.exp(-g_in)
        e_end = jnp.exp(g_end - g_in)
        k_all, a_all = k_ref[...], a_ref[...]
        rt_all = r_ref[...] * jnp.exp(g_in)
        bt_all = b_ref[...] * jnp.exp(g_ex)
        kh_all, ah_all = k_all * e_neg, a_all * e_neg
        kbar_all, abar_all = k_all * e_end, a_all * e_end
        dec_end = jnp.exp(g_end)
        for p in range(RW_DIM // LANES):
            sl = slice(p * LANES, (p + 1) * LANES)
            rt, bt, v = rt_all[:, sl], bt_all[:, sl], v_ref[:, sl]
            xs = jnp.concatenate([rt * hmask[0], rt * hmask[1], bt * hmask[0], bt * hmask[1]], axis=0)
            ak = _mm(xs, kh_all[:, sl], _NT)
            aa = _mm(xs, ah_all[:, sl], _NT)
            m = state[d, p]
            rhs_u = _mm(bt, m, _NT)
            y = _mm(rt, m, _NT)
            bkv, t_inv = [], []
            for j in range(2):
                a_bk = jnp.where(strict, ak[(2 + j) * c:(3 + j) * c], 0.0)
                a_ba = jnp.where(strict, aa[(2 + j) * c:(3 + j) * c], 0.0)
                bkv.append(_mm(a_bk, v))
                t_inv.append(_unit_tri_inv(a_ba, eye))
            rhs_u = rhs_u + jnp.where(head0, bkv[0], bkv[1])
            u = jnp.where(head0, _mm(t_inv[0], rhs_u, exact=True), _mm(t_inv[1], rhs_u, exact=True))
            vu = jnp.concatenate([v, u], axis=0)
            outs = []
            for j in range(2):
                a_rk = jnp.where(incl, ak[j * c:(j + 1) * c], 0.0)
                a_ra = jnp.where(incl, aa[j * c:(j + 1) * c], 0.0)
                outs.append(_mm(jnp.concatenate([a_rk, -a_ra], axis=1), vu))
            y_ref[:, sl] = y + jnp.where(head0, outs[0], outs[1])
            upd = _mm(vu, jnp.concatenate([kbar_all[:, sl], -abar_all[:, sl]], axis=0), _TN)
            state[d, p] = m * dec_end[:, sl] + upd * blockdiag


def _rw_scan(r, k, v, kk, kka, lwf, lwb, bsz, t_len):
    n = r.shape[0]
    c = RW_CHUNK
    nc = t_len // c
    fwd = pl.BlockSpec((c, RW_DIM), lambda b, i: (b * nc + i, 0))
    bwd = pl.BlockSpec((c, RW_DIM), lambda b, i: (b * nc + nc - 1 - i, 0))
    return pl.pallas_call(
        _rw_scan_kernel, grid=(bsz, nc),
        in_specs=[fwd] * 6 + [bwd] * 6, out_specs=[fwd, bwd],
        out_shape=[jax.ShapeDtypeStruct((n, RW_DIM), F32)] * 2,
        scratch_shapes=[pltpu.VMEM((2, RW_DIM // LANES, LANES, LANES), F32)],
        compiler_params=_params("parallel", "arbitrary"), name="rw_scan",
    )(r, k, v, kk, kka, lwf, r, k, v, kk, kka, lwb)


def _out_odd_kernel(x_ref, hf_ref, hb_ref, hg_ref, hnw_ref, yf_ref, yb_ref, r_ref, k_ref, v_ref,
                    gate_ref, rk_ref, lw_ref, lb_ref, bd_ref, wc_ref, wd_ref, g_ref, b_ref, o_ref):
    nw = hnw_ref[...]
    mix = None
    for h in range(HG_HEADS):
        sl = slice(h * HG_DK, (h + 1) * HG_DK)
        o = hf_ref[:, sl] + hb_ref[:, sl]
        y = o * lax.rsqrt(jnp.mean(o * o, -1, keepdims=True) + RMS_EPS) * nw * _silu(hg_ref[:, sl])
        part = _mm(y, wc_ref[sl, :])
        mix = part if mix is None else mix + part
    bd = bd_ref[...]
    y = yf_ref[...] + yb_ref[...]
    yc = y - _mm(y, bd, exact=True) * (1.0 / RW_HD)
    var = _mm(yc * yc, bd, exact=True) * (1.0 / RW_HD)
    yn = yc * lax.rsqrt(var + RW_GN_EPS) * lw_ref[...] + lb_ref[...]
    bonus = _mm(r_ref[...] * k_ref[...] * rk_ref[...], bd, exact=True) * v_ref[...]
    mix = mix + _mm((yn + bonus) * gate_ref[...], wd_ref[...])
    o_ref[...] = _layer_norm(DEEPNORM_ALPHA * x_ref[...] + mix, g_ref[...], b_ref[...])


def _out_odd(x2, hf, hb, proj, hnw, yf, yb, r, k, v, gate, r_k, lnx_w, lnx_b, bd, wc, wd, g, b,
             tm=256):
    n, d = x2.shape
    row = lambda w: pl.BlockSpec((tm, w), lambda i: (i, 0))
    full = lambda a: pl.BlockSpec(a.shape, lambda i: (0,) * a.ndim)
    consts = (r_k, lnx_w, lnx_b, bd, wc, wd, g, b)
    return pl.pallas_call(
        _out_odd_kernel, grid=(n // tm,),
        in_specs=[row(d), row(512), row(512), pl.BlockSpec((tm, 512), lambda i: (i, 4)), full(hnw)]
        + [row(512)] * 6 + [full(a) for a in consts],
        out_specs=row(d), out_shape=jax.ShapeDtypeStruct((n, d), F32),
        compiler_params=_params("parallel"), name="out_odd",
    )(x2, hf, hb, proj, hnw, yf, yb, r, k, v, gate, *consts)


def _odd_layer(x2, bsz, t_len, layer, w_in, hg_lower, hg_norm_w, rw_mu, rw_w0, rw_w2, rw_a0, rw_a2,
               rw_g2, rw_k_k, rw_k_a, rw_r_k, rw_lnx_w, rw_lnx_b, w_out, ln_g, ln_b):
    proj = _project(x2, w_in.astype(BF16))
    hf, hb = _hg_scan(proj, hg_lower, layer, bsz, t_len)
    idx = np.arange(RW_DIM) // RW_HD
    bd = jnp.asarray(idx[:, None] == idx[None, :], F32)
    lora = rw_w2.shape[1]
    w2p = jnp.pad(rw_w2, ((0, 0), (0, LANES - lora), (0, 0)))
    a2p = jnp.pad(rw_a2, ((LANES - rw_a2.shape[0], 0), (0, 0)))
    r, k, v, kk, kka, lwf, lwb, gate = _rw_pre(proj, rw_mu, rw_w0, w2p, rw_a0[None], a2p, rw_g2,
                                               rw_k_k[None], rw_k_a[None], bd, bsz, t_len)
    yf, yb = _rw_scan(r, k, v, kk, kka, lwf, lwb, bsz, t_len)
    return _out_odd(x2, hf, hb, proj, hg_norm_w[None], yf, yb, r, k, v, gate,
                    rw_r_k.reshape(1, RW_DIM), rw_lnx_w[None], rw_lnx_b[None], bd,
                    w_out[:512].astype(BF16), w_out[512:].astype(BF16), ln_g[None], ln_b[None])


def kernel(x, ev_w_in, hy_conv_w, hy_conv_b, hy_filt_w1, hy_filt_b1, hy_filt_w2, hy_filt_b2, hy_filt_w3, hy_filt_b3, hy_filt_w4, hy_filt_freq, hy_skip, gdn_conv_w, gdn_a_log, gdn_dt_bias, gdn_norm_w, ev_w_out, od_w_in, hg_lower, hg_norm_w, rw_mu, rw_w0, rw_w2, rw_a0, rw_a2, rw_g2, rw_k_k, rw_k_a, rw_r_k, rw_lnx_w, rw_lnx_b, od_w_out, ln_g, ln_b, mlp_w1, mlp_w2):
    bsz, t_len, d = x.shape
    x2 = x.reshape(bsz * t_len, d)
    for layer in range(DEPTH):
        i = layer // 2
        if layer % 2 == 0:
            x2 = _even_layer(x2, bsz, t_len, ev_w_in[i], hy_conv_w[i], hy_conv_b[i], hy_filt_w1[i],
                             hy_filt_b1[i], hy_filt_w2[i], hy_filt_b2[i], hy_filt_w3[i],
                             hy_filt_b3[i], hy_filt_w4[i], hy_filt_freq[i], hy_skip[i],
                             gdn_conv_w[i], gdn_a_log[i], gdn_dt_bias[i], gdn_norm_w[i],
                             ev_w_out[i], ln_g[layer, 0], ln_b[layer, 0])
        else:
            x2 = _odd_layer(x2, bsz, t_len, layer, od_w_in[i], hg_lower, hg_norm_w[i], rw_mu[i],
                            rw_w0[i], rw_w2[i], rw_a0[i], rw_a2[i], rw_g2[i], rw_k_k[i], rw_k_a[i],
                            rw_r_k[i], rw_lnx_w[i], rw_lnx_b[i], od_w_out[i], ln_g[layer, 0],
                            ln_b[layer, 0])
        x2 = _mlp_block(x2, mlp_w1[layer].astype(BF16), mlp_w2[layer].astype(BF16),
                        ln_g[layer, 1][None], ln_b[layer, 1][None])
    return x2.reshape(bsz, t_len, d)
```

```python
import functools
import math

import numpy as np
import jax
import jax.numpy as jnp
from jax import lax
from jax.experimental import pallas as pl
from jax.experimental.pallas import tpu as pltpu

F32 = jnp.float32
BF16 = jnp.bfloat16

D_MODEL = 1024
DEPTH = 2
D_FF = 4 * D_MODEL
DEEPNORM_ALPHA = (2.0 * DEPTH) ** 0.25
LN_EPS = 1e-5
RMS_EPS = 1e-6
L2_EPS = 1e-6

HY_DIM = 512
HY_EMB = 33
HY_BANDS = 16
HY_FILT_HID = 64
HY_DECAY_TARGET = 1e-2
HY_FAST_DECAY_PCT = 0.3
HY_SLOW_DECAY_PCT = 1.5

GDN_HEADS = 4
GDN_DK = 128
GDN_CHUNK = 64

HG_HEADS = 4
HG_DK = 128
HG_CHUNK = 16

RW_HEADS = 8
RW_HD = 64
RW_DIM = 512
RW_CHUNK = 64
RW_GROUP = 4
RW_GN_EPS = 64e-5

SCAN_BATCH = 2

LANES = 128
SUBLANES = 8
VMEM_LIMIT = 56 * 1024 * 1024

_NN = (((1,), (0,)), ((), ()))
_NT = (((1,), (1,)), ((), ()))
_TN = (((0,), (0,)), ((), ()))


def _mm(a, b, dims=_NN, exact=False):
    if exact:
        return lax.dot_general(a, b, dims, precision=lax.Precision.HIGHEST,
                               preferred_element_type=F32)
    return lax.dot_general(a.astype(BF16), b.astype(BF16), dims, preferred_element_type=F32)


def _params(*sem):
    return pltpu.CompilerParams(dimension_semantics=sem, vmem_limit_bytes=VMEM_LIMIT)


def _sigmoid(x):
    return 1.0 / (1.0 + jnp.exp(-x))


def _silu(x):
    return x * _sigmoid(x)


def _softplus(x):
    return jnp.maximum(x, 0.0) + jnp.log(1.0 + jnp.exp(-jnp.abs(x)))


def _layer_norm(v, g, b):
    mean = jnp.mean(v, axis=-1, keepdims=True)
    vc = v - mean
    var = jnp.mean(vc * vc, axis=-1, keepdims=True)
    return vc * lax.rsqrt(var + LN_EPS) * g + b


def _col(x, idx, lane):
    return jnp.sum(jnp.where(lane == idx, x, 0.0), axis=1, keepdims=True)


def _split2(a):
    hi = a.astype(BF16)
    return hi, (a - hi.astype(F32)).astype(BF16)


def _mm3(a, b, dims=_NN):
    ah, al = _split2(a)
    bh, bl = _split2(b)
    dot = functools.partial(lax.dot_general, dimension_numbers=dims, preferred_element_type=F32)
    return dot(ah, bh) + dot(ah, bl) + dot(al, bh)


def _sel_mm(sel, x, dims=_NN, sel_first=True):
    hi = x.astype(BF16)
    r1 = x - hi.astype(F32)
    mid = r1.astype(BF16)
    lo = (r1 - mid.astype(F32)).astype(BF16)
    dot = functools.partial(lax.dot_general, dimension_numbers=dims, preferred_element_type=F32)
    if sel_first:
        return dot(sel, hi) + dot(sel, mid) + dot(sel, lo)
    return dot(hi, sel) + dot(mid, sel) + dot(lo, sel)


def _unit_tri_inv(a, eye, depth):
    p = -a
    t = eye + p
    k = 2
    while k < depth:
        mm = _mm3 if k == 2 else _mm
        p = mm(p, p)
        t = t + mm(t, p)
        k *= 2
    return t


def _proj_kernel(x_ref, w_ref, o_ref):
    o_ref[...] = _mm(x_ref[...], w_ref[...])


def _proj_t_kernel(x_ref, w_ref, wt_ref, o_ref, ot_ref):
    xb = x_ref[...].astype(BF16)
    o_ref[...] = _mm(xb, w_ref[...])
    ot_ref[...] = _mm(wt_ref[...], xb, _NT)


def _project(x2, w, wt=None, tm=256):
    n, d = x2.shape
    m = w.shape[1]
    grid = (n // tm,)
    x_spec = pl.BlockSpec((tm, d), lambda i: (i, 0))
    w_spec = pl.BlockSpec((d, m), lambda i: (0, 0))
    o_spec = pl.BlockSpec((tm, m), lambda i: (i, 0))
    if wt is None:
        return pl.pallas_call(
            _proj_kernel, grid=grid, in_specs=[x_spec, w_spec], out_specs=o_spec,
            out_shape=jax.ShapeDtypeStruct((n, m), F32), compiler_params=_params("parallel"),
            name="proj")(x2, w)
    mt = wt.shape[0]
    return pl.pallas_call(
        _proj_t_kernel, grid=grid,
        in_specs=[x_spec, w_spec, pl.BlockSpec((mt, d), lambda i: (0, 0))],
        out_specs=[o_spec, pl.BlockSpec((mt, tm), lambda i: (0, i))],
        out_shape=[jax.ShapeDtypeStruct((n, m), F32), jax.ShapeDtypeStruct((mt, n), F32)],
        compiler_params=_params("parallel"), name="proj_t")(x2, w, wt)


def _mlp_kernel(x_ref, w1_ref, w2_ref, g_ref, b_ref, o_ref, acc_ref):
    j = pl.program_id(1)
    x = x_ref[...]
    h = jnp.maximum(_mm(x, w1_ref[...]), 0.0)
    part = _mm(h * h, w2_ref[...])

    @pl.when(j == 0)
    def _():
        acc_ref[...] = part

    @pl.when(j > 0)
    def _():
        acc_ref[...] += part

    @pl.when(j == pl.num_programs(1) - 1)
    def _():
        o_ref[...] = _layer_norm(DEEPNORM_ALPHA * x + acc_ref[...], g_ref[...], b_ref[...])


def _mlp_block(x2, w1, w2, g, b, tm=512, tf=512):
    n, d = x2.shape
    f = w1.shape[1]
    return pl.pallas_call(
        _mlp_kernel, grid=(n // tm, f // tf),
        in_specs=[pl.BlockSpec((tm, d), lambda i, j: (i, 0)),
                  pl.BlockSpec((d, tf), lambda i, j: (0, j)),
                  pl.BlockSpec((tf, d), lambda i, j: (j, 0)),
                  pl.BlockSpec((1, d), lambda i, j: (0, 0)),
                  pl.BlockSpec((1, d), lambda i, j: (0, 0))],
        out_specs=pl.BlockSpec((tm, d), lambda i, j: (i, 0)),
        out_shape=jax.ShapeDtypeStruct((n, d), F32),
        scratch_shapes=[pltpu.VMEM((tm, d), F32)],
        compiler_params=_params("parallel", "arbitrary"), name="mlp")(x2, w1, w2, g, b)


def _fill_halo(scr, cur, prev8, next8, first, last, halo):
    tb = cur.shape[0]
    scr[pl.ds(SUBLANES, tb), :] = cur
    scr[pl.ds(0, SUBLANES), :] = jnp.where(first, 0.0, prev8)
    scr[pl.ds(SUBLANES + tb, SUBLANES), :] = jnp.where(last, 0.0, next8)


def _halo_specs(width, col_block, tb, t_len, bsz):
    nb = t_len // tb
    r8 = tb // SUBLANES
    tot8 = t_len // SUBLANES

    def cur(b, i):
        return (b * nb + i, col_block)

    def prev(b, i):
        return (jnp.maximum(b * tot8 + i * r8 - 1, 0), col_block)

    def nxt(b, i):
        return (jnp.minimum(b * tot8 + (i + 1) * r8, tot8 * bsz - 1), col_block)

    return [pl.BlockSpec((tb, width), cur), pl.BlockSpec((SUBLANES, width), prev),
            pl.BlockSpec((SUBLANES, width), nxt)]


def _gdn_pre_kernel(cur_ref, prev_ref, next_ref, ab_ref, cw_ref, alog_ref, dtb_ref,
                    q_ref, k_ref, v_ref, g_ref, scr):
    i = pl.program_id(1)
    tb = cur_ref.shape[0]
    _fill_halo(scr, cur_ref[...], prev_ref[...], next_ref[...], i == 0,
               i == pl.num_programs(1) - 1, 2)
    width = cw_ref.shape[0]
    acc = None
    for j in range(width):
        term = scr[pl.ds(SUBLANES - width // 2 + j, tb), :] * cw_ref[pl.ds(j, 1), :]
        acc = term if acc is None else acc + term
    qkv = _silu(acc)
    hd = GDN_HEADS * GDN_DK
    for h in range(GDN_HEADS):
        sl = slice(h * GDN_DK, (h + 1) * GDN_DK)
        q = qkv[:, sl]
        q_ref[:, sl] = q * lax.rsqrt(jnp.sum(q * q, -1, keepdims=True) + L2_EPS) * GDN_DK ** -0.5
        k = qkv[:, hd + h * GDN_DK: hd + (h + 1) * GDN_DK]
        k_ref[:, sl] = k * lax.rsqrt(jnp.sum(k * k, -1, keepdims=True) + L2_EPS)
    v_ref[...] = qkv[:, 2 * hd:]
    ab = ab_ref[...]
    lane = lax.broadcasted_iota(jnp.int32, ab.shape, 1)
    log_g = -jnp.exp(alog_ref[...]) * _softplus(ab + dtb_ref[...])
    g_ref[...] = jnp.where(lane < 2 * GDN_HEADS, log_g, _sigmoid(ab))


def _gdn_pre(proj, conv_w, a_log_row, dt_bias_row, bsz, t_len, tb=256):
    n = proj.shape[0]
    wq = 3 * GDN_HEADS * GDN_DK
    nb = t_len // tb
    specs = _halo_specs(wq, 0, tb, t_len, bsz)
    ab_col = (wq + GDN_HEADS * GDN_DK) // LANES
    out_spec = pl.BlockSpec((tb, 512), lambda b, i: (b * nb + i, 0))
    return pl.pallas_call(
        _gdn_pre_kernel, grid=(bsz, nb),
        in_specs=specs + [pl.BlockSpec((tb, LANES), lambda b, i: (b * nb + i, ab_col)),
                          pl.BlockSpec(conv_w.shape, lambda b, i: (0, 0)),
                          pl.BlockSpec((1, LANES), lambda b, i: (0, 0)),
                          pl.BlockSpec((1, LANES), lambda b, i: (0, 0))],
        out_specs=[out_spec, out_spec, out_spec,
                   pl.BlockSpec((tb, LANES), lambda b, i: (b * nb + i, 0))],
        out_shape=[jax.ShapeDtypeStruct((n, 512), F32)] * 3 + [jax.ShapeDtypeStruct((n, LANES), F32)],
        scratch_shapes=[pltpu.VMEM((tb + 2 * SUBLANES, wq), F32)],
        compiler_params=_params("parallel", "parallel"), name="gdn_pre",
    )(proj, proj, proj, proj, conv_w, a_log_row, dt_bias_row)


def _gdn_scan_kernel(qf, kf, vf, gf, qb, kb, vb, gb, of, ob, state):
    n = pl.program_id(1)

    @pl.when(n == 0)
    def _():
        state[...] = jnp.zeros_like(state)

    c = GDN_CHUNK
    nh = GDN_HEADS
    hc = nh * c
    wide = nh * GDN_DK
    row = lax.broadcasted_iota(jnp.int32, (c, c), 0)
    col = lax.broadcasted_iota(jnp.int32, (c, c), 1)
    lane = lax.broadcasted_iota(jnp.int32, (c, LANES), 1)
    rr = lax.broadcasted_iota(jnp.int32, (hc, hc), 0)
    cc = lax.broadcasted_iota(jnp.int32, (hc, hc), 1)
    same = (rr // c) == (cc // c)
    eye = (rr == cc).astype(F32)
    head_lanes = (lax.broadcasted_iota(jnp.int32, (hc, wide), 0) // c
                  == lax.broadcasted_iota(jnp.int32, (hc, wide), 1) // GDN_DK)
    lane_blk = lax.broadcasted_iota(jnp.int32, (1, hc), 1) // c

    def stack_wide(x):
        return jnp.where(head_lanes, jnp.concatenate([x] * nh, axis=0), 0.0)

    def stack_narrow(x):
        return jnp.concatenate([x[:, h * GDN_DK:(h + 1) * GDN_DK] for h in range(nh)], axis=0)

    dirs = ((qf, kf, vf, gf, of), (qb, kb, vb, gb, ob))
    for bb, d in [(bb, d) for bb in range(qf.shape[0]) for d in range(2)]:
        q_ref, k_ref, v_ref, g_ref, o_ref = dirs[d]
        incl1 = (col <= row) if d == 0 else (col >= row)
        incl = same & ((cc <= rr) if d == 0 else (cc >= rr))
        strict = same & ((cc < rr) if d == 0 else (cc > rr))
        gates = g_ref[bb]
        tri = incl1.astype(BF16)
        gam_all = _sel_mm(tri, gates)
        tri4 = jnp.concatenate([tri] * nh, axis=0)
        gam_rows = _sel_mm(tri4, gates.T, _NT, sel_first=False)
        gam = jnp.concatenate([_col(gam_all, d * nh + h, lane) for h in range(nh)], axis=0)
        beta = jnp.concatenate([_col(gates, 2 * nh + h, lane) for h in range(nh)], axis=0)
        gam_row = jnp.zeros((1, hc), F32)
        for h in range(nh):
            gam_row = jnp.where(lane_blk == h, gam_rows[d * nh + h:d * nh + h + 1, :], gam_row)
        last = c - 1 if d == 0 else 0
        g_last = jnp.concatenate(
            [jnp.broadcast_to(gam[h * c + last:h * c + last + 1, :], (c, 1)) for h in range(nh)], axis=0)
        g_last_s = jnp.concatenate(
            [jnp.broadcast_to(gam[h * c + last:h * c + last + 1, :], (GDN_DK, 1)) for h in range(nh)],
            axis=0)
        diff = gam - gam_row
        decay = jnp.where(incl, jnp.exp(jnp.where(incl, diff, 0.0)), 0.0)
        e_gam = jnp.exp(gam)
        k_in = k_ref[bb]
        k_w, q_w = stack_wide(k_in), stack_wide(q_ref[bb])
        a_mat = jnp.where(strict, _mm(k_w, k_w, _NT) * decay * beta, 0.0)
        t_inv = _unit_tri_inv(a_mat, eye, c)
        rhs = jnp.concatenate([stack_narrow(v_ref[bb]) * beta,
                               stack_narrow(k_in) * (beta * e_gam)], axis=1)
        sol = _mm3(t_inv, rhs)
        u, w = sol[:, :GDN_DK], sol[:, GDN_DK:]
        attn = _mm(q_w, k_w, _NT) * decay
        s = state[bb, d]
        w_w = jnp.where(head_lanes, jnp.concatenate([w] * nh, axis=1), 0.0)
        v_new = u - _mm(w_w, s)
        out = _mm(q_w * e_gam, s) + _mm(attn, v_new)
        o_ref[bb] = jnp.concatenate([out[h * c:(h + 1) * c] for h in range(nh)], axis=1)
        k_dec = k_w * jnp.exp(g_last - gam)
        state[bb, d] = s * jnp.exp(g_last_s) + _mm(k_dec, v_new, _TN)


def _gdn_scan(q, k, v, gates, bsz, t_len, bb=SCAN_BATCH):
    n = q.shape[0]
    c = GDN_CHUNK
    nc = t_len // c
    q, k, v, gates = (a.reshape(bsz, t_len, a.shape[1]) for a in (q, k, v, gates))

    def fwd(g, i):
        return (g, i, 0)

    def bwd(g, i):
        return (g, nc - 1 - i, 0)

    def specs(idx):
        wide = pl.BlockSpec((bb, c, 512), idx)
        return [wide, wide, wide, pl.BlockSpec((bb, c, LANES), idx)]

    of, ob = pl.pallas_call(
        _gdn_scan_kernel, grid=(bsz // bb, nc),
        in_specs=specs(fwd) + specs(bwd),
        out_specs=[pl.BlockSpec((bb, c, 512), fwd), pl.BlockSpec((bb, c, 512), bwd)],
        out_shape=[jax.ShapeDtypeStruct((bsz, t_len, 512), F32)] * 2,
        scratch_shapes=[pltpu.VMEM((bb, 2, GDN_HEADS * GDN_DK, GDN_DK), F32)],
        compiler_params=_params("parallel", "arbitrary"), name="gdn_scan",
    )(q, k, v, gates, q, k, v, gates)
    return of.reshape(n, 512), ob.reshape(n, 512)


_fft_mm = _mm3


def _hy_filter_kernel(z_ref, w1_ref, b1_ref, w2_ref, b2_ref, w3_ref, b3_ref, fr_ref, w4t_ref,
                      dl_ref, o_ref, *, t_len):
    i = pl.program_id(0)
    tl = o_ref.shape[1]
    fr = fr_ref[...]
    h = jnp.sin(fr * (_mm(z_ref[...], w1_ref[...], exact=True) + b1_ref[...]))
    h = jnp.sin(fr * (_mm(h, w2_ref[...], exact=True) + b2_ref[...]))
    h = jnp.sin(fr * (_mm(h, w3_ref[...], exact=True) + b3_ref[...]))
    ht = _mm(w4t_ref[...], h, _NT, exact=True)
    pos = (lax.broadcasted_iota(jnp.int32, (1, tl), 1) + i * tl).astype(F32)
    t_norm = pos / float(max(t_len - 1, 1))
    o_ref[...] = ht * jnp.exp(-t_norm * dl_ref[...])


def _hy_filters(z, w1p, b1, w2, b2, w3, b3, freq, w4t, deltas, t_len):
    tl = min(512, t_len)
    nch = w4t.shape[0]
    full = lambda a: pl.BlockSpec(a.shape, lambda i: (0,) * a.ndim)
    return pl.pallas_call(
        functools.partial(_hy_filter_kernel, t_len=t_len), grid=(t_len // tl,),
        in_specs=[pl.BlockSpec((tl, z.shape[1]), lambda i: (i, 0))]
        + [full(a) for a in (w1p, b1, w2, b2, w3, b3, freq, w4t, deltas)],
        out_specs=pl.BlockSpec((nch, tl), lambda i: (0, i)),
        out_shape=jax.ShapeDtypeStruct((nch, t_len), F32),
        compiler_params=_params("parallel"), name="hy_filter",
    )(z, w1p, b1, w2, b2, w3, b3, freq, w4t, deltas)


def _fft_fwd(x3, fst, twr, twi, wbig):
    cb = x3.shape[0]
    n1 = twr.shape[0]
    ys = [_fft_mm(fst, x3[c]) for c in range(cb)]
    yr = jnp.stack([y[:n1] for y in ys])
    yi = jnp.stack([y[n1:] for y in ys])
    tr = yr * twr - yi * twi
    ti = yr * twi + yi * twr
    cat = jnp.concatenate([tr, ti], axis=-1).reshape(cb * n1, 2 * LANES)
    z = _fft_mm(cat, wbig).reshape(cb, n1, 2 * LANES)
    return z[..., :LANES], z[..., LANES:]


def _fft_inv(zr, zi, gst, twr, twi, wbigc):
    cb, n1 = zr.shape[0], zr.shape[1]
    cat = jnp.concatenate([zr, zi], axis=-1).reshape(cb * n1, 2 * LANES)
    a = _fft_mm(cat, wbigc).reshape(cb, n1, 2 * LANES)
    ar, ai = a[..., :LANES], a[..., LANES:]
    tr = ar * twr + ai * twi
    ti = ai * twr - ar * twi
    outs = [_fft_mm(gst, jnp.concatenate([tr[c], ti[c]], axis=0))
            for c in range(cb)]
    return jnp.stack(outs)


def _hy_spec_kernel(hf_ref, hb_ref, fst_ref, twr_ref, twi_ref, wbig_ref, hr_ref, hi_ref):
    args = (fst_ref[...], twr_ref[...], twi_ref[...], wbig_ref[...])
    fr, fi = _fft_fwd(hf_ref[...], *args)
    br, bi = _fft_fwd(hb_ref[...], *args)
    hr_ref[...] = fr + br
    hi_ref[...] = fi - bi


def _hy_spectra(filt3, consts, cb=8):
    fst, gst, twr, twi, wbig, wbigc = consts
    r = filt3.shape[1]
    n1 = twr.shape[0]
    nj = HY_DIM // cb
    full = lambda a: pl.BlockSpec(a.shape, lambda o, j: (0,) * a.ndim)
    out_spec = pl.BlockSpec((cb, n1, LANES), lambda o, j: (o * nj + j, 0, 0))
    return pl.pallas_call(
        _hy_spec_kernel, grid=(2, nj),
        in_specs=[pl.BlockSpec((cb, r, LANES), lambda o, j: (2 * o * nj + j, 0, 0)),
                  pl.BlockSpec((cb, r, LANES), lambda o, j: ((2 * o + 1) * nj + j, 0, 0)),
                  full(fst), full(twr), full(twi), full(wbig)],
        out_specs=[out_spec, out_spec],
        out_shape=[jax.ShapeDtypeStruct((2 * HY_DIM, n1, LANES), F32)] * 2,
        compiler_params=_params("parallel", "parallel"), name="hy_spec",
    )(filt3, filt3, fst, twr, twi, wbig)


def _tshift(x3, back, lane, rowi):
    r = x3.shape[1]
    if back:
        rl = pltpu.roll(x3, 1, 2)
        rr = pltpu.roll(rl, 1, 1)
        edge = lane == 0
        out = jnp.where(edge, rr, rl)
        return jnp.where(edge & (rowi == 0), 0.0, out)
    rl = pltpu.roll(x3, LANES - 1, 2)
    rr = pltpu.roll(rl, r - 1, 1)
    edge = lane == LANES - 1
    out = jnp.where(edge, rr, rl)
    return jnp.where(edge & (rowi == r - 1), 0.0, out)


def _hy_conv_kernel(x1_ref, x2_ref, v_ref, p1_ref, p2_ref, pv_ref, sk_ref, h1r_ref, h1i_ref,
                    h2r_ref, h2i_ref, fst_ref, gst_ref, twr_ref, twi_ref, wbig_ref, wbigc_ref,
                    o_ref):
    shape = x1_ref.shape
    lane = lax.broadcasted_iota(jnp.int32, shape, 2)
    rowi = lax.broadcasted_iota(jnp.int32, shape, 1)

    def short_conv(x_ref, p_ref):
        x = x_ref[...]
        p = p_ref[...]
        return (p[:, 0:1, :] * _tshift(x, True, lane, rowi) + p[:, 1:2, :] * x
                + p[:, 2:3, :] * _tshift(x, False, lane, rowi) + p[:, 3:4, :])

    x1 = short_conv(x1_ref, p1_ref)
    x2 = short_conv(x2_ref, p2_ref)
    z = short_conv(v_ref, pv_ref)
    sk = sk_ref[...]
    fwd = (fst_ref[...], twr_ref[...], twi_ref[...], wbig_ref[...])
    inv = (gst_ref[...], twr_ref[...], twi_ref[...], wbigc_ref[...])
    for o, (gate, hr_ref, hi_ref) in enumerate(((x1, h1r_ref, h1i_ref), (x2, h2r_ref, h2i_ref))):
        zr, zi = _fft_fwd(z, *fwd)
        hr, hi = hr_ref[...], hi_ref[...]
        y = _fft_inv(zr * hr - zi * hi, zr * hi + zi * hr, *inv)
        z = gate * (y + sk[:, o:o + 1, :] * z)
    o_ref[...] = z


def _hy_conv(projt4, pp, ps, hr, hi, consts, cb=8):
    fst, gst, twr, twi, wbig, wbigc = consts
    _, bsz, r, _ = projt4.shape
    n1 = twr.shape[0]
    nj = HY_DIM // cb
    full = lambda a: pl.BlockSpec(a.shape, lambda j, b: (0,) * a.ndim)

    def xs(off):
        return pl.BlockSpec((cb, None, r, LANES), lambda j, b: (off * nj + j, b, 0, 0))

    def ps_(off):
        return pl.BlockSpec((cb, SUBLANES, LANES), lambda j, b: (off * nj + j, 0, 0))

    def hs(off):
        return pl.BlockSpec((cb, n1, LANES), lambda j, b: (off * nj + j, 0, 0))

    return pl.pallas_call(
        _hy_conv_kernel, grid=(nj, bsz),
        in_specs=[xs(0), xs(1), xs(2), ps_(0), ps_(1), ps_(2), ps_(0), hs(0), hs(0), hs(1), hs(1),
                  full(fst), full(gst), full(twr), full(twi), full(wbig), full(wbigc)],
        out_specs=pl.BlockSpec((cb, None, r, LANES), lambda j, b: (j, b, 0, 0)),
        out_shape=jax.ShapeDtypeStruct((HY_DIM, bsz, r, LANES), F32),
        compiler_params=_params("parallel", "arbitrary"), name="hy_conv",
    )(projt4, projt4, projt4, pp, pp, pp, ps, hr, hi, hr, hi, fst, gst, twr, twi, wbig, wbigc)


def _fft_consts(t_len):
    n = 2 * t_len
    n1 = n // LANES
    r = t_len // LANES
    k1 = np.arange(n1)[:, None]
    th1 = 2.0 * np.pi * k1 * np.arange(r)[None, :] / n1
    fst = np.concatenate([np.cos(th1), -np.sin(th1)], axis=0)
    gst = np.concatenate([np.cos(th1).T, -np.sin(th1).T], axis=1) / n
    tht = 2.0 * np.pi * k1 * np.arange(LANES)[None, :] / n
    twr, twi = np.cos(tht), -np.sin(tht)
    th2 = 2.0 * np.pi * np.arange(LANES)[:, None] * np.arange(LANES)[None, :] / LANES
    cr, ci = np.cos(th2), -np.sin(th2)
    wbig = np.block([[cr, ci], [-ci, cr]])
    wbigc = np.block([[cr, -ci], [ci, cr]])
    return tuple(jnp.asarray(a, F32) for a in (fst, gst, twr, twi, wbig, wbigc))


def _hy_positions(t_len):
    pos = np.arange(t_len, dtype=np.float32)[:, None]
    t = pos / np.float32(max(t_len - 1, 1))
    bands = np.linspace(1e-4, HY_BANDS - 1, HY_BANDS, dtype=np.float32)[None, :]
    ang = bands * np.float32(2.0 * math.pi / t_len) * pos
    z = np.concatenate([t, np.cos(ang), -np.sin(ang)], -1).astype(np.float32)
    return jnp.asarray(np.pad(z, ((0, 0), (0, HY_FILT_HID - HY_EMB))))


def _hy_deltas():
    max_decay = math.log(HY_DECAY_TARGET) / HY_FAST_DECAY_PCT
    min_decay = math.log(HY_DECAY_TARGET) / HY_SLOW_DECAY_PCT
    d = np.abs(np.linspace(min_decay, max_decay, HY_DIM, dtype=np.float32))
    return jnp.asarray(np.tile(d, 4)[:, None])


def _lane_rep(a, rows=SUBLANES):
    ch, k = a.shape
    a = jnp.pad(a, ((0, 0), (0, rows - k)))
    return jnp.broadcast_to(a[:, :, None], (ch, rows, LANES))


def _out_even_kernel(x_ref, yat_ref, of_ref, ob_ref, zg_ref, nw_ref, wa_ref, wb_ref, g_ref, b_ref,
                     o_ref):
    mix = _mm(yat_ref[...], wa_ref[...], _TN)
    nw = nw_ref[...]
    for h in range(GDN_HEADS):
        sl = slice(h * GDN_DK, (h + 1) * GDN_DK)
        o = of_ref[:, sl] + ob_ref[:, sl]
        y = o * lax.rsqrt(jnp.mean(o * o, -1, keepdims=True) + RMS_EPS) * nw * _silu(zg_ref[:, sl])
        mix = mix + _mm(y, wb_ref[sl, :])
    o_ref[...] = _layer_norm(DEEPNORM_ALPHA * x_ref[...] + mix, g_ref[...], b_ref[...])


def _out_even(x2, yat, of, ob, projm, norm_w, wa, wb, g, b, tm=256):
    n, d = x2.shape
    row = lambda w: pl.BlockSpec((tm, w), lambda i: (i, 0))
    full = lambda a: pl.BlockSpec(a.shape, lambda i: (0,) * a.ndim)
    return pl.pallas_call(
        _out_even_kernel, grid=(n // tm,),
        in_specs=[row(d), pl.BlockSpec((HY_DIM, tm), lambda i: (0, i)), row(512), row(512),
                  pl.BlockSpec((tm, 512), lambda i: (i, 3)),
                  full(norm_w), full(wa), full(wb), full(g), full(b)],
        out_specs=row(d), out_shape=jax.ShapeDtypeStruct((n, d), F32),
        compiler_params=_params("parallel"), name="out_even",
    )(x2, yat, of, ob, projm, norm_w, wa, wb, g, b)


def _even_layer(x2, bsz, t_len, w_in, hy_conv_w, hy_conv_b, f_w1, f_b1, f_w2, f_b2, f_w3, f_b3,
                f_w4, f_freq, hy_skip, gdn_conv_w, gdn_a_log, gdn_dt_bias, gdn_norm_w, w_out,
                ln_g, ln_b):
    d = x2.shape[1]
    hyw = 3 * HY_DIM
    wm = jnp.concatenate([w_in[:, hyw:], jnp.zeros((d, LANES - 12), F32)], axis=1).astype(BF16)
    wt = w_in[:, :hyw].T.astype(BF16)
    projm, projt = _project(x2, wm, wt)
    consts = _fft_consts(t_len)
    w1p = jnp.pad(f_w1, ((0, HY_FILT_HID - HY_EMB), (0, 0)))
    filt = _hy_filters(_hy_positions(t_len), w1p, f_b1[None], f_w2, f_b2[None], f_w3, f_b3[None],
                       f_freq[None], f_w4.T, _hy_deltas(), t_len)
    r = t_len // LANES
    hr, hi = _hy_spectra(filt.reshape(4 * HY_DIM, r, LANES), consts)
    pp = _lane_rep(jnp.concatenate([hy_conv_w.T, hy_conv_b[:, None]], axis=1))
    ps = _lane_rep(hy_skip.T)
    yat = _hy_conv(projt.reshape(hyw, bsz, r, LANES), pp, ps, hr, hi, consts)
    yat = yat.reshape(HY_DIM, bsz * t_len)
    alog = jnp.zeros((1, LANES), F32).at[0, :8].set(gdn_a_log.reshape(-1))
    dtb = jnp.zeros((1, LANES), F32).at[0, :8].set(gdn_dt_bias.reshape(-1))
    qn, kn, vn, gates = _gdn_pre(projm, gdn_conv_w, alog, dtb, bsz, t_len)
    of, ob = _gdn_scan(qn, kn, vn, gates, bsz, t_len)
    return _out_even(x2, yat, of, ob, projm, gdn_norm_w[None], w_out[:HY_DIM].astype(BF16),
                     w_out[HY_DIM:].astype(BF16), ln_g[None], ln_b[None])


def _hg_scan_kernel(qf_ref, ff_ref, vf_ref, qb_ref, fb_ref, vb_ref, low_ref, of_ref, ob_ref,
                    state, q_s, k_s, g_s, *, layer):
    n = pl.program_id(1)

    @pl.when(n == 0)
    def _():
        state[...] = jnp.zeros_like(state)

    tb = qf_ref.shape[0]
    c = HG_CHUNK
    nsub = tb // c
    low = low_ref[...]
    e = jnp.exp(low - jnp.max(low, axis=0, keepdims=True))
    p = e / jnp.sum(e, axis=0, keepdims=True)
    lb = jnp.sum(p[1:layer + 1], axis=0, keepdims=True)
    rb = lax.broadcasted_iota(jnp.int32, (tb, tb), 0)
    cb = lax.broadcasted_iota(jnp.int32, (tb, tb), 1)
    same = (rb // c) == (cb // c)
    rowc = lax.broadcasted_iota(jnp.int32, (c, HG_DK), 0)
    dirs = ((qf_ref, ff_ref, vf_ref, of_ref), (qb_ref, fb_ref, vb_ref, ob_ref))
    for d, (q_ref, f_ref, v_ref, o_ref) in enumerate(dirs):
        fg = lb + (1.0 - lb) * _sigmoid(f_ref[...])
        tri = (same & ((cb <= rb) if d == 0 else (cb >= rb))).astype(BF16)
        g_s[...] = _sel_mm(tri, jnp.log(fg))
        k_s[...] = 1.0 - fg
        q_s[...] = _silu(q_ref[...])

        def body(s, carry):
            sub = s if d == 0 else nsub - 1 - s
            rows = pl.ds(pl.multiple_of(sub * c, c), c)
            for h in range(HG_HEADS):
                sl = slice(h * HG_DK, (h + 1) * HG_DK)
                qc, kc, gc, vc = q_s[rows, sl], k_s[rows, sl], g_s[rows, sl], v_ref[rows, sl]
                g_last = gc[c - 1:c, :] if d == 0 else gc[0:1, :]
                st = state[d, h]
                out = _mm(qc * jnp.exp(gc), st, _NT)
                for j in range(c):
                    mask = (rowc >= j) if d == 0 else (rowc <= j)
                    dec = jnp.where(mask, jnp.exp(gc - gc[j:j + 1, :]), 0.0)
                    a_col = jnp.sum(qc * kc[j:j + 1, :] * dec, axis=1, keepdims=True)
                    out = out + a_col * vc[j:j + 1, :]
                o_ref[rows, sl] = out
                k_dec = kc * jnp.exp(g_last - gc)
                state[d, h] = st * jnp.exp(g_last) + _mm(vc, k_dec, _TN)
            return carry

        lax.fori_loop(0, nsub, body, 0)


def _hg_scan(proj, hg_lower, layer, bsz, t_len, tb=128):
    n = proj.shape[0]
    nb = t_len // tb

    def spec(colb, rev):
        if rev:
            return pl.BlockSpec((tb, 512), lambda b, i: (b * nb + nb - 1 - i, colb))
        return pl.BlockSpec((tb, 512), lambda b, i: (b * nb + i, colb))

    return pl.pallas_call(
        functools.partial(_hg_scan_kernel, layer=layer), grid=(bsz, nb),
        in_specs=[spec(0, False), spec(1, False), spec(3, False),
                  spec(0, True), spec(2, True), spec(3, True),
                  pl.BlockSpec(hg_lower.shape, lambda b, i: (0, 0))],
        out_specs=[spec(0, False), spec(0, True)],
        out_shape=[jax.ShapeDtypeStruct((n, 512), F32)] * 2,
        scratch_shapes=[pltpu.VMEM((2, HG_HEADS, HG_DK, HG_DK), F32)]
        + [pltpu.VMEM((tb, 512), F32)] * 3,
        compiler_params=_params("parallel", "arbitrary"), name="hg_scan",
    )(proj, proj, proj, proj, proj, proj, hg_lower)


def _rw_pre_kernel(rc, rp, rn, kc, kp, kn, vc, vp, vn, lc, lp, ln, mu_ref, w0_ref, w2_ref,
                   a0_ref, a2_ref, g2_ref, kk_ref, ka_ref, bd_ref,
                   r_o, k_o, v_o, kk_o, kka_o, lwf_o, lwb_o, g_o, scr):
    i = pl.program_id(1)
    first = i == 0
    last = i == pl.num_programs(1) - 1
    tb = rc.shape[0]

    def shifted(cur_ref, prev_ref, next_ref, col0):
        w = cur_ref.shape[1]
        cur = cur_ref[...]
        scr[pl.ds(SUBLANES, tb), pl.ds(0, w)] = cur
        scr[pl.ds(0, SUBLANES), pl.ds(0, w)] = jnp.where(first, 0.0, prev_ref[...])
        scr[pl.ds(SUBLANES + tb, SUBLANES), pl.ds(0, w)] = jnp.where(last, 0.0, next_ref[...])
        prev = scr[pl.ds(SUBLANES - 1, tb), pl.ds(0, w)]
        nxt = scr[pl.ds(SUBLANES + 1, tb), pl.ds(0, w)]
        mu = mu_ref[:, col0:col0 + w]
        return cur + mu[0:1] * (prev - cur) + mu[1:2] * (nxt - cur)

    r = shifted(rc, rp, rn, 0)
    k = shifted(kc, kp, kn, RW_DIM)
    v = shifted(vc, vp, vn, 2 * RW_DIM)
    lo = shifted(lc, lp, ln, 3 * RW_DIM)
    lo_wa, lo_g = lo[:, :LANES], lo[:, LANES:]
    th = jnp.tanh(lo_wa)
    for d, o_ref in enumerate((lwf_o, lwb_o)):
        w_raw = w0_ref[d:d + 1, :] + _mm(th, w2_ref[d])
        o_ref[...] = -jnp.exp(-_softplus(-w_raw) - 0.5)
    a = _sigmoid(a0_ref[...] + _mm(lo_wa, a2_ref[...]))
    g_o[...] = _mm(_sigmoid(lo_g), g2_ref[...])
    kx = k * kk_ref[...]
    kk = kx * lax.rsqrt(_mm(kx * kx, bd_ref[...], exact=True) + L2_EPS)
    r_o[...] = r
    v_o[...] = v
    k_o[...] = k * (1.0 + (a - 1.0) * ka_ref[...])
    kk_o[...] = kk
    kka_o[...] = kk * a


def _rw_pre(proj, mu, w0, w2p, a0, a2p, g2, k_k, k_a, bd, bsz, t_len, tb=256):
    n = proj.shape[0]
    nb = t_len // tb
    base = 2560
    specs = []
    for off, w in ((0, 512), (512, 512), (1024, 512), (1536, 256)):
        specs += _halo_specs(w, (base + off) // w, tb, t_len, bsz)
    full = lambda a: pl.BlockSpec(a.shape, lambda b, i: (0,) * a.ndim)
    out_spec = pl.BlockSpec((tb, RW_DIM), lambda b, i: (b * nb + i, 0))
    consts = (mu, w0, w2p, a0, a2p, g2, k_k, k_a, bd)
    return pl.pallas_call(
        _rw_pre_kernel, grid=(bsz, nb),
        in_specs=specs + [full(a) for a in consts],
        out_specs=[out_spec] * 8,
        out_shape=[jax.ShapeDtypeStruct((n, RW_DIM), F32)] * 8,
        scratch_shapes=[pltpu.VMEM((tb + 2 * SUBLANES, RW_DIM), F32)],
        compiler_params=_params("parallel", "parallel"), name="rw_pre",
    )(*([proj] * 12), *consts)


def _rw_scan_kernel(rf, kf, vf, bf, af, wf, rb, kb, vb, bb, ab, wb, yf, yb, state):
    n = pl.program_id(1)

    @pl.when(n == 0)
    def _():
        state[...] = jnp.zeros_like(state)

    c = RW_CHUNK
    gw = RW_GROUP * RW_HD
    gc = RW_GROUP * c
    row = lax.broadcasted_iota(jnp.int32, (c, c), 0)
    col = lax.broadcasted_iota(jnp.int32, (c, c), 1)
    rr = lax.broadcasted_iota(jnp.int32, (gc, gc), 0)
    cc = lax.broadcasted_iota(jnp.int32, (gc, gc), 1)
    same = (rr // c) == (cc // c)
    eye = (rr == cc).astype(F32)
    head_lanes = (lax.broadcasted_iota(jnp.int32, (gc, gw), 0) // c
                  == lax.broadcasted_iota(jnp.int32, (gc, gw), 1) // RW_HD)

    def stack(x):
        return jnp.where(head_lanes, jnp.concatenate([x] * RW_GROUP, axis=0), 0.0)

    dirs = ((rf, kf, vf, bf, af, wf, yf), (rb, kb, vb, bb, ab, wb, yb))
    for ib, d in [(ib, d) for ib in range(rf.shape[0]) for d in range(2)]:
        r_ref, k_ref, v_ref, b_ref, a_ref, w_ref, y_ref = dirs[d]
        incl = same & ((cc <= rr) if d == 0 else (cc >= rr))
        strict = same & ((cc < rr) if d == 0 else (cc > rr))
        lw = w_ref[ib]
        tri = ((col <= row) if d == 0 else (col >= row)).astype(BF16)
        g_in = _sel_mm(tri, lw)
        g_ex = g_in - lw
        g_end = g_in[c - 1:c, :] if d == 0 else g_in[0:1, :]
        e_neg = jnp.exp(-g_in)
        e_end = jnp.exp(g_end - g_in)
        k_all, a_all, v_all = k_ref[ib], a_ref[ib], v_ref[ib]
        rt_all = r_ref[ib] * jnp.exp(g_in)
        bt_all = b_ref[ib] * jnp.exp(g_ex)
        kh_all, ah_all = k_all * e_neg, a_all * e_neg
        kbar_all, abar_all = k_all * e_end, a_all * e_end
        dec_end = jnp.exp(g_end)
        for p in range(RW_DIM // gw):
            sl = slice(p * gw, (p + 1) * gw)
            lhs = jnp.concatenate([stack(rt_all[:, sl]), stack(bt_all[:, sl])], axis=0)
            vs = stack(v_all[:, sl])
            ak = _mm(lhs, stack(kh_all[:, sl]), _NT)
            aa = _mm(lhs, stack(ah_all[:, sl]), _NT)
            a_rk = jnp.where(incl, ak[:gc], 0.0)
            a_ra = jnp.where(incl, aa[:gc], 0.0)
            a_bk = jnp.where(strict, ak[gc:], 0.0)
            a_ba = jnp.where(strict, aa[gc:], 0.0)
            t_inv = _unit_tri_inv(a_ba, eye, c)
            m = state[ib, d, p]
            xm = _mm(lhs, m, _NT)
            us = _mm3(t_inv, xm[gc:] + _mm(a_bk, vs))
            vu = jnp.concatenate([vs, us], axis=0)
            ys = xm[:gc] + _mm(jnp.concatenate([a_rk, -a_ra], axis=1), vu)
            y = ys[:c]
            for h in range(1, RW_GROUP):
                y = y + ys[h * c:(h + 1) * c]
            y_ref[ib, :, sl] = y
            kab = jnp.concatenate([stack(kbar_all[:, sl]), -stack(abar_all[:, sl])], axis=0)
            state[ib, d, p] = m * dec_end[:, sl] + _mm(vu, kab, _TN)


def _rw_scan(r, k, v, kk, kka, lwf, lwb, bsz, t_len, bb=SCAN_BATCH):
    n = r.shape[0]
    c = RW_CHUNK
    gw = RW_GROUP * RW_HD
    nc = t_len // c
    r, k, v, kk, kka, lwf, lwb = (a.reshape(bsz, t_len, RW_DIM) for a in (r, k, v, kk, kka, lwf, lwb))
    fwd = pl.BlockSpec((bb, c, RW_DIM), lambda g, i: (g, i, 0))
    bwd = pl.BlockSpec((bb, c, RW_DIM), lambda g, i: (g, nc - 1 - i, 0))
    yf, yb = pl.pallas_call(
        _rw_scan_kernel, grid=(bsz // bb, nc),
        in_specs=[fwd] * 6 + [bwd] * 6, out_specs=[fwd, bwd],
        out_shape=[jax.ShapeDtypeStruct((bsz, t_len, RW_DIM), F32)] * 2,
        scratch_shapes=[pltpu.VMEM((bb, 2, RW_DIM // gw, gw, gw), F32)],
        compiler_params=_params("parallel", "arbitrary"), name="rw_scan",
    )(r, k, v, kk, kka, lwf, r, k, v, kk, kka, lwb)
    return yf.reshape(n, RW_DIM), yb.reshape(n, RW_DIM)


def _out_odd_kernel(x_ref, hf_ref, hb_ref, hg_ref, hnw_ref, yf_ref, yb_ref, r_ref, k_ref, v_ref,
                    gate_ref, rk_ref, lw_ref, lb_ref, bd_ref, wc_ref, wd_ref, g_ref, b_ref, o_ref):
    nw = hnw_ref[...]
    mix = None
    for h in range(HG_HEADS):
        sl = slice(h * HG_DK, (h + 1) * HG_DK)
        o = hf_ref[:, sl] + hb_ref[:, sl]
        y = o * lax.rsqrt(jnp.mean(o * o, -1, keepdims=True) + RMS_EPS) * nw * _silu(hg_ref[:, sl])
        part = _mm(y, wc_ref[sl, :])
        mix = part if mix is None else mix + part
    bd = bd_ref[...]
    y = yf_ref[...] + yb_ref[...]
    yc = y - _mm(y, bd, exact=True) * (1.0 / RW_HD)
    var = _mm(yc * yc, bd, exact=True) * (1.0 / RW_HD)
    yn = yc * lax.rsqrt(var + RW_GN_EPS) * lw_ref[...] + lb_ref[...]
    bonus = _mm(r_ref[...] * k_ref[...] * rk_ref[...], bd, exact=True) * v_ref[...]
    mix = mix + _mm((yn + bonus) * gate_ref[...], wd_ref[...])
    o_ref[...] = _layer_norm(DEEPNORM_ALPHA * x_ref[...] + mix, g_ref[...], b_ref[...])


def _out_odd(x2, hf, hb, proj, hnw, yf, yb, r, k, v, gate, r_k, lnx_w, lnx_b, bd, wc, wd, g, b,
             tm=256):
    n, d = x2.shape
    row = lambda w: pl.BlockSpec((tm, w), lambda i: (i, 0))
    full = lambda a: pl.BlockSpec(a.shape, lambda i: (0,) * a.ndim)
    consts = (r_k, lnx_w, lnx_b, bd, wc, wd, g, b)
    return pl.pallas_call(
        _out_odd_kernel, grid=(n // tm,),
        in_specs=[row(d), row(512), row(512), pl.BlockSpec((tm, 512), lambda i: (i, 4)), full(hnw)]
        + [row(512)] * 6 + [full(a) for a in consts],
        out_specs=row(d), out_shape=jax.ShapeDtypeStruct((n, d), F32),
        compiler_params=_params("parallel"), name="out_odd",
    )(x2, hf, hb, proj, hnw, yf, yb, r, k, v, gate, *consts)


def _odd_layer(x2, bsz, t_len, layer, w_in, hg_lower, hg_norm_w, rw_mu, rw_w0, rw_w2, rw_a0, rw_a2,
               rw_g2, rw_k_k, rw_k_a, rw_r_k, rw_lnx_w, rw_lnx_b, w_out, ln_g, ln_b):
    proj = _project(x2, w_in.astype(BF16))
    hf, hb = _hg_scan(proj, hg_lower, layer, bsz, t_len)
    idx = np.arange(RW_DIM) // RW_HD
    bd = jnp.asarray(idx[:, None] == idx[None, :], F32)
    lora = rw_w2.shape[1]
    w2p = jnp.pad(rw_w2, ((0, 0), (0, LANES - lora), (0, 0)))
    a2p = jnp.pad(rw_a2, ((LANES - rw_a2.shape[0], 0), (0, 0)))
    r, k, v, kk, kka, lwf, lwb, gate = _rw_pre(proj, rw_mu, rw_w0, w2p, rw_a0[None], a2p, rw_g2,
                                               rw_k_k[None], rw_k_a[None], bd, bsz, t_len)
    yf, yb = _rw_scan(r, k, v, kk, kka, lwf, lwb, bsz, t_len)
    return _out_odd(x2, hf, hb, proj, hg_norm_w[None], yf, yb, r, k, v, gate,
                    rw_r_k.reshape(1, RW_DIM), rw_lnx_w[None], rw_lnx_b[None], bd,
                    w_out[:512].astype(BF16), w_out[512:].astype(BF16), ln_g[None], ln_b[None])


def kernel(x, ev_w_in, hy_conv_w, hy_conv_b, hy_filt_w1, hy_filt_b1, hy_filt_w2, hy_filt_b2, hy_filt_w3, hy_filt_b3, hy_filt_w4, hy_filt_freq, hy_skip, gdn_conv_w, gdn_a_log, gdn_dt_bias, gdn_norm_w, ev_w_out, od_w_in, hg_lower, hg_norm_w, rw_mu, rw_w0, rw_w2, rw_a0, rw_a2, rw_g2, rw_k_k, rw_k_a, rw_r_k, rw_lnx_w, rw_lnx_b, od_w_out, ln_g, ln_b, mlp_w1, mlp_w2):
    bsz, t_len, d = x.shape
    x2 = x.reshape(bsz * t_len, d)
    for layer in range(DEPTH):
        i = layer // 2
        if layer % 2 == 0:
            x2 = _even_layer(x2, bsz, t_len, ev_w_in[i], hy_conv_w[i], hy_conv_b[i], hy_filt_w1[i],
                             hy_filt_b1[i], hy_filt_w2[i], hy_filt_b2[i], hy_filt_w3[i],
                             hy_filt_b3[i], hy_filt_w4[i], hy_filt_freq[i], hy_skip[i],
                             gdn_conv_w[i], gdn_a_log[i], gdn_dt_bias[i], gdn_norm_w[i],
                             ev_w_out[i], ln_g[layer, 0], ln_b[layer, 0])
        else:
            x2 = _odd_layer(x2, bsz, t_len, layer, od_w_in[i], hg_lower, hg_norm_w[i], rw_mu[i],
                            rw_w0[i], rw_w2[i], rw_a0[i], rw_a2[i], rw_g2[i], rw_k_k[i], rw_k_a[i],
                            rw_r_k[i], rw_lnx_w[i], rw_lnx_b[i], od_w_out[i], ln_g[layer, 0],
                            ln_b[layer, 0])
        x2 = _mlp_block(x2, mlp_w1[layer].astype(BF16), mlp_w2[layer].astype(BF16),
                        ln_g[layer, 1][None], ln_b[layer, 1][None])
    return x2.reshape(bsz, t_len, d)
```

```python
import functools
import math

import numpy as np
import jax
import jax.numpy as jnp
from jax import lax
from jax.experimental import pallas as pl
from jax.experimental.pallas import tpu as pltpu

F32 = jnp.float32
BF16 = jnp.bfloat16

D_MODEL = 1024
DEPTH = 2
D_FF = 4 * D_MODEL
DEEPNORM_ALPHA = (2.0 * DEPTH) ** 0.25
LN_EPS = 1e-5
RMS_EPS = 1e-6
L2_EPS = 1e-6

HY_DIM = 512
HY_EMB = 33
HY_BANDS = 16
HY_FILT_HID = 64
HY_DECAY_TARGET = 1e-2
HY_FAST_DECAY_PCT = 0.3
HY_SLOW_DECAY_PCT = 1.5

GDN_HEADS = 4
GDN_DK = 128
GDN_CHUNK = 64

HG_HEADS = 4
HG_DK = 128
HG_CHUNK = 16
HG_SUBS = 4

RW_HEADS = 8
RW_HD = 64
RW_DIM = 512
RW_CHUNK = 64
RW_GROUP = 4
RW_GN_EPS = 64e-5

SCAN_BATCH = 4

LANES = 128
SUBLANES = 8
VMEM_LIMIT = 56 * 1024 * 1024

_NN = (((1,), (0,)), ((), ()))
_NT = (((1,), (1,)), ((), ()))
_TN = (((0,), (0,)), ((), ()))


def _mm(a, b, dims=_NN, exact=False):
    if exact:
        return lax.dot_general(a, b, dims, precision=lax.Precision.HIGHEST,
                               preferred_element_type=F32)
    return lax.dot_general(a.astype(BF16), b.astype(BF16), dims, preferred_element_type=F32)


def _params(*sem):
    return pltpu.CompilerParams(dimension_semantics=sem, vmem_limit_bytes=VMEM_LIMIT)


def _sigmoid(x):
    return 1.0 / (1.0 + jnp.exp(-x))


def _silu(x):
    return x * _sigmoid(x)


def _softplus(x):
    return jnp.maximum(x, 0.0) + jnp.log(1.0 + jnp.exp(-jnp.abs(x)))


def _layer_norm(v, g, b):
    mean = jnp.mean(v, axis=-1, keepdims=True)
    vc = v - mean
    var = jnp.mean(vc * vc, axis=-1, keepdims=True)
    return vc * lax.rsqrt(var + LN_EPS) * g + b


def _col(x, idx, lane):
    return jnp.sum(jnp.where(lane == idx, x, 0.0), axis=1, keepdims=True)


def _split2(a):
    hi = a.astype(BF16)
    return hi, (a - hi.astype(F32)).astype(BF16)


def _mm3(a, b, dims=_NN):
    ah, al = _split2(a)
    bh, bl = _split2(b)
    dot = functools.partial(lax.dot_general, dimension_numbers=dims, preferred_element_type=F32)
    return dot(ah, bh) + dot(ah, bl) + dot(al, bh)


def _sel_mm(sel, x, dims=_NN, sel_first=True):
    hi = x.astype(BF16)
    r1 = x - hi.astype(F32)
    mid = r1.astype(BF16)
    lo = (r1 - mid.astype(F32)).astype(BF16)
    dot = functools.partial(lax.dot_general, dimension_numbers=dims, preferred_element_type=F32)
    if sel_first:
        return dot(sel, hi) + dot(sel, mid) + dot(sel, lo)
    return dot(hi, sel) + dot(mid, sel) + dot(lo, sel)


def _expand_bd(x_cat, same):
    g = x_cat.shape[1] // x_cat.shape[0]
    return jnp.where(same, jnp.concatenate([x_cat] * g, axis=0), 0.0)


def _unit_tri_inv_cat(a_cat, eye_cat, same, depth):
    c = a_cat.shape[0]
    p = -a_cat
    t = eye_cat + p
    p = _mm(p, _expand_bd(p, same))
    yield
    k = 4
    while k < depth:
        both = _mm(jnp.concatenate([p, t], axis=0), _expand_bd(p, same))
        yield
        p, t = both[:c], t + both[c:]
        k *= 2
    return t + _mm(t, _expand_bd(p, same))


def _interleave(chains):
    active = list(chains)
    while active:
        still = []
        for g in active:
            try:
                next(g)
                still.append(g)
            except StopIteration:
                pass
        active = still


def _proj_kernel(x_ref, w_ref, o_ref):
    o_ref[...] = _mm(x_ref[...], w_ref[...])


def _proj_t_kernel(x_ref, w_ref, wt_ref, o_ref, ot_ref):
    xb = x_ref[...].astype(BF16)
    o_ref[...] = _mm(xb, w_ref[...])
    ot_ref[...] = _mm(wt_ref[...], xb, _NT)


def _project(x2, w, wt=None, tm=256):
    n, d = x2.shape
    m = w.shape[1]
    grid = (n // tm,)
    x_spec = pl.BlockSpec((tm, d), lambda i: (i, 0))
    w_spec = pl.BlockSpec((d, m), lambda i: (0, 0))
    o_spec = pl.BlockSpec((tm, m), lambda i: (i, 0))
    if wt is None:
        return pl.pallas_call(
            _proj_kernel, grid=grid, in_specs=[x_spec, w_spec], out_specs=o_spec,
            out_shape=jax.ShapeDtypeStruct((n, m), F32), compiler_params=_params("parallel"),
            name="proj")(x2, w)
    mt = wt.shape[0]
    return pl.pallas_call(
        _proj_t_kernel, grid=grid,
        in_specs=[x_spec, w_spec, pl.BlockSpec((mt, d), lambda i: (0, 0))],
        out_specs=[o_spec, pl.BlockSpec((mt, tm), lambda i: (0, i))],
        out_shape=[jax.ShapeDtypeStruct((n, m), F32), jax.ShapeDtypeStruct((mt, n), F32)],
        compiler_params=_params("parallel"), name="proj_t")(x2, w, wt)


def _mlp_kernel(x_ref, w1_ref, w2_ref, g_ref, b_ref, o_ref, acc_ref, xb_ref):
    j = pl.program_id(1)

    @pl.when(j == 0)
    def _():
        xb_ref[...] = x_ref[...].astype(BF16)

    h = jnp.maximum(_mm(xb_ref[...], w1_ref[...]), 0.0)
    part = _mm(h * h, w2_ref[...])

    @pl.when(j == 0)
    def _():
        acc_ref[...] = part

    @pl.when(j > 0)
    def _():
        acc_ref[...] += part

    @pl.when(j == pl.num_programs(1) - 1)
    def _():
        o_ref[...] = _layer_norm(DEEPNORM_ALPHA * x_ref[...] + acc_ref[...], g_ref[...], b_ref[...])


def _mlp_block(x2, w1, w2, g, b, tm=1024, tf=512):
    n, d = x2.shape
    f = w1.shape[1]
    return pl.pallas_call(
        _mlp_kernel, grid=(n // tm, f // tf),
        in_specs=[pl.BlockSpec((tm, d), lambda i, j: (i, 0)),
                  pl.BlockSpec((d, tf), lambda i, j: (0, j)),
                  pl.BlockSpec((tf, d), lambda i, j: (j, 0)),
                  pl.BlockSpec((1, d), lambda i, j: (0, 0)),
                  pl.BlockSpec((1, d), lambda i, j: (0, 0))],
        out_specs=pl.BlockSpec((tm, d), lambda i, j: (i, 0)),
        out_shape=jax.ShapeDtypeStruct((n, d), F32),
        scratch_shapes=[pltpu.VMEM((tm, d), F32), pltpu.VMEM((tm, d), BF16)],
        compiler_params=_params("parallel", "arbitrary"), name="mlp")(x2, w1, w2, g, b)


def _fill_halo(scr, cur, prev8, next8, first, last, halo):
    tb = cur.shape[0]
    scr[pl.ds(SUBLANES, tb), :] = cur
    scr[pl.ds(0, SUBLANES), :] = jnp.where(first, 0.0, prev8)
    scr[pl.ds(SUBLANES + tb, SUBLANES), :] = jnp.where(last, 0.0, next8)


def _halo_specs(width, col_block, tb, t_len, bsz):
    nb = t_len // tb
    r8 = tb // SUBLANES
    tot8 = t_len // SUBLANES

    def cur(b, i):
        return (b * nb + i, col_block)

    def prev(b, i):
        return (jnp.maximum(b * tot8 + i * r8 - 1, 0), col_block)

    def nxt(b, i):
        return (jnp.minimum(b * tot8 + (i + 1) * r8, tot8 * bsz - 1), col_block)

    return [pl.BlockSpec((tb, width), cur), pl.BlockSpec((SUBLANES, width), prev),
            pl.BlockSpec((SUBLANES, width), nxt)]


def _gdn_pre_kernel(cur_ref, prev_ref, next_ref, ab_ref, cw_ref, alog_ref, dtb_ref,
                    q_ref, k_ref, v_ref, g_ref, scr):
    i = pl.program_id(1)
    tb = cur_ref.shape[0]
    _fill_halo(scr, cur_ref[...], prev_ref[...], next_ref[...], i == 0,
               i == pl.num_programs(1) - 1, 2)
    width = cw_ref.shape[0]
    acc = None
    for j in range(width):
        term = scr[pl.ds(SUBLANES - width // 2 + j, tb), :] * cw_ref[pl.ds(j, 1), :]
        acc = term if acc is None else acc + term
    qkv = _silu(acc)
    hd = GDN_HEADS * GDN_DK
    for h in range(GDN_HEADS):
        sl = slice(h * GDN_DK, (h + 1) * GDN_DK)
        q = qkv[:, sl]
        q_ref[:, sl] = q * lax.rsqrt(jnp.sum(q * q, -1, keepdims=True) + L2_EPS) * GDN_DK ** -0.5
        k = qkv[:, hd + h * GDN_DK: hd + (h + 1) * GDN_DK]
        k_ref[:, sl] = k * lax.rsqrt(jnp.sum(k * k, -1, keepdims=True) + L2_EPS)
    v_ref[...] = qkv[:, 2 * hd:]
    ab = ab_ref[...]
    lane = lax.broadcasted_iota(jnp.int32, ab.shape, 1)
    log_g = -jnp.exp(alog_ref[...]) * _softplus(ab + dtb_ref[...])
    g_ref[...] = jnp.where(lane < 2 * GDN_HEADS, log_g, _sigmoid(ab))


def _gdn_pre(proj, conv_w, a_log_row, dt_bias_row, bsz, t_len, tb=256):
    n = proj.shape[0]
    wq = 3 * GDN_HEADS * GDN_DK
    nb = t_len // tb
    specs = _halo_specs(wq, 0, tb, t_len, bsz)
    ab_col = (wq + GDN_HEADS * GDN_DK) // LANES
    out_spec = pl.BlockSpec((tb, 512), lambda b, i: (b * nb + i, 0))
    return pl.pallas_call(
        _gdn_pre_kernel, grid=(bsz, nb),
        in_specs=specs + [pl.BlockSpec((tb, LANES), lambda b, i: (b * nb + i, ab_col)),
                          pl.BlockSpec(conv_w.shape, lambda b, i: (0, 0)),
                          pl.BlockSpec((1, LANES), lambda b, i: (0, 0)),
                          pl.BlockSpec((1, LANES), lambda b, i: (0, 0))],
        out_specs=[out_spec, out_spec, out_spec,
                   pl.BlockSpec((tb, LANES), lambda b, i: (b * nb + i, 0))],
        out_shape=[jax.ShapeDtypeStruct((n, 512), F32)] * 3 + [jax.ShapeDtypeStruct((n, LANES), F32)],
        scratch_shapes=[pltpu.VMEM((tb + 2 * SUBLANES, wq), F32)],
        compiler_params=_params("parallel", "parallel"), name="gdn_pre",
    )(proj, proj, proj, proj, conv_w, a_log_row, dt_bias_row)


def _gdn_scan_kernel(qf, kf, vf, gf, qb, kb, vb, gb, of, ob, state):
    n = pl.program_id(1)

    @pl.when(n == 0)
    def _():
        state[...] = jnp.zeros_like(state)

    c = GDN_CHUNK
    nh = GDN_HEADS
    hc = nh * c
    wide = nh * GDN_DK
    row = lax.broadcasted_iota(jnp.int32, (c, c), 0)
    col = lax.broadcasted_iota(jnp.int32, (c, c), 1)
    lane = lax.broadcasted_iota(jnp.int32, (c, LANES), 1)
    rr = lax.broadcasted_iota(jnp.int32, (hc, hc), 0)
    cc = lax.broadcasted_iota(jnp.int32, (hc, hc), 1)
    same = (rr // c) == (cc // c)
    head_lanes = (lax.broadcasted_iota(jnp.int32, (hc, wide), 0) // c
                  == lax.broadcasted_iota(jnp.int32, (hc, wide), 1) // GDN_DK)
    lane_blk = lax.broadcasted_iota(jnp.int32, (1, hc), 1) // c
    row_cat = lax.broadcasted_iota(jnp.int32, (c, hc), 0)
    lane_cat = lax.broadcasted_iota(jnp.int32, (c, hc), 1)
    col_cat = lane_cat % c
    blk_cat = lane_cat // c
    eye_cat = (col_cat == row_cat).astype(F32)

    def col_cat_of(x, first):
        out = jnp.zeros((c, hc), F32)
        for h in range(nh):
            out = jnp.where(blk_cat == h, _col(x, first + h, lane), out)
        return out

    def stack_wide(x):
        return jnp.where(head_lanes, jnp.concatenate([x] * nh, axis=0), 0.0)

    def stack_narrow(x):
        return jnp.concatenate([x[:, h * GDN_DK:(h + 1) * GDN_DK] for h in range(nh)], axis=0)

    dirs = ((qf, kf, vf, gf, of), (qb, kb, vb, gb, ob))

    def chain(bb, d):
        q_ref, k_ref, v_ref, g_ref, o_ref = dirs[d]
        incl1 = (col <= row) if d == 0 else (col >= row)
        incl = (col_cat <= row_cat) if d == 0 else (col_cat >= row_cat)
        strict = (col_cat < row_cat) if d == 0 else (col_cat > row_cat)
        gates = g_ref[bb]
        tri = incl1.astype(BF16)
        gam_all = _sel_mm(tri, gates)
        tri4 = jnp.concatenate([tri] * nh, axis=0)
        gam_rows = _sel_mm(tri4, gates.T, _NT, sel_first=False)
        yield
        gam = jnp.concatenate([_col(gam_all, d * nh + h, lane) for h in range(nh)], axis=0)
        beta = jnp.concatenate([_col(gates, 2 * nh + h, lane) for h in range(nh)], axis=0)
        gam_row = jnp.zeros((1, hc), F32)
        for h in range(nh):
            gam_row = jnp.where(lane_blk == h, gam_rows[d * nh + h:d * nh + h + 1, :], gam_row)
        last = c - 1 if d == 0 else 0
        g_last = jnp.concatenate(
            [jnp.broadcast_to(gam[h * c + last:h * c + last + 1, :], (c, 1)) for h in range(nh)], axis=0)
        g_last_s = jnp.concatenate(
            [jnp.broadcast_to(gam[h * c + last:h * c + last + 1, :], (GDN_DK, 1)) for h in range(nh)],
            axis=0)
        diff = col_cat_of(gam_all, d * nh) - gam_row
        decay = jnp.where(incl, jnp.exp(jnp.where(incl, diff, 0.0)), 0.0)
        e_gam = jnp.exp(gam)
        k_in, q_in = k_ref[bb], q_ref[bb]
        k_w = stack_wide(k_in)
        kq = _mm(jnp.concatenate([k_in, q_in], axis=0), k_w, _NT)
        yield
        a_cat = jnp.where(strict, kq[:c] * decay * col_cat_of(gates, 2 * nh), 0.0)
        attn = _expand_bd(kq[c:] * decay, same)
        t_cat = yield from _unit_tri_inv_cat(a_cat, eye_cat, same, c)
        yield
        rhs = jnp.concatenate([stack_narrow(v_ref[bb]) * beta,
                               stack_narrow(k_in) * (beta * e_gam)], axis=1)
        sol = _mm(_expand_bd(t_cat, same), rhs)
        yield
        u, w = sol[:, :GDN_DK], sol[:, GDN_DK:]
        s = state[bb, d]
        w_w = jnp.where(head_lanes, jnp.concatenate([w] * nh, axis=1), 0.0)
        ws = _mm(jnp.concatenate([w_w, stack_wide(q_in) * e_gam], axis=0), s)
        yield
        v_new = u - ws[:hc]
        out = ws[hc:] + _mm(attn, v_new)
        o_ref[bb] = jnp.concatenate([out[h * c:(h + 1) * c] for h in range(nh)], axis=1)
        k_dec = k_w * jnp.exp(g_last - gam)
        state[bb, d] = s * jnp.exp(g_last_s) + _mm(k_dec, v_new, _TN)

    _interleave([chain(bb, d) for bb in range(qf.shape[0]) for d in range(2)])


def _gdn_scan(q, k, v, gates, bsz, t_len, bb=SCAN_BATCH):
    n = q.shape[0]
    c = GDN_CHUNK
    nc = t_len // c
    bb = math.gcd(bb, bsz)
    q, k, v, gates =(a.reshape(bsz, t_len, a.shape[1]) for a in (q, k, v, gates))

    def fwd(g, i):
        return (g, i, 0)

    def bwd(g, i):
        return (g, nc - 1 - i, 0)

    def specs(idx):
        wide = pl.BlockSpec((bb, c, 512), idx)
        return [wide, wide, wide, pl.BlockSpec((bb, c, LANES), idx)]

    of, ob = pl.pallas_call(
        _gdn_scan_kernel, grid=(bsz // bb, nc),
        in_specs=specs(fwd) + specs(bwd),
        out_specs=[pl.BlockSpec((bb, c, 512), fwd), pl.BlockSpec((bb, c, 512), bwd)],
        out_shape=[jax.ShapeDtypeStruct((bsz, t_len, 512), F32)] * 2,
        scratch_shapes=[pltpu.VMEM((bb, 2, GDN_HEADS * GDN_DK, GDN_DK), F32)],
        compiler_params=_params("parallel", "arbitrary"), name="gdn_scan",
    )(q, k, v, gates, q, k, v, gates)
    return of.reshape(n, 512), ob.reshape(n, 512)


_fft_mm = _mm
HY_CB = 16
HY_CHAINS = 2


def _hy_filter_kernel(z_ref, w1_ref, b1_ref, w2_ref, b2_ref, w3_ref, b3_ref, fr_ref, w4t_ref,
                      dl_ref, o_ref, *, t_len):
    i = pl.program_id(0)
    tl = o_ref.shape[1]
    fr = fr_ref[...]
    h = jnp.sin(fr * (_mm(z_ref[...], w1_ref[...], exact=True) + b1_ref[...]))
    h = jnp.sin(fr * (_mm(h, w2_ref[...], exact=True) + b2_ref[...]))
    h = jnp.sin(fr * (_mm(h, w3_ref[...], exact=True) + b3_ref[...]))
    ht = _mm(w4t_ref[...], h, _NT, exact=True)
    pos = (lax.broadcasted_iota(jnp.int32, (1, tl), 1) + i * tl).astype(F32)
    t_norm = pos / float(max(t_len - 1, 1))
    o_ref[...] = ht * jnp.exp(-t_norm * dl_ref[...])


def _hy_filters(z, w1p, b1, w2, b2, w3, b3, freq, w4t, deltas, t_len):
    tl = min(512, t_len)
    nch = w4t.shape[0]
    full = lambda a: pl.BlockSpec(a.shape, lambda i: (0,) * a.ndim)
    return pl.pallas_call(
        functools.partial(_hy_filter_kernel, t_len=t_len), grid=(t_len // tl,),
        in_specs=[pl.BlockSpec((tl, z.shape[1]), lambda i: (i, 0))]
        + [full(a) for a in (w1p, b1, w2, b2, w3, b3, freq, w4t, deltas)],
        out_specs=pl.BlockSpec((nch, tl), lambda i: (0, i)),
        out_shape=jax.ShapeDtypeStruct((nch, t_len), F32),
        compiler_params=_params("parallel"), name="hy_filter",
    )(z, w1p, b1, w2, b2, w3, b3, freq, w4t, deltas)


def _fft_fwd(x3, fst, twr, twi, wbig):
    cb = x3.shape[0]
    n1 = twr.shape[0]
    y = _fft_mm(fst, jnp.concatenate([x3[c] for c in range(cb)], axis=1))
    yield
    yr, yi = y[:n1], y[n1:]
    twr_t = jnp.concatenate([twr] * cb, axis=1)
    twi_t = jnp.concatenate([twi] * cb, axis=1)
    tr = yr * twr_t - yi * twi_t
    ti = yr * twi_t + yi * twr_t
    cat = jnp.concatenate(
        [jnp.concatenate([tr[:, c * LANES:(c + 1) * LANES], ti[:, c * LANES:(c + 1) * LANES]], axis=1)
         for c in range(cb)], axis=0)
    z = _fft_mm(cat, wbig)
    yield
    return z[:, :LANES].reshape(cb, n1, LANES), z[:, LANES:].reshape(cb, n1, LANES)


def _fft_inv(zr, zi, gst, twr, twi, wbigc):
    cb, n1 = zr.shape[0], zr.shape[1]
    cat = jnp.concatenate([zr, zi], axis=-1).reshape(cb * n1, 2 * LANES)
    a = _fft_mm(cat, wbigc)
    yield
    ar, ai = a[:, :LANES].reshape(cb, n1, LANES), a[:, LANES:].reshape(cb, n1, LANES)
    tr = ar * twr + ai * twi
    ti = ai * twr - ar * twi
    tt = jnp.concatenate([jnp.concatenate([tr[c] for c in range(cb)], axis=1),
                          jnp.concatenate([ti[c] for c in range(cb)], axis=1)], axis=0)
    out = _fft_mm(gst, tt)
    yield
    return jnp.stack([out[:, c * LANES:(c + 1) * LANES] for c in range(cb)])


def _hy_spec_kernel(hf_ref, hb_ref, fst_ref, twr_ref, twi_ref, wbig_ref, hr_ref, hi_ref):
    args = (fst_ref[...], twr_ref[...], twi_ref[...], wbig_ref[...])
    out = {}

    def chain(name, ref):
        out[name] = yield from _fft_fwd(ref[...], *args)

    _interleave([chain("f", hf_ref), chain("b", hb_ref)])
    hr_ref[...] = out["f"][0] + out["b"][0]
    hi_ref[...] = out["f"][1] - out["b"][1]


def _hy_spectra(filt3, consts, cb=HY_CB):
    fst, gst, twr, twi, wbig, wbigc = consts
    r = filt3.shape[1]
    n1 = twr.shape[0]
    nj = HY_DIM // cb
    full = lambda a: pl.BlockSpec(a.shape, lambda o, j: (0,) * a.ndim)
    out_spec = pl.BlockSpec((cb, n1, LANES), lambda o, j: (o * nj + j, 0, 0))
    return pl.pallas_call(
        _hy_spec_kernel, grid=(2, nj),
        in_specs=[pl.BlockSpec((cb, r, LANES), lambda o, j: (2 * o * nj + j, 0, 0)),
                  pl.BlockSpec((cb, r, LANES), lambda o, j: ((2 * o + 1) * nj + j, 0, 0)),
                  full(fst), full(twr), full(twi), full(wbig)],
        out_specs=[out_spec, out_spec],
        out_shape=[jax.ShapeDtypeStruct((2 * HY_DIM, n1, LANES), F32)] * 2,
        compiler_params=_params("parallel", "parallel"), name="hy_spec",
    )(filt3, filt3, fst, twr, twi, wbig)


def _tshift(x3, back, lane, rowi):
    r = x3.shape[1]
    if back:
        rl = pltpu.roll(x3, 1, 2)
        rr = pltpu.roll(rl, 1, 1)
        edge = lane == 0
        out = jnp.where(edge, rr, rl)
        return jnp.where(edge & (rowi == 0), 0.0, out)
    rl = pltpu.roll(x3, LANES - 1, 2)
    rr = pltpu.roll(rl, r - 1, 1)
    edge = lane == LANES - 1
    out = jnp.where(edge, rr, rl)
    return jnp.where(edge & (rowi == r - 1), 0.0, out)


def _hy_conv_kernel(x1_ref, x2_ref, v_ref, p1_ref, p2_ref, pv_ref, sk_ref, h1r_ref, h1i_ref,
                    h2r_ref, h2i_ref, fst_ref, gst_ref, twr_ref, twi_ref, wbig_ref, wbigc_ref,
                    o_ref):
    cb, r, _ = x1_ref.shape
    half = cb // HY_CHAINS
    shape = (half, r, LANES)
    lane = lax.broadcasted_iota(jnp.int32, shape, 2)
    rowi = lax.broadcasted_iota(jnp.int32, shape, 1)
    fwd = (fst_ref[...], twr_ref[...], twi_ref[...], wbig_ref[...])
    inv = (gst_ref[...], twr_ref[...], twi_ref[...], wbigc_ref[...])

    def chain(lo):
        ch = pl.ds(lo, half)

        def short_conv(x_ref, p_ref):
            x = x_ref[ch]
            p = p_ref[ch]
            return (p[:, 0:1, :] * _tshift(x, True, lane, rowi) + p[:, 1:2, :] * x
                    + p[:, 2:3, :] * _tshift(x, False, lane, rowi) + p[:, 3:4, :])

        x1 = short_conv(x1_ref, p1_ref)
        x2 = short_conv(x2_ref, p2_ref)
        z = short_conv(v_ref, pv_ref)
        sk = sk_ref[ch]
        for o, (gate, hr_ref, hi_ref) in enumerate(((x1, h1r_ref, h1i_ref), (x2, h2r_ref, h2i_ref))):
            zr, zi = yield from _fft_fwd(z, *fwd)
            hr, hi = hr_ref[ch], hi_ref[ch]
            y = yield from _fft_inv(zr * hr - zi * hi, zr * hi + zi * hr, *inv)
            z = gate * (y + sk[:, o:o + 1, :] * z)
        o_ref[ch] = z

    _interleave([chain(i * half) for i in range(HY_CHAINS)])


def _hy_conv(projt4, pp, ps, hr, hi, consts, cb=HY_CB):
    fst, gst, twr, twi, wbig, wbigc = consts
    _, bsz, r, _ = projt4.shape
    n1 = twr.shape[0]
    nj = HY_DIM // cb
    full = lambda a: pl.BlockSpec(a.shape, lambda j, b: (0,) * a.ndim)

    def xs(off):
        return pl.BlockSpec((cb, None, r, LANES), lambda j, b: (off * nj + j, b, 0, 0))

    def ps_(off):
        return pl.BlockSpec((cb, SUBLANES, LANES), lambda j, b: (off * nj + j, 0, 0))

    def hs(off):
        return pl.BlockSpec((cb, n1, LANES), lambda j, b: (off * nj + j, 0, 0))

    return pl.pallas_call(
        _hy_conv_kernel, grid=(nj, bsz),
        in_specs=[xs(0), xs(1), xs(2), ps_(0), ps_(1), ps_(2), ps_(0), hs(0), hs(0), hs(1), hs(1),
                  full(fst), full(gst), full(twr), full(twi), full(wbig), full(wbigc)],
        out_specs=pl.BlockSpec((cb, None, r, LANES), lambda j, b: (j, b, 0, 0)),
        out_shape=jax.ShapeDtypeStruct((HY_DIM, bsz, r, LANES), F32),
        compiler_params=_params("parallel", "arbitrary"), name="hy_conv",
    )(projt4, projt4, projt4, pp, pp, pp, ps, hr, hi, hr, hi, fst, gst, twr, twi, wbig, wbigc)


def _fft_consts(t_len):
    n = 2 * t_len
    n1 = n // LANES
    r = t_len // LANES
    k1 = np.arange(n1)[:, None]
    th1 = 2.0 * np.pi * k1 * np.arange(r)[None, :] / n1
    fst = np.concatenate([np.cos(th1), -np.sin(th1)], axis=0)
    gst = np.concatenate([np.cos(th1).T, -np.sin(th1).T], axis=1) / n
    tht = 2.0 * np.pi * k1 * np.arange(LANES)[None, :] / n
    twr, twi = np.cos(tht), -np.sin(tht)
    th2 = 2.0 * np.pi * np.arange(LANES)[:, None] * np.arange(LANES)[None, :] / LANES
    cr, ci = np.cos(th2), -np.sin(th2)
    wbig = np.block([[cr, ci], [-ci, cr]])
    wbigc = np.block([[cr, -ci], [ci, cr]])
    return tuple(jnp.asarray(a, F32) for a in (fst, gst, twr, twi, wbig, wbigc))


def _hy_positions(t_len):
    pos = np.arange(t_len, dtype=np.float32)[:, None]
    t = pos / np.float32(max(t_len - 1, 1))
    bands = np.linspace(1e-4, HY_BANDS - 1, HY_BANDS, dtype=np.float32)[None, :]
    ang = bands * np.float32(2.0 * math.pi / t_len) * pos
    z = np.concatenate([t, np.cos(ang), -np.sin(ang)], -1).astype(np.float32)
    return jnp.asarray(np.pad(z, ((0, 0), (0, HY_FILT_HID - HY_EMB))))


def _hy_deltas():
    max_decay = math.log(HY_DECAY_TARGET) / HY_FAST_DECAY_PCT
    min_decay = math.log(HY_DECAY_TARGET) / HY_SLOW_DECAY_PCT
    d = np.abs(np.linspace(min_decay, max_decay, HY_DIM, dtype=np.float32))
    return jnp.asarray(np.tile(d, 4)[:, None])


def _lane_rep(a, rows=SUBLANES):
    ch, k = a.shape
    a = jnp.pad(a, ((0, 0), (0, rows - k)))
    return jnp.broadcast_to(a[:, :, None], (ch, rows, LANES))


def _out_even_kernel(x_ref, yat_ref, of_ref, ob_ref, zg_ref, nw_ref, wa_ref, wb_ref, g_ref, b_ref,
                     o_ref):
    mix = _mm(yat_ref[...], wa_ref[...], _TN)
    nw = nw_ref[...]
    for h in range(GDN_HEADS):
        sl = slice(h * GDN_DK, (h + 1) * GDN_DK)
        o = of_ref[:, sl] + ob_ref[:, sl]
        y = o * lax.rsqrt(jnp.mean(o * o, -1, keepdims=True) + RMS_EPS) * nw * _silu(zg_ref[:, sl])
        mix = mix + _mm(y, wb_ref[sl, :])
    o_ref[...] = _layer_norm(DEEPNORM_ALPHA * x_ref[...] + mix, g_ref[...], b_ref[...])


def _out_even(x2, yat, of, ob, projm, norm_w, wa, wb, g, b, tm=256):
    n, d = x2.shape
    row = lambda w: pl.BlockSpec((tm, w), lambda i: (i, 0))
    full = lambda a: pl.BlockSpec(a.shape, lambda i: (0,) * a.ndim)
    return pl.pallas_call(
        _out_even_kernel, grid=(n // tm,),
        in_specs=[row(d), pl.BlockSpec((HY_DIM, tm), lambda i: (0, i)), row(512), row(512),
                  pl.BlockSpec((tm, 512), lambda i: (i, 3)),
                  full(norm_w), full(wa), full(wb), full(g), full(b)],
        out_specs=row(d), out_shape=jax.ShapeDtypeStruct((n, d), F32),
        compiler_params=_params("parallel"), name="out_even",
    )(x2, yat, of, ob, projm, norm_w, wa, wb, g, b)


def _even_layer(x2, bsz, t_len, w_in, hy_conv_w, hy_conv_b, f_w1, f_b1, f_w2, f_b2, f_w3, f_b3,
                f_w4, f_freq, hy_skip, gdn_conv_w, gdn_a_log, gdn_dt_bias, gdn_norm_w, w_out,
                ln_g, ln_b):
    d = x2.shape[1]
    hyw = 3 * HY_DIM
    wm = jnp.concatenate([w_in[:, hyw:], jnp.zeros((d, LANES - 12), F32)], axis=1).astype(BF16)
    wt = w_in[:, :hyw].T.astype(BF16)
    projm, projt = _project(x2, wm, wt)
    consts = _fft_consts(t_len)
    w1p = jnp.pad(f_w1, ((0, HY_FILT_HID - HY_EMB), (0, 0)))
    filt = _hy_filters(_hy_positions(t_len), w1p, f_b1[None], f_w2, f_b2[None], f_w3, f_b3[None],
                       f_freq[None], f_w4.T, _hy_deltas(), t_len)
    r = t_len // LANES
    hr, hi = _hy_spectra(filt.reshape(4 * HY_DIM, r, LANES), consts)
    pp = _lane_rep(jnp.concatenate([hy_conv_w.T, hy_conv_b[:, None]], axis=1))
    ps = _lane_rep(hy_skip.T)
    yat = _hy_conv(projt.reshape(hyw, bsz, r, LANES), pp, ps, hr, hi, consts)
    yat = yat.reshape(HY_DIM, bsz * t_len)
    alog = jnp.zeros((1, LANES), F32).at[0, :8].set(gdn_a_log.reshape(-1))
    dtb = jnp.zeros((1, LANES), F32).at[0, :8].set(gdn_dt_bias.reshape(-1))
    qn, kn, vn, gates = _gdn_pre(projm, gdn_conv_w, alog, dtb, bsz, t_len)
    of, ob = _gdn_scan(qn, kn, vn, gates, bsz, t_len)
    return _out_even(x2, yat, of, ob, projm, gdn_norm_w[None], w_out[:HY_DIM].astype(BF16),
                     w_out[HY_DIM:].astype(BF16), ln_g[None], ln_b[None])


def _hg_scan_kernel(qf_ref, ff_ref, vf_ref, qb_ref, fb_ref, vb_ref, low_ref, of_ref, ob_ref,
                    state, q_s, k_s, g_s, *, layer):
    n = pl.program_id(1)

    @pl.when(n == 0)
    def _():
        state[...] = jnp.zeros_like(state)

    tb = qf_ref.shape[0]
    c = HG_CHUNK
    nsub = tb // c
    low = low_ref[...]
    e = jnp.exp(low - jnp.max(low, axis=0, keepdims=True))
    p = e / jnp.sum(e, axis=0, keepdims=True)
    lb = jnp.sum(p[1:layer + 1], axis=0, keepdims=True)
    rb = lax.broadcasted_iota(jnp.int32, (tb, tb), 0)
    cb = lax.broadcasted_iota(jnp.int32, (tb, tb), 1)
    same = (rb // c) == (cb // c)
    hb = c // 2
    rowh = lax.broadcasted_iota(jnp.int32, (hb, 1), 0)
    dirs = ((qf_ref, ff_ref, vf_ref, of_ref), (qb_ref, fb_ref, vb_ref, ob_ref))
    for d, (q_ref, f_ref, v_ref, o_ref) in enumerate(dirs):
        fg = lb + (1.0 - lb) * _sigmoid(f_ref[...])
        tri = (same & ((cb <= rb) if d == 0 else (cb >= rb))).astype(BF16)
        g_s[...] = _sel_mm(tri, jnp.log(fg))
        k_s[...] = 1.0 - fg
        q_s[...] = _silu(q_ref[...])

        def body(s, carry):
            def sub_rows(u):
                idx = s * HG_SUBS + u
                sub = idx if d == 0 else nsub - 1 - idx
                return pl.ds(pl.multiple_of(sub * c, c), c)

            def pairwise(qb, kb, gb, vb):
                acc = None
                for j in range(hb):
                    dec = jnp.exp(gb - gb[j:j + 1, :])
                    a_col = jnp.sum(qb * kb[j:j + 1, :] * dec, axis=1, keepdims=True)
                    a_col = jnp.where((rowh >= j) if d == 0 else (rowh <= j), a_col, 0.0)
                    term = a_col * vb[j:j + 1, :]
                    acc = term if acc is None else acc + term
                return acc

            def chain(h):
                sl = slice(h * HG_DK, (h + 1) * HG_DK)
                st = state[d, h]
                pending = []
                for u in range(HG_SUBS):
                    rows = sub_rows(u)
                    qc, kc, gc, vc = q_s[rows, sl], k_s[rows, sl], g_s[rows, sl], v_ref[rows, sl]
                    g_last = gc[c - 1:c, :] if d == 0 else gc[0:1, :]
                    inter = _mm(qc * jnp.exp(gc), st, _NT)
                    upd = _mm(vc, kc * jnp.exp(g_last - gc), _TN)
                    yield
                    st = st * jnp.exp(g_last) + upd
                    pending.append((rows, qc, kc, gc, vc, inter))
                state[d, h] = st
                lo, hi = slice(0, hb), slice(hb, c)
                src, dst, bnd = (lo, hi, hb - 1) if d == 0 else (hi, lo, hb)
                for rows, qc, kc, gc, vc, inter in pending:
                    g_bnd = gc[bnd:bnd + 1, :]
                    q_x = qc[dst] * jnp.exp(gc[dst] - g_bnd)
                    k_y = kc[src] * jnp.exp(g_bnd - gc[src])
                    v_src = vc[src]
                    cross = None
                    for j in range(hb):
                        term = jnp.sum(q_x * k_y[j:j + 1, :], axis=1, keepdims=True) * v_src[j:j + 1, :]
                        cross = term if cross is None else cross + term
                    o_lo = pairwise(qc[lo], kc[lo], gc[lo], vc[lo])
                    o_hi = pairwise(qc[hi], kc[hi], gc[hi], vc[hi])
                    if d == 0:
                        o_hi = o_hi + cross
                    else:
                        o_lo = o_lo + cross
                    o_ref[rows, sl] = inter + jnp.concatenate([o_lo, o_hi], axis=0)

            _interleave([chain(h) for h in range(HG_HEADS)])
            return carry

        lax.fori_loop(0, nsub // HG_SUBS, body, 0)


def _hg_scan(proj, hg_lower, layer, bsz, t_len, tb=128):
    n = proj.shape[0]
    nb = t_len // tb

    def spec(colb, rev):
        if rev:
            return pl.BlockSpec((tb, 512), lambda b, i: (b * nb + nb - 1 - i, colb))
        return pl.BlockSpec((tb, 512), lambda b, i: (b * nb + i, colb))

    return pl.pallas_call(
        functools.partial(_hg_scan_kernel, layer=layer), grid=(bsz, nb),
        in_specs=[spec(0, False), spec(1, False), spec(3, False),
                  spec(0, True), spec(2, True), spec(3, True),
                  pl.BlockSpec(hg_lower.shape, lambda b, i: (0, 0))],
        out_specs=[spec(0, False), spec(0, True)],
        out_shape=[jax.ShapeDtypeStruct((n, 512), F32)] * 2,
        scratch_shapes=[pltpu.VMEM((2, HG_HEADS, HG_DK, HG_DK), F32)]
        + [pltpu.VMEM((tb, 512), F32)] * 3,
        compiler_params=_params("parallel", "arbitrary"), name="hg_scan",
    )(proj, proj, proj, proj, proj, proj, hg_lower)


def _rw_pre_kernel(rc, rp, rn, kc, kp, kn, vc, vp, vn, lc, lp, ln, mu_ref, w0_ref, w2_ref,
                   a0_ref, a2_ref, g2_ref, kk_ref, ka_ref, bd_ref,
                   r_o, k_o, v_o, kk_o, kka_o, lwf_o, lwb_o, g_o, scr):
    i = pl.program_id(1)
    first = i == 0
    last = i == pl.num_programs(1) - 1
    tb = rc.shape[0]

    def shifted(cur_ref, prev_ref, next_ref, col0):
        w = cur_ref.shape[1]
        cur = cur_ref[...]
        scr[pl.ds(SUBLANES, tb), pl.ds(0, w)] = cur
        scr[pl.ds(0, SUBLANES), pl.ds(0, w)] = jnp.where(first, 0.0, prev_ref[...])
        scr[pl.ds(SUBLANES + tb, SUBLANES), pl.ds(0, w)] = jnp.where(last, 0.0, next_ref[...])
        prev = scr[pl.ds(SUBLANES - 1, tb), pl.ds(0, w)]
        nxt = scr[pl.ds(SUBLANES + 1, tb), pl.ds(0, w)]
        mu = mu_ref[:, col0:col0 + w]
        return cur + mu[0:1] * (prev - cur) + mu[1:2] * (nxt - cur)

    r = shifted(rc, rp, rn, 0)
    k = shifted(kc, kp, kn, RW_DIM)
    v = shifted(vc, vp, vn, 2 * RW_DIM)
    lo = shifted(lc, lp, ln, 3 * RW_DIM)
    lo_wa, lo_g = lo[:, :LANES], lo[:, LANES:]
    th = jnp.tanh(lo_wa)
    for d, o_ref in enumerate((lwf_o, lwb_o)):
        w_raw = w0_ref[d:d + 1, :] + _mm(th, w2_ref[d])
        o_ref[...] = -jnp.exp(-_softplus(-w_raw) - 0.5)
    a = _sigmoid(a0_ref[...] + _mm(lo_wa, a2_ref[...]))
    g_o[...] = _mm(_sigmoid(lo_g), g2_ref[...])
    kx = k * kk_ref[...]
    kk = kx * lax.rsqrt(_sel_mm(bd_ref[...], kx * kx, sel_first=False) + L2_EPS)
    r_o[...] = r
    v_o[...] = v
    k_o[...] = k * (1.0 + (a - 1.0) * ka_ref[...])
    kk_o[...] = kk
    kka_o[...] = kk * a


def _rw_pre(proj, mu, w0, w2p, a0, a2p, g2, k_k, k_a, bd, bsz, t_len, tb=256):
    n = proj.shape[0]
    nb = t_len // tb
    base = 2560
    specs = []
    for off, w in ((0, 512), (512, 512), (1024, 512), (1536, 256)):
        specs += _halo_specs(w, (base + off) // w, tb, t_len, bsz)
    full = lambda a: pl.BlockSpec(a.shape, lambda b, i: (0,) * a.ndim)
    out_spec = pl.BlockSpec((tb, RW_DIM), lambda b, i: (b * nb + i, 0))
    consts = (mu, w0, w2p, a0, a2p, g2, k_k, k_a, bd)
    return pl.pallas_call(
        _rw_pre_kernel, grid=(bsz, nb),
        in_specs=specs + [full(a) for a in consts],
        out_specs=[out_spec] * 8,
        out_shape=[jax.ShapeDtypeStruct((n, RW_DIM), F32)] * 8,
        scratch_shapes=[pltpu.VMEM((tb + 2 * SUBLANES, RW_DIM), F32)],
        compiler_params=_params("parallel", "parallel"), name="rw_pre",
    )(*([proj] * 12), *consts)


def _rw_scan_kernel(rf, kf, vf, bf, af, wf, rb, kb, vb, bb, ab, wb, yf, yb, state):
    n = pl.program_id(1)

    @pl.when(n == 0)
    def _():
        state[...] = jnp.zeros_like(state)

    c = RW_CHUNK
    gw = RW_GROUP * RW_HD
    gc = RW_GROUP * c
    row = lax.broadcasted_iota(jnp.int32, (c, c), 0)
    col = lax.broadcasted_iota(jnp.int32, (c, c), 1)
    rr = lax.broadcasted_iota(jnp.int32, (gc, gc), 0)
    cc = lax.broadcasted_iota(jnp.int32, (gc, gc), 1)
    same = (rr // c) == (cc // c)
    head_lanes = (lax.broadcasted_iota(jnp.int32, (gc, gw), 0) // c
                  == lax.broadcasted_iota(jnp.int32, (gc, gw), 1) // RW_HD)
    rs = lax.broadcasted_iota(jnp.int32, (gw, gw), 0)
    cs = lax.broadcasted_iota(jnp.int32, (gw, gw), 1)
    head_blocks = ((rs // RW_HD) == (cs // RW_HD)).astype(F32)
    row_cat = lax.broadcasted_iota(jnp.int32, (c, gc), 0)
    col_cat = lax.broadcasted_iota(jnp.int32, (c, gc), 1) % c
    eye_cat = (col_cat == row_cat).astype(F32)

    def stack(x):
        return jnp.where(head_lanes, jnp.concatenate([x] * RW_GROUP, axis=0), 0.0)

    dirs = ((rf, kf, vf, bf, af, wf, yf), (rb, kb, vb, bb, ab, wb, yb))
    def chain(ib, d, p):
        r_ref, k_ref, v_ref, b_ref, a_ref, w_ref, y_ref = dirs[d]
        sl = slice(p * gw, (p + 1) * gw)
        incl = (col_cat <= row_cat) if d == 0 else (col_cat >= row_cat)
        strict = (col_cat < row_cat) if d == 0 else (col_cat > row_cat)
        lw = w_ref[ib, :, sl]
        tri = ((col <= row) if d == 0 else (col >= row)).astype(BF16)
        g_in = _sel_mm(tri, lw)
        yield
        g_ex = g_in - lw
        g_end = g_in[c - 1:c, :] if d == 0 else g_in[0:1, :]
        e_neg = jnp.exp(-g_in)
        e_end = jnp.exp(g_end - g_in)
        k, a, v = k_ref[ib, :, sl], a_ref[ib, :, sl], v_ref[ib, :, sl]
        lhs = jnp.concatenate([r_ref[ib, :, sl] * jnp.exp(g_in), b_ref[ib, :, sl] * jnp.exp(g_ex)],
                              axis=0)
        ak = _mm(lhs, stack(k * e_neg), _NT)
        yield
        aa = _mm(lhs, stack(a * e_neg), _NT)
        yield
        a_rk = jnp.where(incl, ak[:c], 0.0)
        a_ra = jnp.where(incl, aa[:c], 0.0)
        a_bk = jnp.where(strict, ak[c:], 0.0)
        a_ba = jnp.where(strict, aa[c:], 0.0)
        t_inv = yield from _unit_tri_inv_cat(a_ba, eye_cat, same, c)
        yield
        m = state[ib, d, p]
        xm = _mm(lhs, m, _NT)
        yield
        av = _mm(jnp.concatenate([a_rk, a_bk], axis=0), stack(v))
        yield
        u = _mm(t_inv, stack(xm[c:] + av[c:]))
        yield
        y_ref[ib, :, sl] = xm[:c] + av[:c] - _mm(a_ra, stack(u))
        yield
        upd = _mm(jnp.concatenate([v, u], axis=0),
                  jnp.concatenate([k * e_end, -(a * e_end)], axis=0), _TN)
        state[ib, d, p] = m * jnp.exp(g_end) + upd * head_blocks

    _interleave([chain(ib, d, p) for ib in range(rf.shape[0]) for d in range(2)
                 for p in range(RW_DIM // gw)])


def _rw_scan(r, k, v, kk, kka, lwf, lwb, bsz, t_len, bb=SCAN_BATCH):
    n = r.shape[0]
    c = RW_CHUNK
    gw = RW_GROUP * RW_HD
    nc = t_len // c
    bb = math.gcd(bb, bsz)
    r, k, v, kk, kka, lwf, lwb =(a.reshape(bsz, t_len, RW_DIM) for a in (r, k, v, kk, kka, lwf, lwb))
    fwd = pl.BlockSpec((bb, c, RW_DIM), lambda g, i: (g, i, 0))
    bwd = pl.BlockSpec((bb, c, RW_DIM), lambda g, i: (g, nc - 1 - i, 0))
    yf, yb = pl.pallas_call(
        _rw_scan_kernel, grid=(bsz // bb, nc),
        in_specs=[fwd] * 6 + [bwd] * 6, out_specs=[fwd, bwd],
        out_shape=[jax.ShapeDtypeStruct((bsz, t_len, RW_DIM), F32)] * 2,
        scratch_shapes=[pltpu.VMEM((bb, 2, RW_DIM // gw, gw, gw), F32)],
        compiler_params=_params("parallel", "arbitrary"), name="rw_scan",
    )(r, k, v, kk, kka, lwf, r, k, v, kk, kka, lwb)
    return yf.reshape(n, RW_DIM), yb.reshape(n, RW_DIM)


def _out_odd_kernel(x_ref, hf_ref, hb_ref, hg_ref, hnw_ref, yf_ref, yb_ref, r_ref, k_ref, v_ref,
                    gate_ref, rk_ref, lw_ref, lb_ref, bd_ref, wc_ref, wd_ref, g_ref, b_ref, o_ref):
    nw = hnw_ref[...]
    mix = None
    for h in range(HG_HEADS):
        sl = slice(h * HG_DK, (h + 1) * HG_DK)
        o = hf_ref[:, sl] + hb_ref[:, sl]
        y = o * lax.rsqrt(jnp.mean(o * o, -1, keepdims=True) + RMS_EPS) * nw * _silu(hg_ref[:, sl])
        part = _mm(y, wc_ref[sl, :])
        mix = part if mix is None else mix + part
    bd = bd_ref[...]
    y = yf_ref[...] + yb_ref[...]
    yc = y - _sel_mm(bd, y, sel_first=False) * (1.0 / RW_HD)
    var = _sel_mm(bd, yc * yc, sel_first=False) * (1.0 / RW_HD)
    yn = yc * lax.rsqrt(var + RW_GN_EPS) * lw_ref[...] + lb_ref[...]
    bonus = _sel_mm(bd, r_ref[...] * k_ref[...] * rk_ref[...], sel_first=False) * v_ref[...]
    mix = mix + _mm((yn + bonus) * gate_ref[...], wd_ref[...])
    o_ref[...] = _layer_norm(DEEPNORM_ALPHA * x_ref[...] + mix, g_ref[...], b_ref[...])


def _out_odd(x2, hf, hb, proj, hnw, yf, yb, r, k, v, gate, r_k, lnx_w, lnx_b, bd, wc, wd, g, b,
             tm=256):
    n, d = x2.shape
    row = lambda w: pl.BlockSpec((tm, w), lambda i: (i, 0))
    full = lambda a: pl.BlockSpec(a.shape, lambda i: (0,) * a.ndim)
    consts = (r_k, lnx_w, lnx_b, bd, wc, wd, g, b)
    return pl.pallas_call(
        _out_odd_kernel, grid=(n // tm,),
        in_specs=[row(d), row(512), row(512), pl.BlockSpec((tm, 512), lambda i: (i, 4)), full(hnw)]
        + [row(512)] * 6 + [full(a) for a in consts],
        out_specs=row(d), out_shape=jax.ShapeDtypeStruct((n, d), F32),
        compiler_params=_params("parallel"), name="out_odd",
    )(x2, hf, hb, proj, hnw, yf, yb, r, k, v, gate, *consts)


def _odd_layer(x2, bsz, t_len, layer, w_in, hg_lower, hg_norm_w, rw_mu, rw_w0, rw_w2, rw_a0, rw_a2,
               rw_g2, rw_k_k, rw_k_a, rw_r_k, rw_lnx_w, rw_lnx_b, w_out, ln_g, ln_b):
    proj = _project(x2, w_in.astype(BF16))
    hf, hb = _hg_scan(proj, hg_lower, layer, bsz, t_len)
    idx = np.arange(RW_DIM) // RW_HD
    bd = jnp.asarray(idx[:, None] == idx[None, :], BF16)
    lora = rw_w2.shape[1]
    w2p = jnp.pad(rw_w2, ((0, 0), (0, LANES - lora), (0, 0)))
    a2p = jnp.pad(rw_a2, ((LANES - rw_a2.shape[0], 0), (0, 0)))
    r, k, v, kk, kka, lwf, lwb, gate = _rw_pre(proj, rw_mu, rw_w0, w2p, rw_a0[None], a2p, rw_g2,
                                               rw_k_k[None], rw_k_a[None], bd, bsz, t_len)
    yf, yb = _rw_scan(r, k, v, kk, kka, lwf, lwb, bsz, t_len)
    return _out_odd(x2, hf, hb, proj, hg_norm_w[None], yf, yb, r, k, v, gate,
                    rw_r_k.reshape(1, RW_DIM), rw_lnx_w[None], rw_lnx_b[None], bd,
                    w_out[:512].astype(BF16), w_out[512:].astype(BF16), ln_g[None], ln_b[None])


def kernel(x, ev_w_in, hy_conv_w, hy_conv_b, hy_filt_w1, hy_filt_b1, hy_filt_w2, hy_filt_b2, hy_filt_w3, hy_filt_b3, hy_filt_w4, hy_filt_freq, hy_skip, gdn_conv_w, gdn_a_log, gdn_dt_bias, gdn_norm_w, ev_w_out, od_w_in, hg_lower, hg_norm_w, rw_mu, rw_w0, rw_w2, rw_a0, rw_a2, rw_g2, rw_k_k, rw_k_a, rw_r_k, rw_lnx_w, rw_lnx_b, od_w_out, ln_g, ln_b, mlp_w1, mlp_w2):
    bsz, t_len, d = x.shape
    x2 = x.reshape(bsz * t_len, d)
    for layer in range(DEPTH):
        i = layer // 2
        if layer % 2 == 0:
            x2 = _even_layer(x2, bsz, t_len, ev_w_in[i], hy_conv_w[i], hy_conv_b[i], hy_filt_w1[i],
                             hy_filt_b1[i], hy_filt_w2[i], hy_filt_b2[i], hy_filt_w3[i],
                             hy_filt_b3[i], hy_filt_w4[i], hy_filt_freq[i], hy_skip[i],
                             gdn_conv_w[i], gdn_a_log[i], gdn_dt_bias[i], gdn_norm_w[i],
                             ev_w_out[i], ln_g[layer, 0], ln_b[layer, 0])
        else:
            x2 = _odd_layer(x2, bsz, t_len, layer, od_w_in[i], hg_lower, hg_norm_w[i], rw_mu[i],
                            rw_w0[i], rw_w2[i], rw_a0[i], rw_a2[i], rw_g2[i], rw_k_k[i], rw_k_a[i],
                            rw_r_k[i], rw_lnx_w[i], rw_lnx_b[i], od_w_out[i], ln_g[layer, 0],
                            ln_b[layer, 0])
        x2 = _mlp_block(x2, mlp_w1[layer].astype(BF16), mlp_w2[layer].astype(BF16),
                        ln_g[layer, 1][None], ln_b[layer, 1][None])
    return x2.reshape(bsz, t_len, d)
```

```python
import functools
import math

import numpy as np
import jax
import jax.numpy as jnp
from jax import lax
from jax.experimental import pallas as pl
from jax.experimental.pallas import tpu as pltpu

F32 = jnp.float32
BF16 = jnp.bfloat16

D_MODEL = 1024
DEPTH = 2
D_FF = 4 * D_MODEL
DEEPNORM_ALPHA = (2.0 * DEPTH) ** 0.25
LN_EPS = 1e-5
RMS_EPS = 1e-6
L2_EPS = 1e-6

HY_DIM = 512
HY_EMB = 33
HY_BANDS = 16
HY_FILT_HID = 64
HY_DECAY_TARGET = 1e-2
HY_FAST_DECAY_PCT = 0.3
HY_SLOW_DECAY_PCT = 1.5

GDN_HEADS = 4
GDN_DK = 128
GDN_CHUNK = 64

HG_HEADS = 4
HG_DK = 128
HG_CHUNK = 16
HG_SUBS = 4

RW_HEADS = 8
RW_HD = 64
RW_DIM = 512
RW_CHUNK = 64
RW_GROUP = 4
RW_GN_EPS = 64e-5

SCAN_BATCH = 4

LANES = 128
SUBLANES = 8
VMEM_LIMIT = 56 * 1024 * 1024

_NN = (((1,), (0,)), ((), ()))
_NT = (((1,), (1,)), ((), ()))
_TN = (((0,), (0,)), ((), ()))


def _mm(a, b, dims=_NN, exact=False):
    if exact:
        return lax.dot_general(a, b, dims, precision=lax.Precision.HIGHEST,
                               preferred_element_type=F32)
    return lax.dot_general(a.astype(BF16), b.astype(BF16), dims, preferred_element_type=F32)


def _params(*sem):
    return pltpu.CompilerParams(dimension_semantics=sem, vmem_limit_bytes=VMEM_LIMIT)


def _sigmoid(x):
    return 1.0 / (1.0 + jnp.exp(-x))


def _silu(x):
    return x * _sigmoid(x)


def _softplus(x):
    return jnp.maximum(x, 0.0) + jnp.log(1.0 + jnp.exp(-jnp.abs(x)))


def _layer_norm(v, g, b):
    mean = jnp.mean(v, axis=-1, keepdims=True)
    vc = v - mean
    var = jnp.mean(vc * vc, axis=-1, keepdims=True)
    return vc * lax.rsqrt(var + LN_EPS) * g + b


def _col(x, idx, lane):
    return jnp.sum(jnp.where(lane == idx, x, 0.0), axis=1, keepdims=True)


def _split2(a):
    hi = a.astype(BF16)
    return hi, (a - hi.astype(F32)).astype(BF16)


def _mm3(a, b, dims=_NN):
    ah, al = _split2(a)
    bh, bl = _split2(b)
    dot = functools.partial(lax.dot_general, dimension_numbers=dims, preferred_element_type=F32)
    return dot(ah, bh) + dot(ah, bl) + dot(al, bh)


def _sel_mm(sel, x, dims=_NN, sel_first=True):
    hi = x.astype(BF16)
    r1 = x - hi.astype(F32)
    mid = r1.astype(BF16)
    lo = (r1 - mid.astype(F32)).astype(BF16)
    dot = functools.partial(lax.dot_general, dimension_numbers=dims, preferred_element_type=F32)
    if sel_first:
        return dot(sel, hi) + dot(sel, mid) + dot(sel, lo)
    return dot(hi, sel) + dot(mid, sel) + dot(lo, sel)


def _expand_bd(x_cat, same):
    g = x_cat.shape[1] // x_cat.shape[0]
    return jnp.where(same, jnp.concatenate([x_cat] * g, axis=0), 0.0)


def _unit_tri_inv_cat(a_cat, eye_cat, same, depth):
    c = a_cat.shape[0]
    p = -a_cat
    t = eye_cat + p
    p = _mm(p, _expand_bd(p, same))
    yield
    k = 4
    while k < depth:
        both = _mm(jnp.concatenate([p, t], axis=0), _expand_bd(p, same))
        yield
        p, t = both[:c], t + both[c:]
        k *= 2
    return t + _mm(t, _expand_bd(p, same))


def _interleave(chains):
    active = list(chains)
    while active:
        still = []
        for g in active:
            try:
                next(g)
                still.append(g)
            except StopIteration:
                pass
        active = still


def _proj_kernel(x_ref, w_ref, o_ref):
    o_ref[...] = _mm(x_ref[...], w_ref[...])


def _proj_t_kernel(x_ref, w_ref, wt_ref, o_ref, ot_ref):
    xb = x_ref[...].astype(BF16)
    o_ref[...] = _mm(xb, w_ref[...])
    ot_ref[...] = _mm(wt_ref[...], xb, _NT).astype(ot_ref.dtype)


def _project(x2, w, wt=None, tm=512):
    n, d = x2.shape
    m = w.shape[1]
    grid = (n // tm,)
    once = pl.Buffered(1)
    x_spec = pl.BlockSpec((tm, d), lambda i: (i, 0))
    w_spec = pl.BlockSpec((d, m), lambda i: (0, 0), pipeline_mode=once)
    o_spec = pl.BlockSpec((tm, m), lambda i: (i, 0))
    if wt is None:
        return pl.pallas_call(
            _proj_kernel, grid=grid, in_specs=[x_spec, w_spec], out_specs=o_spec,
            out_shape=jax.ShapeDtypeStruct((n, m), F32), compiler_params=_params("parallel"),
            name="proj")(x2, w)
    mt = wt.shape[0]
    return pl.pallas_call(
        _proj_t_kernel, grid=grid,
        in_specs=[x_spec, w_spec, pl.BlockSpec((mt, d), lambda i: (0, 0), pipeline_mode=once)],
        out_specs=[o_spec, pl.BlockSpec((mt, tm), lambda i: (0, i))],
        out_shape=[jax.ShapeDtypeStruct((n, m), F32), jax.ShapeDtypeStruct((mt, n), BF16)],
        compiler_params=_params("parallel"), name="proj_t")(x2, w, wt)


def _mlp_kernel(x_ref, w1_ref, w2_ref, g_ref, b_ref, o_ref):
    x = x_ref[...]
    h = jnp.maximum(_mm(x, w1_ref[...]), 0.0)
    mix = _mm(h * h, w2_ref[...])
    o_ref[...] = _layer_norm(DEEPNORM_ALPHA * x + mix, g_ref[...], b_ref[...])


def _mlp_block(x2, w1, w2, g, b, tm=512):
    n, d = x2.shape
    f = w1.shape[1]
    tm = min(tm, n)
    once = pl.Buffered(1)
    return pl.pallas_call(
        _mlp_kernel, grid=(n // tm,),
        in_specs=[pl.BlockSpec((tm, d), lambda i: (i, 0)),
                  pl.BlockSpec((d, f), lambda i: (0, 0), pipeline_mode=once),
                  pl.BlockSpec((f, d), lambda i: (0, 0), pipeline_mode=once),
                  pl.BlockSpec((1, d), lambda i: (0, 0)),
                  pl.BlockSpec((1, d), lambda i: (0, 0))],
        out_specs=pl.BlockSpec((tm, d), lambda i: (i, 0)),
        out_shape=jax.ShapeDtypeStruct((n, d), F32),
        compiler_params=_params("parallel"), name="mlp")(x2, w1, w2, g, b)


def _fill_halo(scr, cur, prev8, next8, first, last, halo):
    tb = cur.shape[0]
    scr[pl.ds(SUBLANES, tb), :] = cur
    scr[pl.ds(0, SUBLANES), :] = jnp.where(first, 0.0, prev8)
    scr[pl.ds(SUBLANES + tb, SUBLANES), :] = jnp.where(last, 0.0, next8)


def _halo_specs(width, col_block, tb, t_len, bsz):
    nb = t_len // tb
    r8 = tb // SUBLANES
    tot8 = t_len // SUBLANES

    def cur(b, i):
        return (b * nb + i, col_block)

    def prev(b, i):
        return (jnp.maximum(b * tot8 + i * r8 - 1, 0), col_block)

    def nxt(b, i):
        return (jnp.minimum(b * tot8 + (i + 1) * r8, tot8 * bsz - 1), col_block)

    return [pl.BlockSpec((tb, width), cur), pl.BlockSpec((SUBLANES, width), prev),
            pl.BlockSpec((SUBLANES, width), nxt)]


def _gdn_pre_kernel(cur_ref, prev_ref, next_ref, ab_ref, cw_ref, alog_ref, dtb_ref,
                    q_ref, k_ref, v_ref, g_ref, scr):
    i = pl.program_id(1)
    tb = cur_ref.shape[0]
    _fill_halo(scr, cur_ref[...], prev_ref[...], next_ref[...], i == 0,
               i == pl.num_programs(1) - 1, 2)
    width = cw_ref.shape[0]
    acc = None
    for j in range(width):
        term = scr[pl.ds(SUBLANES - width // 2 + j, tb), :] * cw_ref[pl.ds(j, 1), :]
        acc = term if acc is None else acc + term
    qkv = _silu(acc)
    hd = GDN_HEADS * GDN_DK
    for h in range(GDN_HEADS):
        sl = slice(h * GDN_DK, (h + 1) * GDN_DK)
        q = qkv[:, sl]
        q_ref[:, sl] = (q * lax.rsqrt(jnp.sum(q * q, -1, keepdims=True) + L2_EPS)
                        * GDN_DK ** -0.5).astype(q_ref.dtype)
        k = qkv[:, hd + h * GDN_DK: hd + (h + 1) * GDN_DK]
        k_ref[:, sl] = (k * lax.rsqrt(jnp.sum(k * k, -1, keepdims=True) + L2_EPS)).astype(k_ref.dtype)
    v_ref[...] = qkv[:, 2 * hd:].astype(v_ref.dtype)
    ab = ab_ref[...]
    lane = lax.broadcasted_iota(jnp.int32, ab.shape, 1)
    log_g = -jnp.exp(alog_ref[...]) * _softplus(ab + dtb_ref[...])
    g_ref[...] = jnp.where(lane < 2 * GDN_HEADS, log_g, _sigmoid(ab))


def _gdn_pre(proj, conv_w, a_log_row, dt_bias_row, bsz, t_len, tb=256):
    n = proj.shape[0]
    wq = 3 * GDN_HEADS * GDN_DK
    nb = t_len // tb
    specs = _halo_specs(wq, 0, tb, t_len, bsz)
    ab_col = (wq + GDN_HEADS * GDN_DK) // LANES
    out_spec = pl.BlockSpec((tb, 512), lambda b, i: (b * nb + i, 0))
    return pl.pallas_call(
        _gdn_pre_kernel, grid=(bsz, nb),
        in_specs=specs + [pl.BlockSpec((tb, LANES), lambda b, i: (b * nb + i, ab_col)),
                          pl.BlockSpec(conv_w.shape, lambda b, i: (0, 0)),
                          pl.BlockSpec((1, LANES), lambda b, i: (0, 0)),
                          pl.BlockSpec((1, LANES), lambda b, i: (0, 0))],
        out_specs=[out_spec, out_spec, out_spec,
                   pl.BlockSpec((tb, LANES), lambda b, i: (b * nb + i, 0))],
        out_shape=[jax.ShapeDtypeStruct((n, 512), BF16)] * 3 + [jax.ShapeDtypeStruct((n, LANES), F32)],
        scratch_shapes=[pltpu.VMEM((tb + 2 * SUBLANES, wq), F32)],
        compiler_params=_params("parallel", "parallel"), name="gdn_pre",
    )(proj, proj, proj, proj, conv_w, a_log_row, dt_bias_row)


def _gdn_scan_kernel(qf, kf, vf, gf, qb, kb, vb, gb, of, ob, state):
    n = pl.program_id(1)

    @pl.when(n == 0)
    def _():
        state[...] = jnp.zeros_like(state)

    c = GDN_CHUNK
    nh = GDN_HEADS
    hc = nh * c
    wide = nh * GDN_DK
    row = lax.broadcasted_iota(jnp.int32, (c, c), 0)
    col = lax.broadcasted_iota(jnp.int32, (c, c), 1)
    lane = lax.broadcasted_iota(jnp.int32, (c, LANES), 1)
    rr = lax.broadcasted_iota(jnp.int32, (hc, hc), 0)
    cc = lax.broadcasted_iota(jnp.int32, (hc, hc), 1)
    same = (rr // c) == (cc // c)
    head_lanes = (lax.broadcasted_iota(jnp.int32, (hc, wide), 0) // c
                  == lax.broadcasted_iota(jnp.int32, (hc, wide), 1) // GDN_DK)
    lane_blk = lax.broadcasted_iota(jnp.int32, (1, hc), 1) // c
    row_cat = lax.broadcasted_iota(jnp.int32, (c, hc), 0)
    lane_cat = lax.broadcasted_iota(jnp.int32, (c, hc), 1)
    col_cat = lane_cat % c
    blk_cat = lane_cat // c
    eye_cat = (col_cat == row_cat).astype(F32)

    def col_cat_of(x, first):
        out = jnp.zeros((c, hc), F32)
        for h in range(nh):
            out = jnp.where(blk_cat == h, _col(x, first + h, lane), out)
        return out

    def stack_wide(x):
        return jnp.where(head_lanes, jnp.concatenate([x] * nh, axis=0), 0.0)

    def stack_narrow(x):
        return jnp.concatenate([x[:, h * GDN_DK:(h + 1) * GDN_DK] for h in range(nh)], axis=0)

    dirs = ((qf, kf, vf, gf, of), (qb, kb, vb, gb, ob))

    def chain(bb, d):
        q_ref, k_ref, v_ref, g_ref, o_ref = dirs[d]
        incl1 = (col <= row) if d == 0 else (col >= row)
        incl = (col_cat <= row_cat) if d == 0 else (col_cat >= row_cat)
        strict = (col_cat < row_cat) if d == 0 else (col_cat > row_cat)
        gates = g_ref[bb]
        tri = incl1.astype(BF16)
        gam_all = _sel_mm(tri, gates)
        tri4 = jnp.concatenate([tri] * nh, axis=0)
        gam_rows = _sel_mm(tri4, gates.T, _NT, sel_first=False)
        yield
        gam = jnp.concatenate([_col(gam_all, d * nh + h, lane) for h in range(nh)], axis=0)
        beta = jnp.concatenate([_col(gates, 2 * nh + h, lane) for h in range(nh)], axis=0)
        gam_row = jnp.zeros((1, hc), F32)
        for h in range(nh):
            gam_row = jnp.where(lane_blk == h, gam_rows[d * nh + h:d * nh + h + 1, :], gam_row)
        last = c - 1 if d == 0 else 0
        g_last = jnp.concatenate(
            [jnp.broadcast_to(gam[h * c + last:h * c + last + 1, :], (c, 1)) for h in range(nh)], axis=0)
        g_last_s = jnp.concatenate(
            [jnp.broadcast_to(gam[h * c + last:h * c + last + 1, :], (GDN_DK, 1)) for h in range(nh)],
            axis=0)
        diff = col_cat_of(gam_all, d * nh) - gam_row
        decay = jnp.where(incl, jnp.exp(jnp.where(incl, diff, 0.0)), 0.0)
        e_gam = jnp.exp(gam)
        k_in, q_in = k_ref[bb], q_ref[bb]
        k_w = stack_wide(k_in)
        kq = _mm(jnp.concatenate([k_in, q_in], axis=0), k_w, _NT)
        yield
        a_cat = jnp.where(strict, kq[:c] * decay * col_cat_of(gates, 2 * nh), 0.0)
        attn = _expand_bd(kq[c:] * decay, same)
        t_cat = yield from _unit_tri_inv_cat(a_cat, eye_cat, same, c)
        yield
        rhs = jnp.concatenate([stack_narrow(v_ref[bb]) * beta,
                               stack_narrow(k_in) * (beta * e_gam)], axis=1)
        sol = _mm(_expand_bd(t_cat, same), rhs)
        yield
        u, w = sol[:, :GDN_DK], sol[:, GDN_DK:]
        s = state[bb, d]
        w_w = jnp.where(head_lanes, jnp.concatenate([w] * nh, axis=1), 0.0)
        ws = _mm(jnp.concatenate([w_w, stack_wide(q_in) * e_gam], axis=0), s)
        yield
        v_new = u - ws[:hc]
        out = ws[hc:] + _mm(attn, v_new)
        o_ref[bb] = jnp.concatenate([out[h * c:(h + 1) * c] for h in range(nh)], axis=1)
        k_dec = k_w * jnp.exp(g_last - gam)
        state[bb, d] = s * jnp.exp(g_last_s) + _mm(k_dec, v_new, _TN)

    _interleave([chain(bb, d) for bb in range(qf.shape[0]) for d in range(2)])


def _gdn_scan(q, k, v, gates, bsz, t_len, bb=SCAN_BATCH):
    n = q.shape[0]
    c = GDN_CHUNK
    nc = t_len // c
    bb = math.gcd(bb, bsz)
    q, k, v, gates =(a.reshape(bsz, t_len, a.shape[1]) for a in (q, k, v, gates))

    def fwd(g, i):
        return (g, i, 0)

    def bwd(g, i):
        return (g, nc - 1 - i, 0)

    def specs(idx):
        wide = pl.BlockSpec((bb, c, 512), idx)
        return [wide, wide, wide, pl.BlockSpec((bb, c, LANES), idx)]

    of, ob = pl.pallas_call(
        _gdn_scan_kernel, grid=(bsz // bb, nc),
        in_specs=specs(fwd) + specs(bwd),
        out_specs=[pl.BlockSpec((bb, c, 512), fwd), pl.BlockSpec((bb, c, 512), bwd)],
        out_shape=[jax.ShapeDtypeStruct((bsz, t_len, 512), F32)] * 2,
        scratch_shapes=[pltpu.VMEM((bb, 2, GDN_HEADS * GDN_DK, GDN_DK), F32)],
        compiler_params=_params("parallel", "arbitrary"), name="gdn_scan",
    )(q, k, v, gates, q, k, v, gates)
    return of.reshape(n, 512), ob.reshape(n, 512)


_fft_mm = _mm
HY_CB = 16
HY_CHAINS = 2


def _hy_filter_kernel(z_ref, w1_ref, b1_ref, w2_ref, b2_ref, w3_ref, b3_ref, fr_ref, w4t_ref,
                      dl_ref, o_ref, *, t_len):
    i = pl.program_id(0)
    tl = o_ref.shape[1]
    fr = fr_ref[...]
    h = jnp.sin(fr * (_mm(z_ref[...], w1_ref[...], exact=True) + b1_ref[...]))
    h = jnp.sin(fr * (_mm(h, w2_ref[...], exact=True) + b2_ref[...]))
    h = jnp.sin(fr * (_mm(h, w3_ref[...], exact=True) + b3_ref[...]))
    ht = _mm(w4t_ref[...], h, _NT, exact=True)
    pos = (lax.broadcasted_iota(jnp.int32, (1, tl), 1) + i * tl).astype(F32)
    t_norm = pos / float(max(t_len - 1, 1))
    o_ref[...] = ht * jnp.exp(-t_norm * dl_ref[...])


def _hy_filters(z, w1p, b1, w2, b2, w3, b3, freq, w4t, deltas, t_len):
    tl = min(512, t_len)
    nch = w4t.shape[0]
    full = lambda a: pl.BlockSpec(a.shape, lambda i: (0,) * a.ndim)
    return pl.pallas_call(
        functools.partial(_hy_filter_kernel, t_len=t_len), grid=(t_len // tl,),
        in_specs=[pl.BlockSpec((tl, z.shape[1]), lambda i: (i, 0))]
        + [full(a) for a in (w1p, b1, w2, b2, w3, b3, freq, w4t, deltas)],
        out_specs=pl.BlockSpec((nch, tl), lambda i: (0, i)),
        out_shape=jax.ShapeDtypeStruct((nch, t_len), F32),
        compiler_params=_params("parallel"), name="hy_filter",
    )(z, w1p, b1, w2, b2, w3, b3, freq, w4t, deltas)


def _fft_fwd(x3, fst, twr, twi, wbig):
    cb = x3.shape[0]
    n1 = twr.shape[0]
    y = _fft_mm(fst, jnp.concatenate([x3[c] for c in range(cb)], axis=1))
    yield
    yr, yi = y[:n1], y[n1:]
    twr_t = jnp.concatenate([twr] * cb, axis=1)
    twi_t = jnp.concatenate([twi] * cb, axis=1)
    tr = yr * twr_t - yi * twi_t
    ti = yr * twi_t + yi * twr_t
    cat = jnp.concatenate(
        [jnp.concatenate([tr[:, c * LANES:(c + 1) * LANES], ti[:, c * LANES:(c + 1) * LANES]], axis=1)
         for c in range(cb)], axis=0)
    z = _fft_mm(cat, wbig)
    yield
    return z[:, :LANES].reshape(cb, n1, LANES), z[:, LANES:].reshape(cb, n1, LANES)


def _fft_inv(zr, zi, gst, twr, twi, wbigc):
    cb, n1 = zr.shape[0], zr.shape[1]
    cat = jnp.concatenate([zr, zi], axis=-1).reshape(cb * n1, 2 * LANES)
    a = _fft_mm(cat, wbigc)
    yield
    ar, ai = a[:, :LANES].reshape(cb, n1, LANES), a[:, LANES:].reshape(cb, n1, LANES)
    tr = ar * twr + ai * twi
    ti = ai * twr - ar * twi
    tt = jnp.concatenate([jnp.concatenate([tr[c] for c in range(cb)], axis=1),
                          jnp.concatenate([ti[c] for c in range(cb)], axis=1)], axis=0)
    out = _fft_mm(gst, tt)
    yield
    return jnp.stack([out[:, c * LANES:(c + 1) * LANES] for c in range(cb)])


def _hy_spec_kernel(hf_ref, hb_ref, fst_ref, twr_ref, twi_ref, wbig_ref, hr_ref, hi_ref):
    args = (fst_ref[...], twr_ref[...], twi_ref[...], wbig_ref[...])
    out = {}

    def chain(name, ref):
        out[name] = yield from _fft_fwd(ref[...], *args)

    _interleave([chain("f", hf_ref), chain("b", hb_ref)])
    hr_ref[...] = out["f"][0] + out["b"][0]
    hi_ref[...] = out["f"][1] - out["b"][1]


def _hy_spectra(filt3, consts, cb=HY_CB):
    fst, gst, twr, twi, wbig, wbigc = consts
    r = filt3.shape[1]
    n1 = twr.shape[0]
    nj = HY_DIM // cb
    full = lambda a: pl.BlockSpec(a.shape, lambda o, j: (0,) * a.ndim)
    out_spec = pl.BlockSpec((cb, n1, LANES), lambda o, j: (o * nj + j, 0, 0))
    return pl.pallas_call(
        _hy_spec_kernel, grid=(2, nj),
        in_specs=[pl.BlockSpec((cb, r, LANES), lambda o, j: (2 * o * nj + j, 0, 0)),
                  pl.BlockSpec((cb, r, LANES), lambda o, j: ((2 * o + 1) * nj + j, 0, 0)),
                  full(fst), full(twr), full(twi), full(wbig)],
        out_specs=[out_spec, out_spec],
        out_shape=[jax.ShapeDtypeStruct((2 * HY_DIM, n1, LANES), F32)] * 2,
        compiler_params=_params("parallel", "parallel"), name="hy_spec",
    )(filt3, filt3, fst, twr, twi, wbig)


def _tshift(x3, back, lane, rowi):
    r = x3.shape[1]
    if back:
        rl = pltpu.roll(x3, 1, 2)
        rr = pltpu.roll(rl, 1, 1)
        edge = lane == 0
        out = jnp.where(edge, rr, rl)
        return jnp.where(edge & (rowi == 0), 0.0, out)
    rl = pltpu.roll(x3, LANES - 1, 2)
    rr = pltpu.roll(rl, r - 1, 1)
    edge = lane == LANES - 1
    out = jnp.where(edge, rr, rl)
    return jnp.where(edge & (rowi == r - 1), 0.0, out)


def _hy_conv_kernel(x1_ref, x2_ref, v_ref, p1_ref, p2_ref, pv_ref, sk_ref, h1r_ref, h1i_ref,
                    h2r_ref, h2i_ref, fst_ref, gst_ref, twr_ref, twi_ref, wbig_ref, wbigc_ref,
                    o_ref):
    cb, r, _ = x1_ref.shape
    half = cb // HY_CHAINS
    shape = (half, r, LANES)
    lane = lax.broadcasted_iota(jnp.int32, shape, 2)
    rowi = lax.broadcasted_iota(jnp.int32, shape, 1)
    fwd = (fst_ref[...], twr_ref[...], twi_ref[...], wbig_ref[...])
    inv = (gst_ref[...], twr_ref[...], twi_ref[...], wbigc_ref[...])

    def chain(lo):
        ch = pl.ds(lo, half)

        def short_conv(x_ref, p_ref):
            x = x_ref[ch].astype(F32)
            p = p_ref[ch]
            return (p[:, 0:1, :] * _tshift(x, True, lane, rowi) + p[:, 1:2, :] * x
                    + p[:, 2:3, :] * _tshift(x, False, lane, rowi) + p[:, 3:4, :])

        x1 = short_conv(x1_ref, p1_ref)
        x2 = short_conv(x2_ref, p2_ref)
        z = short_conv(v_ref, pv_ref)
        sk = sk_ref[ch]
        for o, (gate, hr_ref, hi_ref) in enumerate(((x1, h1r_ref, h1i_ref), (x2, h2r_ref, h2i_ref))):
            zr, zi = yield from _fft_fwd(z, *fwd)
            hr, hi = hr_ref[ch], hi_ref[ch]
            y = yield from _fft_inv(zr * hr - zi * hi, zr * hi + zi * hr, *inv)
            z = gate * (y + sk[:, o:o + 1, :] * z)
        o_ref[ch] = z.astype(o_ref.dtype)

    _interleave([chain(i * half) for i in range(HY_CHAINS)])


def _hy_conv(projt4, pp, ps, hr, hi, consts, cb=HY_CB):
    fst, gst, twr, twi, wbig, wbigc = consts
    _, bsz, r, _ = projt4.shape
    n1 = twr.shape[0]
    nj = HY_DIM // cb
    full = lambda a: pl.BlockSpec(a.shape, lambda j, b: (0,) * a.ndim)

    def xs(off):
        return pl.BlockSpec((cb, None, r, LANES), lambda j, b: (off * nj + j, b, 0, 0))

    def ps_(off):
        return pl.BlockSpec((cb, SUBLANES, LANES), lambda j, b: (off * nj + j, 0, 0))

    def hs(off):
        return pl.BlockSpec((cb, n1, LANES), lambda j, b: (off * nj + j, 0, 0))

    return pl.pallas_call(
        _hy_conv_kernel, grid=(nj, bsz),
        in_specs=[xs(0), xs(1), xs(2), ps_(0), ps_(1), ps_(2), ps_(0), hs(0), hs(0), hs(1), hs(1),
                  full(fst), full(gst), full(twr), full(twi), full(wbig), full(wbigc)],
        out_specs=pl.BlockSpec((cb, None, r, LANES), lambda j, b: (j, b, 0, 0)),
        out_shape=jax.ShapeDtypeStruct((HY_DIM, bsz, r, LANES), BF16),
        compiler_params=_params("parallel", "arbitrary"), name="hy_conv",
    )(projt4, projt4, projt4, pp, pp, pp, ps, hr, hi, hr, hi, fst, gst, twr, twi, wbig, wbigc)


def _fft_consts(t_len):
    n = 2 * t_len
    n1 = n // LANES
    r = t_len // LANES
    k1 = np.arange(n1)[:, None]
    th1 = 2.0 * np.pi * k1 * np.arange(r)[None, :] / n1
    fst = np.concatenate([np.cos(th1), -np.sin(th1)], axis=0)
    gst = np.concatenate([np.cos(th1).T, -np.sin(th1).T], axis=1) / n
    tht = 2.0 * np.pi * k1 * np.arange(LANES)[None, :] / n
    twr, twi = np.cos(tht), -np.sin(tht)
    th2 = 2.0 * np.pi * np.arange(LANES)[:, None] * np.arange(LANES)[None, :] / LANES
    cr, ci = np.cos(th2), -np.sin(th2)
    wbig = np.block([[cr, ci], [-ci, cr]])
    wbigc = np.block([[cr, -ci], [ci, cr]])
    return tuple(jnp.asarray(a, F32) for a in (fst, gst, twr, twi, wbig, wbigc))


def _hy_positions(t_len):
    pos = np.arange(t_len, dtype=np.float32)[:, None]
    t = pos / np.float32(max(t_len - 1, 1))
    bands = np.linspace(1e-4, HY_BANDS - 1, HY_BANDS, dtype=np.float32)[None, :]
    ang = bands * np.float32(2.0 * math.pi / t_len) * pos
    z = np.concatenate([t, np.cos(ang), -np.sin(ang)], -1).astype(np.float32)
    return jnp.asarray(np.pad(z, ((0, 0), (0, HY_FILT_HID - HY_EMB))))


def _hy_deltas():
    max_decay = math.log(HY_DECAY_TARGET) / HY_FAST_DECAY_PCT
    min_decay = math.log(HY_DECAY_TARGET) / HY_SLOW_DECAY_PCT
    d = np.abs(np.linspace(min_decay, max_decay, HY_DIM, dtype=np.float32))
    return jnp.asarray(np.tile(d, 4)[:, None])


def _lane_rep(a, rows=SUBLANES):
    ch, k = a.shape
    a = jnp.pad(a, ((0, 0), (0, rows - k)))
    return jnp.broadcast_to(a[:, :, None], (ch, rows, LANES))


def _out_even_kernel(x_ref, yat_ref, of_ref, ob_ref, zg_ref, nw_ref, wa_ref, wb_ref, g_ref, b_ref,
                     o_ref):
    mix = _mm(yat_ref[...], wa_ref[...], _TN)
    nw = nw_ref[...]
    for h in range(GDN_HEADS):
        sl = slice(h * GDN_DK, (h + 1) * GDN_DK)
        o = of_ref[:, sl] + ob_ref[:, sl]
        y = o * lax.rsqrt(jnp.mean(o * o, -1, keepdims=True) + RMS_EPS) * nw * _silu(zg_ref[:, sl])
        mix = mix + _mm(y, wb_ref[sl, :])
    o_ref[...] = _layer_norm(DEEPNORM_ALPHA * x_ref[...] + mix, g_ref[...], b_ref[...])


def _out_even(x2, yat, of, ob, projm, norm_w, wa, wb, g, b, tm=256):
    n, d = x2.shape
    row = lambda w: pl.BlockSpec((tm, w), lambda i: (i, 0))
    full = lambda a: pl.BlockSpec(a.shape, lambda i: (0,) * a.ndim)
    return pl.pallas_call(
        _out_even_kernel, grid=(n // tm,),
        in_specs=[row(d), pl.BlockSpec((HY_DIM, tm), lambda i: (0, i)), row(512), row(512),
                  pl.BlockSpec((tm, 512), lambda i: (i, 3)),
                  full(norm_w), full(wa), full(wb), full(g), full(b)],
        out_specs=row(d), out_shape=jax.ShapeDtypeStruct((n, d), F32),
        compiler_params=_params("parallel"), name="out_even",
    )(x2, yat, of, ob, projm, norm_w, wa, wb, g, b)


def _even_layer(x2, bsz, t_len, w_in, hy_conv_w, hy_conv_b, f_w1, f_b1, f_w2, f_b2, f_w3, f_b3,
                f_w4, f_freq, hy_skip, gdn_conv_w, gdn_a_log, gdn_dt_bias, gdn_norm_w, w_out,
                ln_g, ln_b):
    d = x2.shape[1]
    hyw = 3 * HY_DIM
    wm = jnp.concatenate([w_in[:, hyw:], jnp.zeros((d, LANES - 12), F32)], axis=1).astype(BF16)
    wt = w_in[:, :hyw].T.astype(BF16)
    projm, projt = _project(x2, wm, wt)
    consts = _fft_consts(t_len)
    w1p = jnp.pad(f_w1, ((0, HY_FILT_HID - HY_EMB), (0, 0)))
    filt = _hy_filters(_hy_positions(t_len), w1p, f_b1[None], f_w2, f_b2[None], f_w3, f_b3[None],
                       f_freq[None], f_w4.T, _hy_deltas(), t_len)
    r = t_len // LANES
    hr, hi = _hy_spectra(filt.reshape(4 * HY_DIM, r, LANES), consts)
    pp = _lane_rep(jnp.concatenate([hy_conv_w.T, hy_conv_b[:, None]], axis=1))
    ps = _lane_rep(hy_skip.T)
    yat = _hy_conv(projt.reshape(hyw, bsz, r, LANES), pp, ps, hr, hi, consts)
    yat = yat.reshape(HY_DIM, bsz * t_len)
    alog = jnp.zeros((1, LANES), F32).at[0, :8].set(gdn_a_log.reshape(-1))
    dtb = jnp.zeros((1, LANES), F32).at[0, :8].set(gdn_dt_bias.reshape(-1))
    qn, kn, vn, gates = _gdn_pre(projm, gdn_conv_w, alog, dtb, bsz, t_len)
    of, ob = _gdn_scan(qn, kn, vn, gates, bsz, t_len)
    return _out_even(x2, yat, of, ob, projm, gdn_norm_w[None], w_out[:HY_DIM].astype(BF16),
                     w_out[HY_DIM:].astype(BF16), ln_g[None], ln_b[None])


def _hg_scan_kernel(qf_ref, ff_ref, vf_ref, qb_ref, fb_ref, vb_ref, low_ref, of_ref, ob_ref,
                    state, q_s, k_s, g_s, *, layer):
    n = pl.program_id(1)

    @pl.when(n == 0)
    def _():
        state[...] = jnp.zeros_like(state)

    tb = qf_ref.shape[0]
    c = HG_CHUNK
    nsub = tb // c
    low = low_ref[...]
    e = jnp.exp(low - jnp.max(low, axis=0, keepdims=True))
    p = e / jnp.sum(e, axis=0, keepdims=True)
    lb = jnp.sum(p[1:layer + 1], axis=0, keepdims=True)
    rb = lax.broadcasted_iota(jnp.int32, (tb, tb), 0)
    cb = lax.broadcasted_iota(jnp.int32, (tb, tb), 1)
    same = (rb // c) == (cb // c)
    hb = c // 2
    rowh = lax.broadcasted_iota(jnp.int32, (hb, 1), 0)
    dirs = ((qf_ref, ff_ref, vf_ref, of_ref), (qb_ref, fb_ref, vb_ref, ob_ref))
    for d, (q_ref, f_ref, v_ref, o_ref) in enumerate(dirs):
        fg = lb + (1.0 - lb) * _sigmoid(f_ref[...])
        tri = (same & ((cb <= rb) if d == 0 else (cb >= rb))).astype(BF16)
        g_s[...] = _sel_mm(tri, jnp.log(fg))
        k_s[...] = 1.0 - fg
        q_s[...] = _silu(q_ref[...])

        def body(s, carry):
            def sub_rows(u):
                idx = s * HG_SUBS + u
                sub = idx if d == 0 else nsub - 1 - idx
                return pl.ds(pl.multiple_of(sub * c, c), c)

            def pairwise(qb, kb, gb, vb):
                acc = None
                for j in range(hb):
                    dec = jnp.exp(gb - gb[j:j + 1, :])
                    a_col = jnp.sum(qb * kb[j:j + 1, :] * dec, axis=1, keepdims=True)
                    a_col = jnp.where((rowh >= j) if d == 0 else (rowh <= j), a_col, 0.0)
                    term = a_col * vb[j:j + 1, :]
                    acc = term if acc is None else acc + term
                return acc

            def chain(h):
                sl = slice(h * HG_DK, (h + 1) * HG_DK)
                st = state[d, h]
                pending = []
                for u in range(HG_SUBS):
                    rows = sub_rows(u)
                    qc, kc, gc, vc = q_s[rows, sl], k_s[rows, sl], g_s[rows, sl], v_ref[rows, sl]
                    g_last = gc[c - 1:c, :] if d == 0 else gc[0:1, :]
                    inter = _mm(qc * jnp.exp(gc), st, _NT)
                    upd = _mm(vc, kc * jnp.exp(g_last - gc), _TN)
                    yield
                    st = st * jnp.exp(g_last) + upd
                    pending.append((rows, qc, kc, gc, vc, inter))
                state[d, h] = st
                lo, hi = slice(0, hb), slice(hb, c)
                src, dst, bnd = (lo, hi, hb - 1) if d == 0 else (hi, lo, hb)
                for rows, qc, kc, gc, vc, inter in pending:
                    g_bnd = gc[bnd:bnd + 1, :]
                    q_x = qc[dst] * jnp.exp(gc[dst] - g_bnd)
                    k_y = kc[src] * jnp.exp(g_bnd - gc[src])
                    v_src = vc[src]
                    cross = None
                    for j in range(hb):
                        term = jnp.sum(q_x * k_y[j:j + 1, :], axis=1, keepdims=True) * v_src[j:j + 1, :]
                        cross = term if cross is None else cross + term
                    o_lo = pairwise(qc[lo], kc[lo], gc[lo], vc[lo])
                    o_hi = pairwise(qc[hi], kc[hi], gc[hi], vc[hi])
                    if d == 0:
                        o_hi = o_hi + cross
                    else:
                        o_lo = o_lo + cross
                    o_ref[rows, sl] = inter + jnp.concatenate([o_lo, o_hi], axis=0)

            _interleave([chain(h) for h in range(HG_HEADS)])
            return carry

        lax.fori_loop(0, nsub // HG_SUBS, body, 0)


def _hg_scan(proj, hg_lower, layer, bsz, t_len, tb=128):
    n = proj.shape[0]
    nb = t_len // tb

    def spec(colb, rev):
        if rev:
            return pl.BlockSpec((tb, 512), lambda b, i: (b * nb + nb - 1 - i, colb))
        return pl.BlockSpec((tb, 512), lambda b, i: (b * nb + i, colb))

    return pl.pallas_call(
        functools.partial(_hg_scan_kernel, layer=layer), grid=(bsz, nb),
        in_specs=[spec(0, False), spec(1, False), spec(3, False),
                  spec(0, True), spec(2, True), spec(3, True),
                  pl.BlockSpec(hg_lower.shape, lambda b, i: (0, 0))],
        out_specs=[spec(0, False), spec(0, True)],
        out_shape=[jax.ShapeDtypeStruct((n, 512), F32)] * 2,
        scratch_shapes=[pltpu.VMEM((2, HG_HEADS, HG_DK, HG_DK), F32)]
        + [pltpu.VMEM((tb, 512), F32)] * 3,
        compiler_params=_params("parallel", "arbitrary"), name="hg_scan",
    )(proj, proj, proj, proj, proj, proj, hg_lower)


def _rw_pre_kernel(rc, rp, rn, kc, kp, kn, vc, vp, vn, lc, lp, ln, mu_ref, w0_ref, w2_ref,
                   a0_ref, a2_ref, g2_ref, kk_ref, ka_ref, bd_ref,
                   r_o, k_o, v_o, kk_o, kka_o, lwf_o, lwb_o, g_o, scr):
    i = pl.program_id(1)
    first = i == 0
    last = i == pl.num_programs(1) - 1
    tb = rc.shape[0]

    def shifted(cur_ref, prev_ref, next_ref, col0):
        w = cur_ref.shape[1]
        cur = cur_ref[...]
        scr[pl.ds(SUBLANES, tb), pl.ds(0, w)] = cur
        scr[pl.ds(0, SUBLANES), pl.ds(0, w)] = jnp.where(first, 0.0, prev_ref[...])
        scr[pl.ds(SUBLANES + tb, SUBLANES), pl.ds(0, w)] = jnp.where(last, 0.0, next_ref[...])
        prev = scr[pl.ds(SUBLANES - 1, tb), pl.ds(0, w)]
        nxt = scr[pl.ds(SUBLANES + 1, tb), pl.ds(0, w)]
        mu = mu_ref[:, col0:col0 + w]
        return cur + mu[0:1] * (prev - cur) + mu[1:2] * (nxt - cur)

    r = shifted(rc, rp, rn, 0)
    k = shifted(kc, kp, kn, RW_DIM)
    v = shifted(vc, vp, vn, 2 * RW_DIM)
    lo = shifted(lc, lp, ln, 3 * RW_DIM)
    lo_wa, lo_g = lo[:, :LANES], lo[:, LANES:]
    th = jnp.tanh(lo_wa)
    for d, o_ref in enumerate((lwf_o, lwb_o)):
        w_raw = w0_ref[d:d + 1, :] + _mm(th, w2_ref[d])
        o_ref[...] = -math.exp(-0.5) * _sigmoid(w_raw)
    a = _sigmoid(a0_ref[...] + _mm(lo_wa, a2_ref[...]))
    g_o[...] = _mm(_sigmoid(lo_g), g2_ref[...]).astype(g_o.dtype)
    kx = k * kk_ref[...]
    kk = kx * lax.rsqrt(_mm(kx * kx, bd_ref[...]) + L2_EPS)
    r_o[...] = r.astype(r_o.dtype)
    v_o[...] = v.astype(v_o.dtype)
    k_o[...] = (k * (1.0 + (a - 1.0) * ka_ref[...])).astype(k_o.dtype)
    kk_o[...] = kk.astype(kk_o.dtype)
    kka_o[...] = (kk * a).astype(kka_o.dtype)


def _rw_pre(proj, mu, w0, w2p, a0, a2p, g2, k_k, k_a, bd, bsz, t_len, tb=256):
    n = proj.shape[0]
    nb = t_len // tb
    base = 2560
    specs = []
    for off, w in ((0, 512), (512, 512), (1024, 512), (1536, 256)):
        specs += _halo_specs(w, (base + off) // w, tb, t_len, bsz)
    full = lambda a: pl.BlockSpec(a.shape, lambda b, i: (0,) * a.ndim)
    out_spec = pl.BlockSpec((tb, RW_DIM), lambda b, i: (b * nb + i, 0))
    consts = (mu, w0, w2p, a0, a2p, g2, k_k, k_a, bd)
    return pl.pallas_call(
        _rw_pre_kernel, grid=(bsz, nb),
        in_specs=specs + [full(a) for a in consts],
        out_specs=[out_spec] * 8,
        out_shape=[jax.ShapeDtypeStruct((n, RW_DIM), dt) for dt in (BF16,) * 5 + (F32, F32, BF16)],
        scratch_shapes=[pltpu.VMEM((tb + 2 * SUBLANES, RW_DIM), F32)],
        compiler_params=_params("parallel", "parallel"), name="rw_pre",
    )(*([proj] * 12), *consts)


def _rw_scan_kernel(rf, kf, vf, bf, af, wf, rb, kb, vb, bb, ab, wb, yf, yb, state):
    n = pl.program_id(1)

    @pl.when(n == 0)
    def _():
        state[...] = jnp.zeros_like(state)

    c = RW_CHUNK
    gw = RW_GROUP * RW_HD
    gc = RW_GROUP * c
    row = lax.broadcasted_iota(jnp.int32, (c, c), 0)
    col = lax.broadcasted_iota(jnp.int32, (c, c), 1)
    rr = lax.broadcasted_iota(jnp.int32, (gc, gc), 0)
    cc = lax.broadcasted_iota(jnp.int32, (gc, gc), 1)
    same = (rr // c) == (cc // c)
    head_lanes = (lax.broadcasted_iota(jnp.int32, (gc, gw), 0) // c
                  == lax.broadcasted_iota(jnp.int32, (gc, gw), 1) // RW_HD)
    rs = lax.broadcasted_iota(jnp.int32, (gw, gw), 0)
    cs = lax.broadcasted_iota(jnp.int32, (gw, gw), 1)
    head_blocks = ((rs // RW_HD) == (cs // RW_HD)).astype(F32)
    row_cat = lax.broadcasted_iota(jnp.int32, (c, gc), 0)
    col_cat = lax.broadcasted_iota(jnp.int32, (c, gc), 1) % c
    eye_cat = (col_cat == row_cat).astype(F32)

    def stack(x):
        return jnp.where(head_lanes, jnp.concatenate([x] * RW_GROUP, axis=0), 0.0)

    dirs = ((rf, kf, vf, bf, af, wf, yf), (rb, kb, vb, bb, ab, wb, yb))
    def chain(ib, d, p):
        r_ref, k_ref, v_ref, b_ref, a_ref, w_ref, y_ref = dirs[d]
        sl = slice(p * gw, (p + 1) * gw)
        incl = (col_cat <= row_cat) if d == 0 else (col_cat >= row_cat)
        strict = (col_cat < row_cat) if d == 0 else (col_cat > row_cat)
        lw = w_ref[ib, :, sl]
        tri = ((col <= row) if d == 0 else (col >= row)).astype(BF16)
        g_in = _sel_mm(tri, lw)
        yield
        g_ex = g_in - lw
        g_end = g_in[c - 1:c, :] if d == 0 else g_in[0:1, :]
        e_neg = jnp.exp(-g_in)
        e_end = jnp.exp(g_end - g_in)
        k, a, v = k_ref[ib, :, sl], a_ref[ib, :, sl], v_ref[ib, :, sl]
        lhs = jnp.concatenate([r_ref[ib, :, sl] * jnp.exp(g_in), b_ref[ib, :, sl] * jnp.exp(g_ex)],
                              axis=0)
        ak = _mm(lhs, stack(k * e_neg), _NT)
        yield
        aa = _mm(lhs, stack(a * e_neg), _NT)
        yield
        a_rk = jnp.where(incl, ak[:c], 0.0)
        a_ra = jnp.where(incl, aa[:c], 0.0)
        a_bk = jnp.where(strict, ak[c:], 0.0)
        a_ba = jnp.where(strict, aa[c:], 0.0)
        t_inv = yield from _unit_tri_inv_cat(a_ba, eye_cat, same, c)
        yield
        m = state[ib, d, p]
        xm = _mm(lhs, m, _NT)
        yield
        av = _mm(jnp.concatenate([a_rk, a_bk], axis=0), stack(v))
        yield
        u = _mm(t_inv, stack(xm[c:] + av[c:]))
        yield
        y_ref[ib, :, sl] = xm[:c] + av[:c] - _mm(a_ra, stack(u))
        yield
        upd = _mm(jnp.concatenate([v.astype(F32), u], axis=0),
                  jnp.concatenate([k * e_end, -(a * e_end)], axis=0), _TN)
        state[ib, d, p] = m * jnp.exp(g_end) + upd * head_blocks

    _interleave([chain(ib, d, p) for ib in range(rf.shape[0]) for d in range(2)
                 for p in range(RW_DIM // gw)])


def _rw_scan(r, k, v, kk, kka, lwf, lwb, bsz, t_len, bb=SCAN_BATCH):
    n = r.shape[0]
    c = RW_CHUNK
    gw = RW_GROUP * RW_HD
    nc = t_len // c
    bb = math.gcd(bb, bsz)
    r, k, v, kk, kka, lwf, lwb =(a.reshape(bsz, t_len, RW_DIM) for a in (r, k, v, kk, kka, lwf, lwb))
    fwd = pl.BlockSpec((bb, c, RW_DIM), lambda g, i: (g, i, 0))
    bwd = pl.BlockSpec((bb, c, RW_DIM), lambda g, i: (g, nc - 1 - i, 0))
    yf, yb = pl.pallas_call(
        _rw_scan_kernel, grid=(bsz // bb, nc),
        in_specs=[fwd] * 6 + [bwd] * 6, out_specs=[fwd, bwd],
        out_shape=[jax.ShapeDtypeStruct((bsz, t_len, RW_DIM), F32)] * 2,
        scratch_shapes=[pltpu.VMEM((bb, 2, RW_DIM // gw, gw, gw), F32)],
        compiler_params=_params("parallel", "arbitrary"), name="rw_scan",
    )(r, k, v, kk, kka, lwf, r, k, v, kk, kka, lwb)
    return yf.reshape(n, RW_DIM), yb.reshape(n, RW_DIM)


def _out_odd_kernel(x_ref, hf_ref, hb_ref, hg_ref, hnw_ref, yf_ref, yb_ref, r_ref, k_ref, v_ref,
                    gate_ref, rk_ref, lw_ref, lb_ref, bd_ref, wc_ref, wd_ref, g_ref, b_ref, o_ref):
    nw = hnw_ref[...]
    mix = None
    for h in range(HG_HEADS):
        sl = slice(h * HG_DK, (h + 1) * HG_DK)
        o = hf_ref[:, sl] + hb_ref[:, sl]
        y = o * lax.rsqrt(jnp.mean(o * o, -1, keepdims=True) + RMS_EPS) * nw * _silu(hg_ref[:, sl])
        part = _mm(y, wc_ref[sl, :])
        mix = part if mix is None else mix + part
    bd = bd_ref[...]
    y = yf_ref[...] + yb_ref[...]
    yc = y - _sel_mm(bd, y, sel_first=False) * (1.0 / RW_HD)
    var = _mm(yc * yc, bd) * (1.0 / RW_HD)
    yn = yc * lax.rsqrt(var + RW_GN_EPS) * lw_ref[...] + lb_ref[...]
    rk = r_ref[...].astype(F32) * k_ref[...] * rk_ref[...]
    bonus = _mm(rk, bd) * v_ref[...]
    mix = mix + _mm((yn + bonus) * gate_ref[...], wd_ref[...])
    o_ref[...] = _layer_norm(DEEPNORM_ALPHA * x_ref[...] + mix, g_ref[...], b_ref[...])


def _out_odd(x2, hf, hb, proj, hnw, yf, yb, r, k, v, gate, r_k, lnx_w, lnx_b, bd, wc, wd, g, b,
             tm=256):
    n, d = x2.shape
    row = lambda w: pl.BlockSpec((tm, w), lambda i: (i, 0))
    full = lambda a: pl.BlockSpec(a.shape, lambda i: (0,) * a.ndim)
    consts = (r_k, lnx_w, lnx_b, bd, wc, wd, g, b)
    return pl.pallas_call(
        _out_odd_kernel, grid=(n // tm,),
        in_specs=[row(d), row(512), row(512), pl.BlockSpec((tm, 512), lambda i: (i, 4)), full(hnw)]
        + [row(512)] * 6 + [full(a) for a in consts],
        out_specs=row(d), out_shape=jax.ShapeDtypeStruct((n, d), F32),
        compiler_params=_params("parallel"), name="out_odd",
    )(x2, hf, hb, proj, hnw, yf, yb, r, k, v, gate, *consts)


def _odd_layer(x2, bsz, t_len, layer, w_in, hg_lower, hg_norm_w, rw_mu, rw_w0, rw_w2, rw_a0, rw_a2,
               rw_g2, rw_k_k, rw_k_a, rw_r_k, rw_lnx_w, rw_lnx_b, w_out, ln_g, ln_b):
    proj = _project(x2, w_in.astype(BF16))
    hf, hb = _hg_scan(proj, hg_lower, layer, bsz, t_len)
    idx = np.arange(RW_DIM) // RW_HD
    bd = jnp.asarray(idx[:, None] == idx[None, :], BF16)
    lora = rw_w2.shape[1]
    w2p = jnp.pad(rw_w2, ((0, 0), (0, LANES - lora), (0, 0)))
    a2p = jnp.pad(rw_a2, ((LANES - rw_a2.shape[0], 0), (0, 0)))
    r, k, v, kk, kka, lwf, lwb, gate = _rw_pre(proj, rw_mu, rw_w0, w2p, rw_a0[None], a2p, rw_g2,
                                               rw_k_k[None], rw_k_a[None], bd, bsz, t_len)
    yf, yb = _rw_scan(r, k, v, kk, kka, lwf, lwb, bsz, t_len)
    return _out_odd(x2, hf, hb, proj, hg_norm_w[None], yf, yb, r, k, v, gate,
                    rw_r_k.reshape(1, RW_DIM), rw_lnx_w[None], rw_lnx_b[None], bd,
                    w_out[:512].astype(BF16), w_out[512:].astype(BF16), ln_g[None], ln_b[None])


def kernel(x, ev_w_in, hy_conv_w, hy_conv_b, hy_filt_w1, hy_filt_b1, hy_filt_w2, hy_filt_b2, hy_filt_w3, hy_filt_b3, hy_filt_w4, hy_filt_freq, hy_skip, gdn_conv_w, gdn_a_log, gdn_dt_bias, gdn_norm_w, ev_w_out, od_w_in, hg_lower, hg_norm_w, rw_mu, rw_w0, rw_w2, rw_a0, rw_a2, rw_g2, rw_k_k, rw_k_a, rw_r_k, rw_lnx_w, rw_lnx_b, od_w_out, ln_g, ln_b, mlp_w1, mlp_w2):
    bsz, t_len, d = x.shape
    x2 = x.reshape(bsz * t_len, d)
    for layer in range(DEPTH):
        i = layer // 2
        if layer % 2 == 0:
            x2 = _even_layer(x2, bsz, t_len, ev_w_in[i], hy_conv_w[i], hy_conv_b[i], hy_filt_w1[i],
                             hy_filt_b1[i], hy_filt_w2[i], hy_filt_b2[i], hy_filt_w3[i],
                             hy_filt_b3[i], hy_filt_w4[i], hy_filt_freq[i], hy_skip[i],
                             gdn_conv_w[i], gdn_a_log[i], gdn_dt_bias[i], gdn_norm_w[i],
                             ev_w_out[i], ln_g[layer, 0], ln_b[layer, 0])
        else:
            x2 = _odd_layer(x2, bsz, t_len, layer, od_w_in[i], hg_lower, hg_norm_w[i], rw_mu[i],
                            rw_w0[i], rw_w2[i], rw_a0[i], rw_a2[i], rw_g2[i], rw_k_k[i], rw_k_a[i],
                            rw_r_k[i], rw_lnx_w[i], rw_lnx_b[i], od_w_out[i], ln_g[layer, 0],
                            ln_b[layer, 0])
        x2 = _mlp_block(x2, mlp_w1[layer].astype(BF16), mlp_w2[layer].astype(BF16),
                        ln_g[layer, 1][None], ln_b[layer, 1][None])
    return x2.reshape(bsz, t_len, d)
```

```python
import functools
import math

import numpy as np
import jax
import jax.numpy as jnp
from jax import lax
from jax.experimental import pallas as pl
from jax.experimental.pallas import tpu as pltpu

F32 = jnp.float32
BF16 = jnp.bfloat16

D_MODEL = 1024
DEPTH = 2
D_FF = 4 * D_MODEL
DEEPNORM_ALPHA = (2.0 * DEPTH) ** 0.25
LN_EPS = 1e-5
RMS_EPS = 1e-6
L2_EPS = 1e-6

HY_DIM = 512
HY_EMB = 33
HY_BANDS = 16
HY_FILT_HID = 64
HY_DECAY_TARGET = 1e-2
HY_FAST_DECAY_PCT = 0.3
HY_SLOW_DECAY_PCT = 1.5

GDN_HEADS = 4
GDN_DK = 128
GDN_CHUNK = 64

HG_HEADS = 4
HG_DK = 128
HG_CHUNK = 16
HG_SUBS = 4

RW_HEADS = 8
RW_HD = 64
RW_DIM = 512
RW_CHUNK = 64
RW_GROUP = 4
RW_GN_EPS = 64e-5

SCAN_BATCH = 4

LANES = 128
SUBLANES = 8
VMEM_LIMIT = 56 * 1024 * 1024

_NN = (((1,), (0,)), ((), ()))
_NT = (((1,), (1,)), ((), ()))
_TN = (((0,), (0,)), ((), ()))


def _mm(a, b, dims=_NN, exact=False):
    if exact:
        return lax.dot_general(a, b, dims, precision=lax.Precision.HIGHEST,
                               preferred_element_type=F32)
    return lax.dot_general(a.astype(BF16), b.astype(BF16), dims, preferred_element_type=F32)


def _params(*sem):
    return pltpu.CompilerParams(dimension_semantics=sem, vmem_limit_bytes=VMEM_LIMIT)


def _sigmoid(x):
    return 1.0 / (1.0 + jnp.exp(-x))


def _silu(x):
    return x * _sigmoid(x)


def _softplus(x):
    return jnp.maximum(x, 0.0) + jnp.log(1.0 + jnp.exp(-jnp.abs(x)))


def _layer_norm(v, g, b):
    mean = jnp.mean(v, axis=-1, keepdims=True)
    vc = v - mean
    var = jnp.mean(vc * vc, axis=-1, keepdims=True)
    return vc * lax.rsqrt(var + LN_EPS) * g + b


def _col(x, idx, lane):
    return jnp.sum(jnp.where(lane == idx, x, 0.0), axis=1, keepdims=True)


def _split2(a):
    hi = a.astype(BF16)
    return hi, (a - hi.astype(F32)).astype(BF16)


def _mm3(a, b, dims=_NN):
    ah, al = _split2(a)
    bh, bl = _split2(b)
    dot = functools.partial(lax.dot_general, dimension_numbers=dims, preferred_element_type=F32)
    return dot(ah, bh) + dot(ah, bl) + dot(al, bh)


def _sel_mm(sel, x, dims=_NN, sel_first=True):
    hi = x.astype(BF16)
    r1 = x - hi.astype(F32)
    mid = r1.astype(BF16)
    lo = (r1 - mid.astype(F32)).astype(BF16)
    dot = functools.partial(lax.dot_general, dimension_numbers=dims, preferred_element_type=F32)
    if sel_first:
        return dot(sel, hi) + dot(sel, mid) + dot(sel, lo)
    return dot(hi, sel) + dot(mid, sel) + dot(lo, sel)


def _expand_bd(x_cat, same):
    g = x_cat.shape[1] // x_cat.shape[0]
    return jnp.where(same, jnp.concatenate([x_cat] * g, axis=0), 0.0)


def _unit_tri_inv_cat(a_cat, eye_cat, same, depth):
    c = a_cat.shape[0]
    p = -a_cat
    t = eye_cat + p
    p = _mm(p, _expand_bd(p, same))
    yield
    k = 4
    while k < depth:
        both = _mm(jnp.concatenate([p, t], axis=0), _expand_bd(p, same))
        yield
        p, t = both[:c], t + both[c:]
        k *= 2
    return t + _mm(t, _expand_bd(p, same))


def _interleave(chains):
    active = list(chains)
    while active:
        still = []
        for g in active:
            try:
                next(g)
                still.append(g)
            except StopIteration:
                pass
        active = still


def _proj_kernel(x_ref, w_ref, o_ref):
    o_ref[...] = _mm(x_ref[...], w_ref[...])


def _proj_t_kernel(x_ref, w_ref, wt_ref, o_ref, ot_ref):
    xb = x_ref[...].astype(BF16)
    o_ref[...] = _mm(xb, w_ref[...])
    ot_ref[...] = _mm(wt_ref[...], xb, _NT).astype(ot_ref.dtype)


def _project(x2, w, wt=None, tm=512):
    n, d = x2.shape
    m = w.shape[1]
    grid = (n // tm,)
    once = pl.Buffered(1)
    x_spec = pl.BlockSpec((tm, d), lambda i: (i, 0))
    w_spec = pl.BlockSpec((d, m), lambda i: (0, 0), pipeline_mode=once)
    o_spec = pl.BlockSpec((tm, m), lambda i: (i, 0))
    if wt is None:
        return pl.pallas_call(
            _proj_kernel, grid=grid, in_specs=[x_spec, w_spec], out_specs=o_spec,
            out_shape=jax.ShapeDtypeStruct((n, m), F32), compiler_params=_params("parallel"),
            name="proj")(x2, w)
    mt = wt.shape[0]
    return pl.pallas_call(
        _proj_t_kernel, grid=grid,
        in_specs=[x_spec, w_spec, pl.BlockSpec((mt, d), lambda i: (0, 0), pipeline_mode=once)],
        out_specs=[o_spec, pl.BlockSpec((mt, tm), lambda i: (0, i))],
        out_shape=[jax.ShapeDtypeStruct((n, m), F32), jax.ShapeDtypeStruct((mt, n), BF16)],
        compiler_params=_params("parallel"), name="proj_t")(x2, w, wt)


def _mlp_tail(x, w1_ref, w2_ref, g_ref, b_ref):
    h = jnp.maximum(_mm(x, w1_ref[...]), 0.0)
    return _layer_norm(DEEPNORM_ALPHA * x + _mm(h * h, w2_ref[...]), g_ref[...], b_ref[...])


def _resident(a):
    return pl.BlockSpec(a.shape, lambda i: (0,) * a.ndim, pipeline_mode=pl.Buffered(1))


def _fill_halo(scr, cur, prev8, next8, first, last, halo):
    tb = cur.shape[0]
    scr[pl.ds(SUBLANES, tb), :] = cur
    scr[pl.ds(0, SUBLANES), :] = jnp.where(first, 0.0, prev8)
    scr[pl.ds(SUBLANES + tb, SUBLANES), :] = jnp.where(last, 0.0, next8)


def _halo_specs(width, col_block, tb, t_len, bsz):
    nb = t_len // tb
    r8 = tb // SUBLANES
    tot8 = t_len // SUBLANES

    def cur(b, i):
        return (b * nb + i, col_block)

    def prev(b, i):
        return (jnp.maximum(b * tot8 + i * r8 - 1, 0), col_block)

    def nxt(b, i):
        return (jnp.minimum(b * tot8 + (i + 1) * r8, tot8 * bsz - 1), col_block)

    return [pl.BlockSpec((tb, width), cur), pl.BlockSpec((SUBLANES, width), prev),
            pl.BlockSpec((SUBLANES, width), nxt)]


def _gdn_pre_kernel(cur_ref, prev_ref, next_ref, ab_ref, cw_ref, alog_ref, dtb_ref,
                    q_ref, k_ref, v_ref, g_ref, scr):
    i = pl.program_id(1)
    tb = cur_ref.shape[0]
    _fill_halo(scr, cur_ref[...], prev_ref[...], next_ref[...], i == 0,
               i == pl.num_programs(1) - 1, 2)
    width = cw_ref.shape[0]
    acc = None
    for j in range(width):
        term = scr[pl.ds(SUBLANES - width // 2 + j, tb), :] * cw_ref[pl.ds(j, 1), :]
        acc = term if acc is None else acc + term
    qkv = _silu(acc)
    hd = GDN_HEADS * GDN_DK
    for h in range(GDN_HEADS):
        sl = slice(h * GDN_DK, (h + 1) * GDN_DK)
        q = qkv[:, sl]
        q_ref[:, sl] = (q * lax.rsqrt(jnp.sum(q * q, -1, keepdims=True) + L2_EPS)
                        * GDN_DK ** -0.5).astype(q_ref.dtype)
        k = qkv[:, hd + h * GDN_DK: hd + (h + 1) * GDN_DK]
        k_ref[:, sl] = (k * lax.rsqrt(jnp.sum(k * k, -1, keepdims=True) + L2_EPS)).astype(k_ref.dtype)
    v_ref[...] = qkv[:, 2 * hd:].astype(v_ref.dtype)
    ab = ab_ref[...]
    lane = lax.broadcasted_iota(jnp.int32, ab.shape, 1)
    log_g = -jnp.exp(alog_ref[...]) * _softplus(ab + dtb_ref[...])
    g_ref[...] = jnp.where(lane < 2 * GDN_HEADS, log_g, _sigmoid(ab))


def _gdn_pre(proj, conv_w, a_log_row, dt_bias_row, bsz, t_len, tb=256):
    n = proj.shape[0]
    wq = 3 * GDN_HEADS * GDN_DK
    nb = t_len // tb
    specs = _halo_specs(wq, 0, tb, t_len, bsz)
    ab_col = (wq + GDN_HEADS * GDN_DK) // LANES
    out_spec = pl.BlockSpec((tb, 512), lambda b, i: (b * nb + i, 0))
    return pl.pallas_call(
        _gdn_pre_kernel, grid=(bsz, nb),
        in_specs=specs + [pl.BlockSpec((tb, LANES), lambda b, i: (b * nb + i, ab_col)),
                          pl.BlockSpec(conv_w.shape, lambda b, i: (0, 0)),
                          pl.BlockSpec((1, LANES), lambda b, i: (0, 0)),
                          pl.BlockSpec((1, LANES), lambda b, i: (0, 0))],
        out_specs=[out_spec, out_spec, out_spec,
                   pl.BlockSpec((tb, LANES), lambda b, i: (b * nb + i, 0))],
        out_shape=[jax.ShapeDtypeStruct((n, 512), BF16)] * 3 + [jax.ShapeDtypeStruct((n, LANES), F32)],
        scratch_shapes=[pltpu.VMEM((tb + 2 * SUBLANES, wq), F32)],
        compiler_params=_params("parallel", "parallel"), name="gdn_pre",
    )(proj, proj, proj, proj, conv_w, a_log_row, dt_bias_row)


def _gdn_scan_kernel(qf, kf, vf, gf, qb, kb, vb, gb, of, ob, state):
    n = pl.program_id(1)

    @pl.when(n == 0)
    def _():
        state[...] = jnp.zeros_like(state)

    c = GDN_CHUNK
    nh = GDN_HEADS
    hc = nh * c
    wide = nh * GDN_DK
    row = lax.broadcasted_iota(jnp.int32, (c, c), 0)
    col = lax.broadcasted_iota(jnp.int32, (c, c), 1)
    lane = lax.broadcasted_iota(jnp.int32, (c, LANES), 1)
    rr = lax.broadcasted_iota(jnp.int32, (hc, hc), 0)
    cc = lax.broadcasted_iota(jnp.int32, (hc, hc), 1)
    same = (rr // c) == (cc // c)
    head_lanes = (lax.broadcasted_iota(jnp.int32, (hc, wide), 0) // c
                  == lax.broadcasted_iota(jnp.int32, (hc, wide), 1) // GDN_DK)
    lane_blk = lax.broadcasted_iota(jnp.int32, (1, hc), 1) // c
    row_cat = lax.broadcasted_iota(jnp.int32, (c, hc), 0)
    lane_cat = lax.broadcasted_iota(jnp.int32, (c, hc), 1)
    col_cat = lane_cat % c
    blk_cat = lane_cat // c
    eye_cat = (col_cat == row_cat).astype(F32)

    def col_cat_of(x, first):
        out = jnp.zeros((c, hc), F32)
        for h in range(nh):
            out = jnp.where(blk_cat == h, _col(x, first + h, lane), out)
        return out

    def stack_wide(x):
        return jnp.where(head_lanes, jnp.concatenate([x] * nh, axis=0), 0.0)

    def stack_narrow(x):
        return jnp.concatenate([x[:, h * GDN_DK:(h + 1) * GDN_DK] for h in range(nh)], axis=0)

    dirs = ((qf, kf, vf, gf, of), (qb, kb, vb, gb, ob))

    def chain(bb, d):
        q_ref, k_ref, v_ref, g_ref, o_ref = dirs[d]
        incl1 = (col <= row) if d == 0 else (col >= row)
        incl = (col_cat <= row_cat) if d == 0 else (col_cat >= row_cat)
        strict = (col_cat < row_cat) if d == 0 else (col_cat > row_cat)
        gates = g_ref[bb]
        tri = incl1.astype(BF16)
        gam_all = _sel_mm(tri, gates)
        tri4 = jnp.concatenate([tri] * nh, axis=0)
        gam_rows = _sel_mm(tri4, gates.T[:2 * nh], _NT, sel_first=False)
        yield
        gam = jnp.concatenate([_col(gam_all, d * nh + h, lane) for h in range(nh)], axis=0)
        beta = jnp.concatenate([_col(gates, 2 * nh + h, lane) for h in range(nh)], axis=0)
        gam_row = jnp.zeros((1, hc), F32)
        for h in range(nh):
            gam_row = jnp.where(lane_blk == h, gam_rows[d * nh + h:d * nh + h + 1, :], gam_row)
        last = c - 1 if d == 0 else 0
        g_last = jnp.concatenate(
            [jnp.broadcast_to(gam[h * c + last:h * c + last + 1, :], (c, 1)) for h in range(nh)], axis=0)
        g_last_s = jnp.concatenate(
            [jnp.broadcast_to(gam[h * c + last:h * c + last + 1, :], (GDN_DK, 1)) for h in range(nh)],
            axis=0)
        diff = col_cat_of(gam_all, d * nh) - gam_row
        decay = jnp.where(incl, jnp.exp(jnp.where(incl, diff, 0.0)), 0.0)
        e_gam = jnp.exp(gam)
        k_in, q_in = k_ref[bb], q_ref[bb]
        k_w = stack_wide(k_in)
        kq = _mm(jnp.concatenate([k_in, q_in], axis=0), k_w, _NT)
        yield
        a_cat = jnp.where(strict, kq[:c] * decay * col_cat_of(gates, 2 * nh), 0.0)
        attn = _expand_bd(kq[c:] * decay, same)
        t_cat = yield from _unit_tri_inv_cat(a_cat, eye_cat, same, c)
        yield
        rhs = jnp.concatenate([stack_narrow(v_ref[bb]) * beta,
                               stack_narrow(k_in) * (beta * e_gam)], axis=1)
        sol = _mm(_expand_bd(t_cat, same), rhs)
        yield
        u, w = sol[:, :GDN_DK], sol[:, GDN_DK:]
        s = state[bb, d]
        w_w = jnp.where(head_lanes, jnp.concatenate([w] * nh, axis=1), 0.0)
        ws = _mm(jnp.concatenate([w_w, stack_wide(q_in) * e_gam], axis=0), s)
        yield
        v_new = u - ws[:hc]
        out = ws[hc:] + _mm(attn, v_new)
        o_ref[bb] = jnp.concatenate([out[h * c:(h + 1) * c] for h in range(nh)], axis=1)
        k_dec = k_w * jnp.exp(g_last - gam)
        state[bb, d] = s * jnp.exp(g_last_s) + _mm(k_dec, v_new, _TN)

    _interleave([chain(bb, d) for bb in range(qf.shape[0]) for d in range(2)])


def _gdn_scan(q, k, v, gates, bsz, t_len, bb=SCAN_BATCH):
    n = q.shape[0]
    c = GDN_CHUNK
    nc = t_len // c
    bb = math.gcd(bb, bsz)
    q, k, v, gates =(a.reshape(bsz, t_len, a.shape[1]) for a in (q, k, v, gates))

    def fwd(g, i):
        return (g, i, 0)

    def bwd(g, i):
        return (g, nc - 1 - i, 0)

    def specs(idx):
        wide = pl.BlockSpec((bb, c, 512), idx)
        return [wide, wide, wide, pl.BlockSpec((bb, c, LANES), idx)]

    of, ob = pl.pallas_call(
        _gdn_scan_kernel, grid=(bsz // bb, nc),
        in_specs=specs(fwd) + specs(bwd),
        out_specs=[pl.BlockSpec((bb, c, 512), fwd), pl.BlockSpec((bb, c, 512), bwd)],
        out_shape=[jax.ShapeDtypeStruct((bsz, t_len, 512), F32)] * 2,
        scratch_shapes=[pltpu.VMEM((bb, 2, GDN_HEADS * GDN_DK, GDN_DK), F32)],
        compiler_params=_params("parallel", "arbitrary"), name="gdn_scan",
    )(q, k, v, gates, q, k, v, gates)
    return of.reshape(n, 512), ob.reshape(n, 512)


_fft_mm = _mm
HY_CB = 16
HY_CHAINS = 2


def _hy_filter_kernel(z_ref, w1_ref, b1_ref, w2_ref, b2_ref, w3_ref, b3_ref, fr_ref, w4t_ref,
                      dl_ref, o_ref, *, t_len):
    i = pl.program_id(0)
    tl = o_ref.shape[1]
    fr = fr_ref[...]
    h = jnp.sin(fr * (_mm(z_ref[...], w1_ref[...], exact=True) + b1_ref[...]))
    h = jnp.sin(fr * (_mm(h, w2_ref[...], exact=True) + b2_ref[...]))
    h = jnp.sin(fr * (_mm(h, w3_ref[...], exact=True) + b3_ref[...]))
    ht = _mm(w4t_ref[...], h, _NT, exact=True)
    pos = (lax.broadcasted_iota(jnp.int32, (1, tl), 1) + i * tl).astype(F32)
    t_norm = pos / float(max(t_len - 1, 1))
    o_ref[...] = ht * jnp.exp(-t_norm * dl_ref[...])


def _hy_filters(z, w1p, b1, w2, b2, w3, b3, freq, w4t, deltas, t_len):
    tl = min(512, t_len)
    nch = w4t.shape[0]
    full = lambda a: pl.BlockSpec(a.shape, lambda i: (0,) * a.ndim)
    return pl.pallas_call(
        functools.partial(_hy_filter_kernel, t_len=t_len), grid=(t_len // tl,),
        in_specs=[pl.BlockSpec((tl, z.shape[1]), lambda i: (i, 0))]
        + [full(a) for a in (w1p, b1, w2, b2, w3, b3, freq, w4t, deltas)],
        out_specs=pl.BlockSpec((nch, tl), lambda i: (0, i)),
        out_shape=jax.ShapeDtypeStruct((nch, t_len), F32),
        compiler_params=_params("parallel"), name="hy_filter",
    )(z, w1p, b1, w2, b2, w3, b3, freq, w4t, deltas)


def _fft_fwd(x3, x3i, fst, twr, twi, wbig):
    cb, r = x3.shape[0], x3.shape[1]
    n1 = twr.shape[0]
    xcat = jnp.concatenate([x3[c] for c in range(cb)], axis=1)
    if x3i is None:
        y = _fft_mm(fst[:, :r], xcat)
    else:
        xicat = jnp.concatenate([x3i[c] for c in range(cb)], axis=1)
        y = _fft_mm(fst, jnp.concatenate([xcat, xicat], axis=0))
    yield
    yr, yi = y[:n1], y[n1:]
    twr_t = jnp.concatenate([twr] * cb, axis=1)
    twi_t = jnp.concatenate([twi] * cb, axis=1)
    tr = yr * twr_t - yi * twi_t
    ti = yr * twi_t + yi * twr_t
    cat = jnp.concatenate(
        [jnp.concatenate([tr[:, c * LANES:(c + 1) * LANES], ti[:, c * LANES:(c + 1) * LANES]], axis=1)
         for c in range(cb)], axis=0)
    z = _fft_mm(cat, wbig)
    yield
    return z[:, :LANES].reshape(cb, n1, LANES), z[:, LANES:].reshape(cb, n1, LANES)


def _fft_inv(zr, zi, gst, twr, twi, wbigc):
    cb, n1 = zr.shape[0], zr.shape[1]
    r = gst.shape[0] // 2
    cat = jnp.concatenate([zr, zi], axis=-1).reshape(cb * n1, 2 * LANES)
    a = _fft_mm(cat, wbigc)
    yield
    ar, ai = a[:, :LANES].reshape(cb, n1, LANES), a[:, LANES:].reshape(cb, n1, LANES)
    tr = ar * twr + ai * twi
    ti = ai * twr - ar * twi
    tt = jnp.concatenate([jnp.concatenate([tr[c] for c in range(cb)], axis=1),
                          jnp.concatenate([ti[c] for c in range(cb)], axis=1)], axis=0)
    out = _fft_mm(gst, tt)
    yield
    return (jnp.stack([out[:r, c * LANES:(c + 1) * LANES] for c in range(cb)]),
            jnp.stack([out[r:, c * LANES:(c + 1) * LANES] for c in range(cb)]))


def _hy_spec_kernel(hf_ref, hb_ref, fst_ref, twr_ref, twi_ref, wbig_ref, hr_ref, hi_ref):
    args = (fst_ref[...], twr_ref[...], twi_ref[...], wbig_ref[...])
    out = {}

    def chain(name, ref):
        out[name] = yield from _fft_fwd(ref[...], None, *args)

    _interleave([chain("f", hf_ref), chain("b", hb_ref)])
    hr_ref[...] = out["f"][0] + out["b"][0]
    hi_ref[...] = out["f"][1] - out["b"][1]


def _hy_spectra(filt3, consts, cb=HY_CB):
    fst, gst, twr, twi, wbig, wbigc = consts
    r = filt3.shape[1]
    n1 = twr.shape[0]
    nj = HY_DIM // cb
    full = lambda a: pl.BlockSpec(a.shape, lambda o, j: (0,) * a.ndim)
    out_spec = pl.BlockSpec((cb, n1, LANES), lambda o, j: (o * nj + j, 0, 0))
    return pl.pallas_call(
        _hy_spec_kernel, grid=(2, nj),
        in_specs=[pl.BlockSpec((cb, r, LANES), lambda o, j: (2 * o * nj + j, 0, 0)),
                  pl.BlockSpec((cb, r, LANES), lambda o, j: ((2 * o + 1) * nj + j, 0, 0)),
                  full(fst), full(twr), full(twi), full(wbig)],
        out_specs=[out_spec, out_spec],
        out_shape=[jax.ShapeDtypeStruct((2 * HY_DIM, n1, LANES), F32)] * 2,
        compiler_params=_params("parallel", "parallel"), name="hy_spec",
    )(filt3, filt3, fst, twr, twi, wbig)


def _tshift(x3, back, lane, rowi):
    r = x3.shape[1]
    if back:
        rl = pltpu.roll(x3, 1, 2)
        rr = pltpu.roll(rl, 1, 1)
        edge = lane == 0
        out = jnp.where(edge, rr, rl)
        return jnp.where(edge & (rowi == 0), 0.0, out)
    rl = pltpu.roll(x3, LANES - 1, 2)
    rr = pltpu.roll(rl, r - 1, 1)
    edge = lane == LANES - 1
    out = jnp.where(edge, rr, rl)
    return jnp.where(edge & (rowi == r - 1), 0.0, out)


def _hy_conv_kernel(x1_ref, x2_ref, v_ref, p1_ref, p2_ref, pv_ref, sk_ref, h1r_ref, h1i_ref,
                    h2r_ref, h2i_ref, fst_ref, gst_ref, twr_ref, twi_ref, wbig_ref, wbigc_ref,
                    o_ref):
    cb, nb, r, _ = x1_ref.shape
    half = cb // HY_CHAINS
    shape = (half, r, LANES)
    lane = lax.broadcasted_iota(jnp.int32, shape, 2)
    rowi = lax.broadcasted_iota(jnp.int32, shape, 1)
    fwd = (fst_ref[...], twr_ref[...], twi_ref[...], wbig_ref[...])
    inv = (gst_ref[...], twr_ref[...], twi_ref[...], wbigc_ref[...])

    def chain(lo):
        ch = pl.ds(lo, half)

        def short_conv(x_ref, p_ref):
            p = p_ref[ch]
            out = []
            for b in range(nb):
                x = x_ref[ch, b].astype(F32)
                out.append(p[:, 0:1, :] * _tshift(x, True, lane, rowi) + p[:, 1:2, :] * x
                           + p[:, 2:3, :] * _tshift(x, False, lane, rowi) + p[:, 3:4, :])
            return out

        x1 = short_conv(x1_ref, p1_ref)
        x2 = short_conv(x2_ref, p2_ref)
        z = short_conv(v_ref, pv_ref)
        sk = sk_ref[ch]
        for o, (gate, hr_ref, hi_ref) in enumerate(((x1, h1r_ref, h1i_ref), (x2, h2r_ref, h2i_ref))):
            zr, zi = yield from _fft_fwd(z[0], z[1], *fwd)
            hr, hi = hr_ref[ch], hi_ref[ch]
            y = yield from _fft_inv(zr * hr - zi * hi, zr * hi + zi * hr, *inv)
            z = [gate[b] * (y[b] + sk[:, o:o + 1, :] * z[b]) for b in range(nb)]
        for b in range(nb):
            o_ref[ch, b] = z[b].astype(o_ref.dtype)

    _interleave([chain(i * half) for i in range(HY_CHAINS)])


def _hy_conv(projt4, pp, ps, hr, hi, consts, cb=HY_CB):
    fst, gst, twr, twi, wbig, wbigc = consts
    _, bsz, r, _ = projt4.shape
    n1 = twr.shape[0]
    nj = HY_DIM // cb
    full = lambda a: pl.BlockSpec(a.shape, lambda j, b: (0,) * a.ndim)

    assert bsz % 2 == 0, "sequences are transformed in pairs"

    def xs(off):
        return pl.BlockSpec((cb, 2, r, LANES), lambda j, b: (off * nj + j, b, 0, 0))

    def ps_(off):
        return pl.BlockSpec((cb, SUBLANES, LANES), lambda j, b: (off * nj + j, 0, 0))

    def hs(off):
        return pl.BlockSpec((cb, n1, LANES), lambda j, b: (off * nj + j, 0, 0))

    return pl.pallas_call(
        _hy_conv_kernel, grid=(nj, bsz // 2),
        in_specs=[xs(0), xs(1), xs(2), ps_(0), ps_(1), ps_(2), ps_(0), hs(0), hs(0), hs(1), hs(1),
                  full(fst), full(gst), full(twr), full(twi), full(wbig), full(wbigc)],
        out_specs=pl.BlockSpec((cb, 2, r, LANES), lambda j, b: (j, b, 0, 0)),
        out_shape=jax.ShapeDtypeStruct((HY_DIM, bsz, r, LANES), BF16),
        compiler_params=_params("parallel", "arbitrary"), name="hy_conv",
    )(projt4, projt4, projt4, pp, pp, pp, ps, hr, hi, hr, hi, fst, gst, twr, twi, wbig, wbigc)


def _fft_consts(t_len):
    n = 2 * t_len
    n1 = n // LANES
    r = t_len // LANES
    k1 = np.arange(n1)[:, None]
    th1 = 2.0 * np.pi * k1 * np.arange(r)[None, :] / n1
    c1, s1 = np.cos(th1), np.sin(th1)
    fst = np.block([[c1, s1], [-s1, c1]])
    gst = np.block([[c1.T, -s1.T], [s1.T, c1.T]]) / n
    tht = 2.0 * np.pi * k1 * np.arange(LANES)[None, :] / n
    twr, twi = np.cos(tht), -np.sin(tht)
    th2 = 2.0 * np.pi * np.arange(LANES)[:, None] * np.arange(LANES)[None, :] / LANES
    cr, ci = np.cos(th2), -np.sin(th2)
    wbig = np.block([[cr, ci], [-ci, cr]])
    wbigc = np.block([[cr, -ci], [ci, cr]])
    return tuple(jnp.asarray(a, F32) for a in (fst, gst, twr, twi, wbig, wbigc))


def _hy_positions(t_len):
    pos = np.arange(t_len, dtype=np.float32)[:, None]
    t = pos / np.float32(max(t_len - 1, 1))
    bands = np.linspace(1e-4, HY_BANDS - 1, HY_BANDS, dtype=np.float32)[None, :]
    ang = bands * np.float32(2.0 * math.pi / t_len) * pos
    z = np.concatenate([t, np.cos(ang), -np.sin(ang)], -1).astype(np.float32)
    return jnp.asarray(np.pad(z, ((0, 0), (0, HY_FILT_HID - HY_EMB))))


def _hy_deltas():
    max_decay = math.log(HY_DECAY_TARGET) / HY_FAST_DECAY_PCT
    min_decay = math.log(HY_DECAY_TARGET) / HY_SLOW_DECAY_PCT
    d = np.abs(np.linspace(min_decay, max_decay, HY_DIM, dtype=np.float32))
    return jnp.asarray(np.tile(d, 4)[:, None])


def _lane_rep(a, rows=SUBLANES):
    ch, k = a.shape
    a = jnp.pad(a, ((0, 0), (0, rows - k)))
    return jnp.broadcast_to(a[:, :, None], (ch, rows, LANES))


def _out_even_kernel(x_ref, yat_ref, of_ref, ob_ref, zg_ref, nw_ref, wa_ref, wb_ref, g_ref, b_ref,
                     w1_ref, w2_ref, g2_ref, b2_ref, o_ref):
    mix = _mm(yat_ref[...], wa_ref[...], _TN)
    nw = nw_ref[...]
    for h in range(GDN_HEADS):
        sl = slice(h * GDN_DK, (h + 1) * GDN_DK)
        o = of_ref[:, sl] + ob_ref[:, sl]
        y = o * lax.rsqrt(jnp.mean(o * o, -1, keepdims=True) + RMS_EPS) * nw * _silu(zg_ref[:, sl])
        mix = mix + _mm(y, wb_ref[sl, :])
    x1 = _layer_norm(DEEPNORM_ALPHA * x_ref[...] + mix, g_ref[...], b_ref[...])
    o_ref[...] = _mlp_tail(x1, w1_ref, w2_ref, g2_ref, b2_ref)


def _out_even(x2, yat, of, ob, projm, norm_w, wa, wb, g, b, mlp, tm=512):
    n, d = x2.shape
    tm = min(tm, n)
    row = lambda w: pl.BlockSpec((tm, w), lambda i: (i, 0))
    consts = (norm_w, wa, wb, g, b) + tuple(mlp)
    return pl.pallas_call(
        _out_even_kernel, grid=(n // tm,),
        in_specs=[row(d), pl.BlockSpec((HY_DIM, tm), lambda i: (0, i)), row(512), row(512),
                  pl.BlockSpec((tm, 512), lambda i: (i, 3))] + [_resident(a) for a in consts],
        out_specs=row(d), out_shape=jax.ShapeDtypeStruct((n, d), F32),
        compiler_params=_params("parallel"), name="out_even",
    )(x2, yat, of, ob, projm, *consts)


def _even_layer(x2, bsz, t_len, w_in, hy_conv_w, hy_conv_b, f_w1, f_b1, f_w2, f_b2, f_w3, f_b3,
                f_w4, f_freq, hy_skip, gdn_conv_w, gdn_a_log, gdn_dt_bias, gdn_norm_w, w_out,
                ln_g, ln_b, mlp):
    d = x2.shape[1]
    hyw = 3 * HY_DIM
    wm = jnp.concatenate([w_in[:, hyw:], jnp.zeros((d, LANES - 12), F32)], axis=1).astype(BF16)
    wt = w_in[:, :hyw].T.astype(BF16)
    projm, projt = _project(x2, wm, wt)
    consts = _fft_consts(t_len)
    w1p = jnp.pad(f_w1, ((0, HY_FILT_HID - HY_EMB), (0, 0)))
    filt = _hy_filters(_hy_positions(t_len), w1p, f_b1[None], f_w2, f_b2[None], f_w3, f_b3[None],
                       f_freq[None], f_w4.T, _hy_deltas(), t_len)
    r = t_len // LANES
    hr, hi = _hy_spectra(filt.reshape(4 * HY_DIM, r, LANES), consts)
    pp = _lane_rep(jnp.concatenate([hy_conv_w.T, hy_conv_b[:, None]], axis=1))
    ps = _lane_rep(hy_skip.T)
    yat = _hy_conv(projt.reshape(hyw, bsz, r, LANES), pp, ps, hr, hi, consts)
    yat = yat.reshape(HY_DIM, bsz * t_len)
    alog = jnp.zeros((1, LANES), F32).at[0, :8].set(gdn_a_log.reshape(-1))
    dtb = jnp.zeros((1, LANES), F32).at[0, :8].set(gdn_dt_bias.reshape(-1))
    qn, kn, vn, gates = _gdn_pre(projm, gdn_conv_w, alog, dtb, bsz, t_len)
    of, ob = _gdn_scan(qn, kn, vn, gates, bsz, t_len)
    return _out_even(x2, yat, of, ob, projm, gdn_norm_w[None], w_out[:HY_DIM].astype(BF16),
                     w_out[HY_DIM:].astype(BF16), ln_g[None], ln_b[None], mlp)


def _hg_scan_kernel(qf_ref, ff_ref, vf_ref, qb_ref, fb_ref, vb_ref, low_ref, of_ref, ob_ref,
                    state, q_s, k_s, g_s, *, layer):
    n = pl.program_id(1)

    @pl.when(n == 0)
    def _():
        state[...] = jnp.zeros_like(state)

    tb = qf_ref.shape[0]
    c = HG_CHUNK
    nsub = tb // c
    low = low_ref[...]
    e = jnp.exp(low - jnp.max(low, axis=0, keepdims=True))
    p = e / jnp.sum(e, axis=0, keepdims=True)
    lb = jnp.sum(p[1:layer + 1], axis=0, keepdims=True)
    rb = lax.broadcasted_iota(jnp.int32, (tb, tb), 0)
    cb = lax.broadcasted_iota(jnp.int32, (tb, tb), 1)
    same = (rb // c) == (cb // c)
    hb = c // 2
    rowh = lax.broadcasted_iota(jnp.int32, (hb, 1), 0)
    dirs = ((qf_ref, ff_ref, vf_ref, of_ref), (qb_ref, fb_ref, vb_ref, ob_ref))
    for d, (q_ref, f_ref, v_ref, o_ref) in enumerate(dirs):
        fg = lb + (1.0 - lb) * _sigmoid(f_ref[...])
        tri = (same & ((cb <= rb) if d == 0 else (cb >= rb))).astype(BF16)
        g_s[...] = _sel_mm(tri, jnp.log(fg))
        k_s[...] = 1.0 - fg
        q_s[...] = _silu(q_ref[...])

        def body(s, carry):
            def sub_rows(u):
                idx = s * HG_SUBS + u
                sub = idx if d == 0 else nsub - 1 - idx
                return pl.ds(pl.multiple_of(sub * c, c), c)

            def pairwise(qb, gkb, gb, vb):
                acc = None
                for j in range(hb):
                    dec = jnp.exp(gb - gkb[j:j + 1, :])
                    a_col = jnp.sum(qb * dec, axis=1, keepdims=True)
                    a_col = jnp.where((rowh >= j) if d == 0 else (rowh <= j), a_col, 0.0)
                    term = a_col * vb[j:j + 1, :]
                    acc = term if acc is None else acc + term
                return acc

            def chain(h):
                sl = slice(h * HG_DK, (h + 1) * HG_DK)
                st = state[d, h]
                pending = []
                for u in range(HG_SUBS):
                    rows = sub_rows(u)
                    qc, kc, gc, vc = q_s[rows, sl], k_s[rows, sl], g_s[rows, sl], v_ref[rows, sl]
                    g_last = gc[c - 1:c, :] if d == 0 else gc[0:1, :]
                    inter = _mm(qc * jnp.exp(gc), st, _NT)
                    upd = _mm(vc, kc * jnp.exp(g_last - gc), _TN)
                    yield
                    st = st * jnp.exp(g_last) + upd
                    pending.append((rows, qc, kc, gc, vc, inter))
                state[d, h] = st
                lo, hi = slice(0, hb), slice(hb, c)
                src, dst, bnd = (lo, hi, hb - 1) if d == 0 else (hi, lo, hb)
                for rows, qc, kc, gc, vc, inter in pending:
                    g_bnd = gc[bnd:bnd + 1, :]
                    q_x = qc[dst] * jnp.exp(gc[dst] - g_bnd)
                    k_y = kc[src] * jnp.exp(g_bnd - gc[src])
                    v_src = vc[src]
                    cross = None
                    for j in range(hb):
                        term = jnp.sum(q_x * k_y[j:j + 1, :], axis=1, keepdims=True) * v_src[j:j + 1, :]
                        cross = term if cross is None else cross + term
                    gk = gc - jnp.log(kc)
                    o_lo = pairwise(qc[lo], gk[lo], gc[lo], vc[lo])
                    o_hi = pairwise(qc[hi], gk[hi], gc[hi], vc[hi])
                    if d == 0:
                        o_hi = o_hi + cross
                    else:
                        o_lo = o_lo + cross
                    o_ref[rows, sl] = inter + jnp.concatenate([o_lo, o_hi], axis=0)

            _interleave([chain(h) for h in range(HG_HEADS)])
            return carry

        lax.fori_loop(0, nsub // HG_SUBS, body, 0)


def _hg_scan(proj, hg_lower, layer, bsz, t_len, tb=128):
    n = proj.shape[0]
    nb = t_len // tb

    def spec(colb, rev):
        if rev:
            return pl.BlockSpec((tb, 512), lambda b, i: (b * nb + nb - 1 - i, colb))
        return pl.BlockSpec((tb, 512), lambda b, i: (b * nb + i, colb))

    return pl.pallas_call(
        functools.partial(_hg_scan_kernel, layer=layer), grid=(bsz, nb),
        in_specs=[spec(0, False), spec(1, False), spec(3, False),
                  spec(0, True), spec(2, True), spec(3, True),
                  pl.BlockSpec(hg_lower.shape, lambda b, i: (0, 0))],
        out_specs=[spec(0, False), spec(0, True)],
        out_shape=[jax.ShapeDtypeStruct((n, 512), F32)] * 2,
        scratch_shapes=[pltpu.VMEM((2, HG_HEADS, HG_DK, HG_DK), F32)]
        + [pltpu.VMEM((tb, 512), F32)] * 3,
        compiler_params=_params("parallel", "arbitrary"), name="hg_scan",
    )(proj, proj, proj, proj, proj, proj, hg_lower)


def _rw_pre_kernel(rc, rp, rn, kc, kp, kn, vc, vp, vn, lc, lp, ln, mu_ref, w0_ref, w2_ref,
                   a0_ref, a2_ref, g2_ref, kk_ref, ka_ref, bd_ref,
                   r_o, k_o, v_o, kk_o, kka_o, lwf_o, lwb_o, g_o, scr):
    i = pl.program_id(1)
    first = i == 0
    last = i == pl.num_programs(1) - 1
    tb = rc.shape[0]

    def shifted(cur_ref, prev_ref, next_ref, col0):
        w = cur_ref.shape[1]
        cur = cur_ref[...]
        scr[pl.ds(SUBLANES, tb), pl.ds(0, w)] = cur
        scr[pl.ds(0, SUBLANES), pl.ds(0, w)] = jnp.where(first, 0.0, prev_ref[...])
        scr[pl.ds(SUBLANES + tb, SUBLANES), pl.ds(0, w)] = jnp.where(last, 0.0, next_ref[...])
        prev = scr[pl.ds(SUBLANES - 1, tb), pl.ds(0, w)]
        nxt = scr[pl.ds(SUBLANES + 1, tb), pl.ds(0, w)]
        mu = mu_ref[:, col0:col0 + w]
        return cur + mu[0:1] * (prev - cur) + mu[1:2] * (nxt - cur)

    r = shifted(rc, rp, rn, 0)
    k = shifted(kc, kp, kn, RW_DIM)
    v = shifted(vc, vp, vn, 2 * RW_DIM)
    lo = shifted(lc, lp, ln, 3 * RW_DIM)
    lo_wa, lo_g = lo[:, :LANES], lo[:, LANES:]
    th = jnp.tanh(lo_wa)
    for d, o_ref in enumerate((lwf_o, lwb_o)):
        w_raw = w0_ref[d:d + 1, :] + _mm(th, w2_ref[d])
        o_ref[...] = -math.exp(-0.5) * _sigmoid(w_raw)
    a = _sigmoid(a0_ref[...] + _mm(lo_wa, a2_ref[...]))
    g_o[...] = _mm(_sigmoid(lo_g), g2_ref[...]).astype(g_o.dtype)
    kx = k * kk_ref[...]
    kk = kx * lax.rsqrt(_mm(kx * kx, bd_ref[...]) + L2_EPS)
    r_o[...] = r.astype(r_o.dtype)
    v_o[...] = v.astype(v_o.dtype)
    k_o[...] = (k * (1.0 + (a - 1.0) * ka_ref[...])).astype(k_o.dtype)
    kk_o[...] = kk.astype(kk_o.dtype)
    kka_o[...] = (kk * a).astype(kka_o.dtype)


def _rw_pre(proj, mu, w0, w2p, a0, a2p, g2, k_k, k_a, bd, bsz, t_len, tb=256):
    n = proj.shape[0]
    nb = t_len // tb
    base = 2560
    specs = []
    for off, w in ((0, 512), (512, 512), (1024, 512), (1536, 256)):
        specs += _halo_specs(w, (base + off) // w, tb, t_len, bsz)
    full = lambda a: pl.BlockSpec(a.shape, lambda b, i: (0,) * a.ndim)
    out_spec = pl.BlockSpec((tb, RW_DIM), lambda b, i: (b * nb + i, 0))
    consts = (mu, w0, w2p, a0, a2p, g2, k_k, k_a, bd)
    return pl.pallas_call(
        _rw_pre_kernel, grid=(bsz, nb),
        in_specs=specs + [full(a) for a in consts],
        out_specs=[out_spec] * 8,
        out_shape=[jax.ShapeDtypeStruct((n, RW_DIM), dt) for dt in (BF16,) * 5 + (F32, F32, BF16)],
        scratch_shapes=[pltpu.VMEM((tb + 2 * SUBLANES, RW_DIM), F32)],
        compiler_params=_params("parallel", "parallel"), name="rw_pre",
    )(*([proj] * 12), *consts)


def _rw_scan_kernel(rf, kf, vf, bf, af, wf, rb, kb, vb, bb, ab, wb, yf, yb, state):
    n = pl.program_id(1)

    @pl.when(n == 0)
    def _():
        state[...] = jnp.zeros_like(state)

    c = RW_CHUNK
    gw = RW_GROUP * RW_HD
    gc = RW_GROUP * c
    row = lax.broadcasted_iota(jnp.int32, (c, c), 0)
    col = lax.broadcasted_iota(jnp.int32, (c, c), 1)
    rr = lax.broadcasted_iota(jnp.int32, (gc, gc), 0)
    cc = lax.broadcasted_iota(jnp.int32, (gc, gc), 1)
    same = (rr // c) == (cc // c)
    head_lanes = (lax.broadcasted_iota(jnp.int32, (gc, gw), 0) // c
                  == lax.broadcasted_iota(jnp.int32, (gc, gw), 1) // RW_HD)
    rs = lax.broadcasted_iota(jnp.int32, (gw, gw), 0)
    cs = lax.broadcasted_iota(jnp.int32, (gw, gw), 1)
    head_blocks = ((rs // RW_HD) == (cs // RW_HD)).astype(F32)
    row_cat = lax.broadcasted_iota(jnp.int32, (c, gc), 0)
    col_cat = lax.broadcasted_iota(jnp.int32, (c, gc), 1) % c
    eye_cat = (col_cat == row_cat).astype(F32)

    def stack(x):
        return jnp.where(head_lanes, jnp.concatenate([x] * RW_GROUP, axis=0), 0.0)

    dirs = ((rf, kf, vf, bf, af, wf, yf), (rb, kb, vb, bb, ab, wb, yb))
    def chain(ib, d, p):
        r_ref, k_ref, v_ref, b_ref, a_ref, w_ref, y_ref = dirs[d]
        sl = slice(p * gw, (p + 1) * gw)
        incl = (col_cat <= row_cat) if d == 0 else (col_cat >= row_cat)
        strict = (col_cat < row_cat) if d == 0 else (col_cat > row_cat)
        lw = w_ref[ib, :, sl]
        tri = ((col <= row) if d == 0 else (col >= row)).astype(BF16)
        g_in = _sel_mm(tri, lw)
        yield
        g_ex = g_in - lw
        g_end = g_in[c - 1:c, :] if d == 0 else g_in[0:1, :]
        e_neg = jnp.exp(-g_in)
        e_end = jnp.exp(g_end - g_in)
        k, a, v = k_ref[ib, :, sl], a_ref[ib, :, sl], v_ref[ib, :, sl]
        lhs = jnp.concatenate([r_ref[ib, :, sl] * jnp.exp(g_in), b_ref[ib, :, sl] * jnp.exp(g_ex)],
                              axis=0)
        ak = _mm(lhs, stack(k * e_neg), _NT)
        yield
        aa = _mm(lhs, stack(a * e_neg), _NT)
        yield
        a_rk = jnp.where(incl, ak[:c], 0.0)
        a_ra = jnp.where(incl, aa[:c], 0.0)
        a_bk = jnp.where(strict, ak[c:], 0.0)
        a_ba = jnp.where(strict, aa[c:], 0.0)
        t_inv = yield from _unit_tri_inv_cat(a_ba, eye_cat, same, c)
        yield
        m = state[ib, d, p]
        xm = _mm(lhs, m, _NT)
        yield
        av = _mm(jnp.concatenate([a_rk, a_bk], axis=0), stack(v))
        yield
        u = _mm(t_inv, stack(xm[c:] + av[c:]))
        yield
        y_ref[ib, :, sl] = xm[:c] + av[:c] - _mm(a_ra, stack(u))
        yield
        upd = _mm(jnp.concatenate([v.astype(F32), u], axis=0),
                  jnp.concatenate([k * e_end, -(a * e_end)], axis=0), _TN)
        state[ib, d, p] = m * jnp.exp(g_end) + upd * head_blocks

    _interleave([chain(ib, d, p) for ib in range(rf.shape[0]) for d in range(2)
                 for p in range(RW_DIM // gw)])


def _rw_scan(r, k, v, kk, kka, lwf, lwb, bsz, t_len, bb=SCAN_BATCH):
    n = r.shape[0]
    c = RW_CHUNK
    gw = RW_GROUP * RW_HD
    nc = t_len // c
    bb = math.gcd(bb, bsz)
    r, k, v, kk, kka, lwf, lwb =(a.reshape(bsz, t_len, RW_DIM) for a in (r, k, v, kk, kka, lwf, lwb))
    fwd = pl.BlockSpec((bb, c, RW_DIM), lambda g, i: (g, i, 0))
    bwd = pl.BlockSpec((bb, c, RW_DIM), lambda g, i: (g, nc - 1 - i, 0))
    yf, yb = pl.pallas_call(
        _rw_scan_kernel, grid=(bsz // bb, nc),
        in_specs=[fwd] * 6 + [bwd] * 6, out_specs=[fwd, bwd],
        out_shape=[jax.ShapeDtypeStruct((bsz, t_len, RW_DIM), F32)] * 2,
        scratch_shapes=[pltpu.VMEM((bb, 2, RW_DIM // gw, gw, gw), F32)],
        compiler_params=_params("parallel", "arbitrary"), name="rw_scan",
    )(r, k, v, kk, kka, lwf, r, k, v, kk, kka, lwb)
    return yf.reshape(n, RW_DIM), yb.reshape(n, RW_DIM)


def _out_odd_kernel(x_ref, hf_ref, hb_ref, hg_ref, hnw_ref, yf_ref, yb_ref, r_ref, k_ref, v_ref,
                    gate_ref, rk_ref, lw_ref, lb_ref, bd_ref, wc_ref, wd_ref, g_ref, b_ref,
                    w1_ref, w2_ref, g2_ref, b2_ref, o_ref):
    nw = hnw_ref[...]
    mix = None
    for h in range(HG_HEADS):
        sl = slice(h * HG_DK, (h + 1) * HG_DK)
        o = hf_ref[:, sl] + hb_ref[:, sl]
        y = o * lax.rsqrt(jnp.mean(o * o, -1, keepdims=True) + RMS_EPS) * nw * _silu(hg_ref[:, sl])
        part = _mm(y, wc_ref[sl, :])
        mix = part if mix is None else mix + part
    bd = bd_ref[...]
    y = yf_ref[...] + yb_ref[...]
    yc = y - _sel_mm(bd, y, sel_first=False) * (1.0 / RW_HD)
    var = _mm(yc * yc, bd) * (1.0 / RW_HD)
    yn = yc * lax.rsqrt(var + RW_GN_EPS) * lw_ref[...] + lb_ref[...]
    rk = r_ref[...].astype(F32) * k_ref[...] * rk_ref[...]
    bonus = _mm(rk, bd) * v_ref[...]
    mix = mix + _mm((yn + bonus) * gate_ref[...], wd_ref[...])
    x1 = _layer_norm(DEEPNORM_ALPHA * x_ref[...] + mix, g_ref[...], b_ref[...])
    o_ref[...] = _mlp_tail(x1, w1_ref, w2_ref, g2_ref, b2_ref)


def _out_odd(x2, hf, hb, proj, hnw, yf, yb, r, k, v, gate, r_k, lnx_w, lnx_b, bd, wc, wd, g, b,
             mlp, tm=512):
    n, d = x2.shape
    tm = min(tm, n)
    row = lambda w: pl.BlockSpec((tm, w), lambda i: (i, 0))
    consts = (r_k, lnx_w, lnx_b, bd, wc, wd, g, b) + tuple(mlp)
    return pl.pallas_call(
        _out_odd_kernel, grid=(n // tm,),
        in_specs=[row(d), row(512), row(512), pl.BlockSpec((tm, 512), lambda i: (i, 4)),
                  _resident(hnw)] + [row(512)] * 6 + [_resident(a) for a in consts],
        out_specs=row(d), out_shape=jax.ShapeDtypeStruct((n, d), F32),
        compiler_params=_params("parallel"), name="out_odd",
    )(x2, hf, hb, proj, hnw, yf, yb, r, k, v, gate, *consts)


def _odd_layer(x2, bsz, t_len, layer, w_in, hg_lower, hg_norm_w, rw_mu, rw_w0, rw_w2, rw_a0, rw_a2,
               rw_g2, rw_k_k, rw_k_a, rw_r_k, rw_lnx_w, rw_lnx_b, w_out, ln_g, ln_b, mlp):
    proj = _project(x2, w_in.astype(BF16))
    hf, hb = _hg_scan(proj, hg_lower, layer, bsz, t_len)
    idx = np.arange(RW_DIM) // RW_HD
    bd = jnp.asarray(idx[:, None] == idx[None, :], BF16)
    lora = rw_w2.shape[1]
    w2p = jnp.pad(rw_w2, ((0, 0), (0, LANES - lora), (0, 0)))
    a2p = jnp.pad(rw_a2, ((LANES - rw_a2.shape[0], 0), (0, 0)))
    r, k, v, kk, kka, lwf, lwb, gate = _rw_pre(proj, rw_mu, rw_w0, w2p, rw_a0[None], a2p, rw_g2,
                                               rw_k_k[None], rw_k_a[None], bd, bsz, t_len)
    yf, yb = _rw_scan(r, k, v, kk, kka, lwf, lwb, bsz, t_len)
    return _out_odd(x2, hf, hb, proj, hg_norm_w[None], yf, yb, r, k, v, gate,
                    rw_r_k.reshape(1, RW_DIM), rw_lnx_w[None], rw_lnx_b[None], bd,
                    w_out[:512].astype(BF16), w_out[512:].astype(BF16), ln_g[None], ln_b[None], mlp)


def kernel(x, ev_w_in, hy_conv_w, hy_conv_b, hy_filt_w1, hy_filt_b1, hy_filt_w2, hy_filt_b2, hy_filt_w3, hy_filt_b3, hy_filt_w4, hy_filt_freq, hy_skip, gdn_conv_w, gdn_a_log, gdn_dt_bias, gdn_norm_w, ev_w_out, od_w_in, hg_lower, hg_norm_w, rw_mu, rw_w0, rw_w2, rw_a0, rw_a2, rw_g2, rw_k_k, rw_k_a, rw_r_k, rw_lnx_w, rw_lnx_b, od_w_out, ln_g, ln_b, mlp_w1, mlp_w2):
    bsz, t_len, d = x.shape
    x2 = x.reshape(bsz * t_len, d)
    for layer in range(DEPTH):
        i = layer // 2
        mlp = (mlp_w1[layer].astype(BF16), mlp_w2[layer].astype(BF16), ln_g[layer, 1][None],
               ln_b[layer, 1][None])
        if layer % 2 == 0:
            x2 = _even_layer(x2, bsz, t_len, ev_w_in[i], hy_conv_w[i], hy_conv_b[i], hy_filt_w1[i],
                             hy_filt_b1[i], hy_filt_w2[i], hy_filt_b2[i], hy_filt_w3[i],
                             hy_filt_b3[i], hy_filt_w4[i], hy_filt_freq[i], hy_skip[i],
                             gdn_conv_w[i], gdn_a_log[i], gdn_dt_bias[i], gdn_norm_w[i],
                             ev_w_out[i], ln_g[layer, 0], ln_b[layer, 0], mlp)
        else:
            x2 = _odd_layer(x2, bsz, t_len, layer, od_w_in[i], hg_lower, hg_norm_w[i], rw_mu[i],
                            rw_w0[i], rw_w2[i], rw_a0[i], rw_a2[i], rw_g2[i], rw_k_k[i], rw_k_a[i],
                            rw_r_k[i], rw_lnx_w[i], rw_lnx_b[i], od_w_out[i], ln_g[layer, 0],
                            ln_b[layer, 0], mlp)
    return x2.reshape(bsz, t_len, d)
```

```python
import functools
import math

import numpy as np
import jax
import jax.numpy as jnp
from jax import lax
from jax.experimental import pallas as pl
from jax.experimental.pallas import tpu as pltpu

F32 = jnp.float32
BF16 = jnp.bfloat16

D_MODEL = 1024
DEPTH = 2
D_FF = 4 * D_MODEL
DEEPNORM_ALPHA = (2.0 * DEPTH) ** 0.25
LN_EPS = 1e-5
RMS_EPS = 1e-6
L2_EPS = 1e-6

HY_DIM = 512
HY_EMB = 33
HY_BANDS = 16
HY_FILT_HID = 64
HY_DECAY_TARGET = 1e-2
HY_FAST_DECAY_PCT = 0.3
HY_SLOW_DECAY_PCT = 1.5

GDN_HEADS = 4
GDN_DK = 128
GDN_CHUNK = 64

HG_HEADS = 4
HG_DK = 128
HG_CHUNK = 16
HG_SUBS = 4

RW_HEADS = 8
RW_HD = 64
RW_DIM = 512
RW_CHUNK = 64
RW_GROUP = 4
RW_GN_EPS = 64e-5

SCAN_BATCH = 4

LANES = 128
SUBLANES = 8
VMEM_LIMIT = 56 * 1024 * 1024

_NN = (((1,), (0,)), ((), ()))
_NT = (((1,), (1,)), ((), ()))
_TN = (((0,), (0,)), ((), ()))


def _mm(a, b, dims=_NN):
    return lax.dot_general(a.astype(BF16), b.astype(BF16), dims, preferred_element_type=F32)


def _params(*sem):
    return pltpu.CompilerParams(dimension_semantics=sem, vmem_limit_bytes=VMEM_LIMIT)


def _sigmoid(x):
    return 1.0 / (1.0 + jnp.exp(-x))


def _silu(x):
    return x * _sigmoid(x)


def _softplus(x):
    return jnp.maximum(x, 0.0) + jnp.log(1.0 + jnp.exp(-jnp.abs(x)))


def _layer_norm(v, g, b):
    mean = jnp.mean(v, axis=-1, keepdims=True)
    vc = v - mean
    var = jnp.mean(vc * vc, axis=-1, keepdims=True)
    return vc * lax.rsqrt(var + LN_EPS) * g + b


def _col(x, idx, lane):
    return jnp.sum(jnp.where(lane == idx, x, 0.0), axis=1, keepdims=True)


def _split2(a):
    hi = a.astype(BF16)
    return hi, (a - hi.astype(F32)).astype(BF16)


def _mm3(a, b, dims=_NN):
    ah, al = _split2(a)
    bh, bl = _split2(b)
    dot = functools.partial(lax.dot_general, dimension_numbers=dims, preferred_element_type=F32)
    return dot(ah, bh) + dot(ah, bl) + dot(al, bh)


def _sel_mm(sel, x, dims=_NN, sel_first=True):
    hi = x.astype(BF16)
    r1 = x - hi.astype(F32)
    mid = r1.astype(BF16)
    lo = (r1 - mid.astype(F32)).astype(BF16)
    dot = functools.partial(lax.dot_general, dimension_numbers=dims, preferred_element_type=F32)
    if sel_first:
        return dot(sel, hi) + dot(sel, mid) + dot(sel, lo)
    return dot(hi, sel) + dot(mid, sel) + dot(lo, sel)


def _expand_bd(x_cat, same):
    g = x_cat.shape[1] // x_cat.shape[0]
    return jnp.where(same, jnp.concatenate([x_cat] * g, axis=0), 0.0)


def _unit_tri_inv_cat(a_cat, eye_cat, same, depth):
    c = a_cat.shape[0]
    p = -a_cat
    t = eye_cat + p
    p = _mm(p, _expand_bd(p, same))
    yield
    k = 4
    while k < depth:
        both = _mm(jnp.concatenate([p, t], axis=0), _expand_bd(p, same))
        yield
        p, t = both[:c], t + both[c:]
        k *= 2
    return t + _mm(t, _expand_bd(p, same))


def _interleave(chains):
    active = list(chains)
    while active:
        still = []
        for g in active:
            try:
                next(g)
                still.append(g)
            except StopIteration:
                pass
        active = still


def _proj_kernel(x_ref, w_ref, o_ref):
    o_ref[...] = _mm(x_ref[...], w_ref[...])


def _proj_t_kernel(x_ref, w_ref, wt_ref, o_ref, ot_ref):
    xb = x_ref[...].astype(BF16)
    o_ref[...] = _mm(xb, w_ref[...])
    ot_ref[...] = _mm(wt_ref[...], xb, _NT).astype(ot_ref.dtype)


def _project(x2, w, wt=None, tm=512):
    n, d = x2.shape
    m = w.shape[1]
    grid = (n // tm,)
    once = pl.Buffered(1)
    x_spec = pl.BlockSpec((tm, d), lambda i: (i, 0))
    w_spec = pl.BlockSpec((d, m), lambda i: (0, 0), pipeline_mode=once)
    o_spec = pl.BlockSpec((tm, m), lambda i: (i, 0))
    if wt is None:
        return pl.pallas_call(
            _proj_kernel, grid=grid, in_specs=[x_spec, w_spec], out_specs=o_spec,
            out_shape=jax.ShapeDtypeStruct((n, m), F32), compiler_params=_params("parallel"),
            name="proj")(x2, w)
    mt = wt.shape[0]
    return pl.pallas_call(
        _proj_t_kernel, grid=grid,
        in_specs=[x_spec, w_spec, pl.BlockSpec((mt, d), lambda i: (0, 0), pipeline_mode=once)],
        out_specs=[o_spec, pl.BlockSpec((mt, tm), lambda i: (0, i))],
        out_shape=[jax.ShapeDtypeStruct((n, m), F32), jax.ShapeDtypeStruct((mt, n), BF16)],
        compiler_params=_params("parallel"), name="proj_t")(x2, w, wt)


def _mlp_tail(x, w1_ref, w2_ref, g_ref, b_ref):
    h = jnp.maximum(_mm(x, w1_ref[...]), 0.0)
    return _layer_norm(DEEPNORM_ALPHA * x + _mm(h * h, w2_ref[...]), g_ref[...], b_ref[...])


def _resident(a):
    return pl.BlockSpec(a.shape, lambda i: (0,) * a.ndim, pipeline_mode=pl.Buffered(1))


def _fill_halo(scr, cur, prev8, next8, first, last, halo):
    tb = cur.shape[0]
    scr[pl.ds(SUBLANES, tb), :] = cur
    scr[pl.ds(0, SUBLANES), :] = jnp.where(first, 0.0, prev8)
    scr[pl.ds(SUBLANES + tb, SUBLANES), :] = jnp.where(last, 0.0, next8)


def _halo_specs(width, col_block, tb, t_len, bsz):
    nb = t_len // tb
    r8 = tb // SUBLANES
    tot8 = t_len // SUBLANES

    def cur(b, i):
        return (b * nb + i, col_block)

    def prev(b, i):
        return (jnp.maximum(b * tot8 + i * r8 - 1, 0), col_block)

    def nxt(b, i):
        return (jnp.minimum(b * tot8 + (i + 1) * r8, tot8 * bsz - 1), col_block)

    return [pl.BlockSpec((tb, width), cur), pl.BlockSpec((SUBLANES, width), prev),
            pl.BlockSpec((SUBLANES, width), nxt)]


def _gdn_pre_kernel(cur_ref, prev_ref, next_ref, ab_ref, cw_ref, alog_ref, dtb_ref,
                    q_ref, k_ref, v_ref, g_ref, scr):
    i = pl.program_id(1)
    tb = cur_ref.shape[0]
    _fill_halo(scr, cur_ref[...], prev_ref[...], next_ref[...], i == 0,
               i == pl.num_programs(1) - 1, 2)
    width = cw_ref.shape[0]
    acc = None
    for j in range(width):
        term = scr[pl.ds(SUBLANES - width // 2 + j, tb), :] * cw_ref[pl.ds(j, 1), :]
        acc = term if acc is None else acc + term
    qkv = _silu(acc)
    hd = GDN_HEADS * GDN_DK
    for h in range(GDN_HEADS):
        sl = slice(h * GDN_DK, (h + 1) * GDN_DK)
        q = qkv[:, sl]
        q_ref[:, sl] = (q * lax.rsqrt(jnp.sum(q * q, -1, keepdims=True) + L2_EPS)
                        * GDN_DK ** -0.5).astype(q_ref.dtype)
        k = qkv[:, hd + h * GDN_DK: hd + (h + 1) * GDN_DK]
        k_ref[:, sl] = (k * lax.rsqrt(jnp.sum(k * k, -1, keepdims=True) + L2_EPS)).astype(k_ref.dtype)
    v_ref[...] = qkv[:, 2 * hd:].astype(v_ref.dtype)
    ab = ab_ref[...]
    lane = lax.broadcasted_iota(jnp.int32, ab.shape, 1)
    log_g = -jnp.exp(alog_ref[...]) * _softplus(ab + dtb_ref[...])
    g_ref[...] = jnp.where(lane < 2 * GDN_HEADS, log_g, _sigmoid(ab))


def _gdn_pre(proj, conv_w, a_log_row, dt_bias_row, bsz, t_len, tb=256):
    n = proj.shape[0]
    wq = 3 * GDN_HEADS * GDN_DK
    nb = t_len // tb
    specs = _halo_specs(wq, 0, tb, t_len, bsz)
    ab_col = (wq + GDN_HEADS * GDN_DK) // LANES
    out_spec = pl.BlockSpec((tb, 512), lambda b, i: (b * nb + i, 0))
    return pl.pallas_call(
        _gdn_pre_kernel, grid=(bsz, nb),
        in_specs=specs + [pl.BlockSpec((tb, LANES), lambda b, i: (b * nb + i, ab_col)),
                          pl.BlockSpec(conv_w.shape, lambda b, i: (0, 0)),
                          pl.BlockSpec((1, LANES), lambda b, i: (0, 0)),
                          pl.BlockSpec((1, LANES), lambda b, i: (0, 0))],
        out_specs=[out_spec, out_spec, out_spec,
                   pl.BlockSpec((tb, LANES), lambda b, i: (b * nb + i, 0))],
        out_shape=[jax.ShapeDtypeStruct((n, 512), BF16)] * 3 + [jax.ShapeDtypeStruct((n, LANES), F32)],
        scratch_shapes=[pltpu.VMEM((tb + 2 * SUBLANES, wq), F32)],
        compiler_params=_params("parallel", "parallel"), name="gdn_pre",
    )(proj, proj, proj, proj, conv_w, a_log_row, dt_bias_row)


def _gdn_scan_kernel(qf, kf, vf, gf, qb, kb, vb, gb, of, ob, state):
    n = pl.program_id(1)

    @pl.when(n == 0)
    def _():
        state[...] = jnp.zeros_like(state)

    c = GDN_CHUNK
    nh = GDN_HEADS
    hc = nh * c
    wide = nh * GDN_DK
    row = lax.broadcasted_iota(jnp.int32, (c, c), 0)
    col = lax.broadcasted_iota(jnp.int32, (c, c), 1)
    lane = lax.broadcasted_iota(jnp.int32, (c, LANES), 1)
    rr = lax.broadcasted_iota(jnp.int32, (hc, hc), 0)
    cc = lax.broadcasted_iota(jnp.int32, (hc, hc), 1)
    same = (rr // c) == (cc // c)
    head_lanes = (lax.broadcasted_iota(jnp.int32, (hc, wide), 0) // c
                  == lax.broadcasted_iota(jnp.int32, (hc, wide), 1) // GDN_DK)
    lane_blk = lax.broadcasted_iota(jnp.int32, (1, hc), 1) // c
    row_cat = lax.broadcasted_iota(jnp.int32, (c, hc), 0)
    lane_cat = lax.broadcasted_iota(jnp.int32, (c, hc), 1)
    col_cat = lane_cat % c
    blk_cat = lane_cat // c
    eye_cat = (col_cat == row_cat).astype(F32)

    def col_cat_of(x, first):
        out = jnp.zeros((c, hc), F32)
        for h in range(nh):
            out = jnp.where(blk_cat == h, _col(x, first + h, lane), out)
        return out

    def stack_wide(x):
        return jnp.where(head_lanes, jnp.concatenate([x] * nh, axis=0), 0.0)

    def stack_narrow(x):
        return jnp.concatenate([x[:, h * GDN_DK:(h + 1) * GDN_DK] for h in range(nh)], axis=0)

    dirs = ((qf, kf, vf, gf, of), (qb, kb, vb, gb, ob))

    def chain(bb, d):
        q_ref, k_ref, v_ref, g_ref, o_ref = dirs[d]
        incl1 = (col <= row) if d == 0 else (col >= row)
        incl = (col_cat <= row_cat) if d == 0 else (col_cat >= row_cat)
        strict = (col_cat < row_cat) if d == 0 else (col_cat > row_cat)
        gates = g_ref[bb]
        tri = incl1.astype(BF16)
        gam_all = _sel_mm(tri, gates)
        tri4 = jnp.concatenate([tri] * nh, axis=0)
        gam_rows = _sel_mm(tri4, gates.T[:2 * nh], _NT, sel_first=False)
        yield
        gam = jnp.concatenate([_col(gam_all, d * nh + h, lane) for h in range(nh)], axis=0)
        beta = jnp.concatenate([_col(gates, 2 * nh + h, lane) for h in range(nh)], axis=0)
        gam_row = jnp.zeros((1, hc), F32)
        for h in range(nh):
            gam_row = jnp.where(lane_blk == h, gam_rows[d * nh + h:d * nh + h + 1, :], gam_row)
        last = c - 1 if d == 0 else 0
        g_last = jnp.concatenate(
            [jnp.broadcast_to(gam[h * c + last:h * c + last + 1, :], (c, 1)) for h in range(nh)], axis=0)
        g_last_s = jnp.concatenate(
            [jnp.broadcast_to(gam[h * c + last:h * c + last + 1, :], (GDN_DK, 1)) for h in range(nh)],
            axis=0)
        diff = col_cat_of(gam_all, d * nh) - gam_row
        decay = jnp.where(incl, jnp.exp(jnp.where(incl, diff, 0.0)), 0.0)
        e_gam = jnp.exp(gam)
        k_in, q_in = k_ref[bb], q_ref[bb]
        k_w = stack_wide(k_in)
        kq = _mm(jnp.concatenate([k_in, q_in], axis=0), k_w, _NT)
        yield
        a_cat = jnp.where(strict, kq[:c] * decay * col_cat_of(gates, 2 * nh), 0.0)
        attn = _expand_bd(kq[c:] * decay, same)
        t_cat = yield from _unit_tri_inv_cat(a_cat, eye_cat, same, c)
        yield
        rhs = jnp.concatenate([stack_narrow(v_ref[bb]) * beta,
                               stack_narrow(k_in) * (beta * e_gam)], axis=1)
        sol = _mm(_expand_bd(t_cat, same), rhs)
        yield
        u, w = sol[:, :GDN_DK], sol[:, GDN_DK:]
        s = state[bb, d]
        w_w = jnp.where(head_lanes, jnp.concatenate([w] * nh, axis=1), 0.0)
        ws = _mm(jnp.concatenate([w_w, stack_wide(q_in) * e_gam], axis=0), s)
        yield
        v_new = u - ws[:hc]
        out = ws[hc:] + _mm(attn, v_new)
        o_ref[bb] = jnp.concatenate([out[h * c:(h + 1) * c] for h in range(nh)], axis=1)
        k_dec = k_w * jnp.exp(g_last - gam)
        state[bb, d] = s * jnp.exp(g_last_s) + _mm(k_dec, v_new, _TN)

    _interleave([chain(bb, d) for bb in range(qf.shape[0]) for d in range(2)])


def _gdn_scan(q, k, v, gates, bsz, t_len, bb=SCAN_BATCH):
    n = q.shape[0]
    c = GDN_CHUNK
    nc = t_len // c
    bb = math.gcd(bb, bsz)
    q, k, v, gates =(a.reshape(bsz, t_len, a.shape[1]) for a in (q, k, v, gates))

    def fwd(g, i):
        return (g, i, 0)

    def bwd(g, i):
        return (g, nc - 1 - i, 0)

    def specs(idx):
        wide = pl.BlockSpec((bb, c, 512), idx)
        return [wide, wide, wide, pl.BlockSpec((bb, c, LANES), idx)]

    of, ob = pl.pallas_call(
        _gdn_scan_kernel, grid=(bsz // bb, nc),
        in_specs=specs(fwd) + specs(bwd),
        out_specs=[pl.BlockSpec((bb, c, 512), fwd), pl.BlockSpec((bb, c, 512), bwd)],
        out_shape=[jax.ShapeDtypeStruct((bsz, t_len, 512), F32)] * 2,
        scratch_shapes=[pltpu.VMEM((bb, 2, GDN_HEADS * GDN_DK, GDN_DK), F32)],
        compiler_params=_params("parallel", "arbitrary"), name="gdn_scan",
    )(q, k, v, gates, q, k, v, gates)
    return of.reshape(n, 512), ob.reshape(n, 512)


_fft_mm = _mm
HY_CB = 16
HY_CHAINS = 2


def _hy_filter_kernel(z_ref, w1_ref, b1_ref, w2_ref, b2_ref, w3_ref, b3_ref, fr_ref, w4t_ref,
                      dl_ref, o_ref, *, t_len):
    i = pl.program_id(0)
    tl = o_ref.shape[1]
    fr = fr_ref[...]
    h = jnp.sin(fr * (_mm3(z_ref[...], w1_ref[...]) + b1_ref[...]))
    h = jnp.sin(fr * (_mm3(h, w2_ref[...]) + b2_ref[...]))
    h = jnp.sin(fr * (_mm3(h, w3_ref[...]) + b3_ref[...]))
    ht = _mm3(w4t_ref[...], h, _NT)
    pos = (lax.broadcasted_iota(jnp.int32, (1, tl), 1) + i * tl).astype(F32)
    t_norm = pos / float(max(t_len - 1, 1))
    o_ref[...] = ht * jnp.exp(-t_norm * dl_ref[...])


def _hy_filters(z, w1p, b1, w2, b2, w3, b3, freq, w4t, deltas, t_len):
    tl = min(512, t_len)
    nch = w4t.shape[0]
    full = lambda a: pl.BlockSpec(a.shape, lambda i: (0,) * a.ndim)
    return pl.pallas_call(
        functools.partial(_hy_filter_kernel, t_len=t_len), grid=(t_len // tl,),
        in_specs=[pl.BlockSpec((tl, z.shape[1]), lambda i: (i, 0))]
        + [full(a) for a in (w1p, b1, w2, b2, w3, b3, freq, w4t, deltas)],
        out_specs=pl.BlockSpec((nch, tl), lambda i: (0, i)),
        out_shape=jax.ShapeDtypeStruct((nch, t_len), F32),
        compiler_params=_params("parallel"), name="hy_filter",
    )(z, w1p, b1, w2, b2, w3, b3, freq, w4t, deltas)


def _fft_fwd(x3, x3i, fst, twr, twi, wbig):
    cb, r = x3.shape[0], x3.shape[1]
    n1 = twr.shape[0]
    xcat = jnp.concatenate([x3[c] for c in range(cb)], axis=1)
    if x3i is None:
        y = _fft_mm(fst[:, :r], xcat)
    else:
        xicat = jnp.concatenate([x3i[c] for c in range(cb)], axis=1)
        y = _fft_mm(fst, jnp.concatenate([xcat, xicat], axis=0))
    yield
    yr, yi = y[:n1], y[n1:]
    twr_t = jnp.concatenate([twr] * cb, axis=1)
    twi_t = jnp.concatenate([twi] * cb, axis=1)
    tr = yr * twr_t - yi * twi_t
    ti = yr * twi_t + yi * twr_t
    cat = jnp.concatenate(
        [jnp.concatenate([tr[:, c * LANES:(c + 1) * LANES], ti[:, c * LANES:(c + 1) * LANES]], axis=1)
         for c in range(cb)], axis=0)
    z = _fft_mm(cat, wbig)
    yield
    return z[:, :LANES].reshape(cb, n1, LANES), z[:, LANES:].reshape(cb, n1, LANES)


def _fft_inv(zr, zi, gst, twr, twi, wbigc):
    cb, n1 = zr.shape[0], zr.shape[1]
    r = gst.shape[0] // 2
    cat = jnp.concatenate([zr, zi], axis=-1).reshape(cb * n1, 2 * LANES)
    a = _fft_mm(cat, wbigc)
    yield
    ar, ai = a[:, :LANES].reshape(cb, n1, LANES), a[:, LANES:].reshape(cb, n1, LANES)
    tr = ar * twr + ai * twi
    ti = ai * twr - ar * twi
    tt = jnp.concatenate([jnp.concatenate([tr[c] for c in range(cb)], axis=1),
                          jnp.concatenate([ti[c] for c in range(cb)], axis=1)], axis=0)
    out = _fft_mm(gst, tt)
    yield
    return (jnp.stack([out[:r, c * LANES:(c + 1) * LANES] for c in range(cb)]),
            jnp.stack([out[r:, c * LANES:(c + 1) * LANES] for c in range(cb)]))


def _hy_spec_kernel(hf_ref, hb_ref, fst_ref, twr_ref, twi_ref, wbig_ref, hr_ref, hi_ref):
    args = (fst_ref[...], twr_ref[...], twi_ref[...], wbig_ref[...])
    out = {}

    def chain(name, ref):
        out[name] = yield from _fft_fwd(ref[...], None, *args)

    _interleave([chain("f", hf_ref), chain("b", hb_ref)])
    hr_ref[...] = out["f"][0] + out["b"][0]
    hi_ref[...] = out["f"][1] - out["b"][1]


def _hy_spectra(filt3, consts, cb=HY_CB):
    fst, gst, twr, twi, wbig, wbigc = consts
    r = filt3.shape[1]
    n1 = twr.shape[0]
    nj = HY_DIM // cb
    full = lambda a: pl.BlockSpec(a.shape, lambda o, j: (0,) * a.ndim)
    out_spec = pl.BlockSpec((cb, n1, LANES), lambda o, j: (o * nj + j, 0, 0))
    return pl.pallas_call(
        _hy_spec_kernel, grid=(2, nj),
        in_specs=[pl.BlockSpec((cb, r, LANES), lambda o, j: (2 * o * nj + j, 0, 0)),
                  pl.BlockSpec((cb, r, LANES), lambda o, j: ((2 * o + 1) * nj + j, 0, 0)),
                  full(fst), full(twr), full(twi), full(wbig)],
        out_specs=[out_spec, out_spec],
        out_shape=[jax.ShapeDtypeStruct((2 * HY_DIM, n1, LANES), F32)] * 2,
        compiler_params=_params("parallel", "parallel"), name="hy_spec",
    )(filt3, filt3, fst, twr, twi, wbig)


def _tshift(x3, back, lane, rowi):
    r = x3.shape[1]
    if back:
        rl = pltpu.roll(x3, 1, 2)
        rr = pltpu.roll(rl, 1, 1)
        edge = lane == 0
        out = jnp.where(edge, rr, rl)
        return jnp.where(edge & (rowi == 0), 0.0, out)
    rl = pltpu.roll(x3, LANES - 1, 2)
    rr = pltpu.roll(rl, r - 1, 1)
    edge = lane == LANES - 1
    out = jnp.where(edge, rr, rl)
    return jnp.where(edge & (rowi == r - 1), 0.0, out)


def _hy_conv_kernel(x1_ref, x2_ref, v_ref, p1_ref, p2_ref, pv_ref, sk_ref, h1r_ref, h1i_ref,
                    h2r_ref, h2i_ref, fst_ref, gst_ref, twr_ref, twi_ref, wbig_ref, wbigc_ref,
                    o_ref):
    cb, nb, r, _ = x1_ref.shape
    half = cb // HY_CHAINS
    shape = (half, r, LANES)
    lane = lax.broadcasted_iota(jnp.int32, shape, 2)
    rowi = lax.broadcasted_iota(jnp.int32, shape, 1)
    fwd = (fst_ref[...], twr_ref[...], twi_ref[...], wbig_ref[...])
    inv = (gst_ref[...], twr_ref[...], twi_ref[...], wbigc_ref[...])

    def chain(lo):
        ch = pl.ds(lo, half)

        def short_conv(x_ref, p_ref):
            p = p_ref[ch]
            out = []
            for b in range(nb):
                x = x_ref[ch, b].astype(F32)
                out.append(p[:, 0:1, :] * _tshift(x, True, lane, rowi) + p[:, 1:2, :] * x
                           + p[:, 2:3, :] * _tshift(x, False, lane, rowi) + p[:, 3:4, :])
            return out

        x1 = short_conv(x1_ref, p1_ref)
        x2 = short_conv(x2_ref, p2_ref)
        z = short_conv(v_ref, pv_ref)
        sk = sk_ref[ch]
        for o, (gate, hr_ref, hi_ref) in enumerate(((x1, h1r_ref, h1i_ref), (x2, h2r_ref, h2i_ref))):
            zr, zi = yield from _fft_fwd(z[0], z[1], *fwd)
            hr, hi = hr_ref[ch], hi_ref[ch]
            y = yield from _fft_inv(zr * hr - zi * hi, zr * hi + zi * hr, *inv)
            z = [gate[b] * (y[b] + sk[:, o:o + 1, :] * z[b]) for b in range(nb)]
        for b in range(nb):
            o_ref[ch, b] = z[b].astype(o_ref.dtype)

    _interleave([chain(i * half) for i in range(HY_CHAINS)])


def _hy_conv(projt4, pp, ps, hr, hi, consts, cb=HY_CB):
    fst, gst, twr, twi, wbig, wbigc = consts
    _, bsz, r, _ = projt4.shape
    n1 = twr.shape[0]
    nj = HY_DIM // cb
    full = lambda a: pl.BlockSpec(a.shape, lambda j, b: (0,) * a.ndim)

    assert bsz % 2 == 0, "sequences are transformed in pairs"

    def xs(off):
        return pl.BlockSpec((cb, 2, r, LANES), lambda j, b: (off * nj + j, b, 0, 0))

    def ps_(off):
        return pl.BlockSpec((cb, SUBLANES, LANES), lambda j, b: (off * nj + j, 0, 0))

    def hs(off):
        return pl.BlockSpec((cb, n1, LANES), lambda j, b: (off * nj + j, 0, 0))

    return pl.pallas_call(
        _hy_conv_kernel, grid=(nj, bsz // 2),
        in_specs=[xs(0), xs(1), xs(2), ps_(0), ps_(1), ps_(2), ps_(0), hs(0), hs(0), hs(1), hs(1),
                  full(fst), full(gst), full(twr), full(twi), full(wbig), full(wbigc)],
        out_specs=pl.BlockSpec((cb, 2, r, LANES), lambda j, b: (j, b, 0, 0)),
        out_shape=jax.ShapeDtypeStruct((HY_DIM, bsz, r, LANES), BF16),
        compiler_params=_params("parallel", "arbitrary"), name="hy_conv",
    )(projt4, projt4, projt4, pp, pp, pp, ps, hr, hi, hr, hi, fst, gst, twr, twi, wbig, wbigc)


def _fft_consts(t_len):
    n = 2 * t_len
    n1 = n // LANES
    r = t_len // LANES
    k1 = np.arange(n1)[:, None]
    th1 = 2.0 * np.pi * k1 * np.arange(r)[None, :] / n1
    c1, s1 = np.cos(th1), np.sin(th1)
    fst = np.block([[c1, s1], [-s1, c1]])
    gst = np.block([[c1.T, -s1.T], [s1.T, c1.T]]) / n
    tht = 2.0 * np.pi * k1 * np.arange(LANES)[None, :] / n
    twr, twi = np.cos(tht), -np.sin(tht)
    th2 = 2.0 * np.pi * np.arange(LANES)[:, None] * np.arange(LANES)[None, :] / LANES
    cr, ci = np.cos(th2), -np.sin(th2)
    wbig = np.block([[cr, ci], [-ci, cr]])
    wbigc = np.block([[cr, -ci], [ci, cr]])
    return tuple(jnp.asarray(a, F32) for a in (fst, gst, twr, twi, wbig, wbigc))


def _hy_positions(t_len):
    pos = np.arange(t_len, dtype=np.float32)[:, None]
    t = pos / np.float32(max(t_len - 1, 1))
    bands = np.linspace(1e-4, HY_BANDS - 1, HY_BANDS, dtype=np.float32)[None, :]
    ang = bands * np.float32(2.0 * math.pi / t_len) * pos
    z = np.concatenate([t, np.cos(ang), -np.sin(ang)], -1).astype(np.float32)
    return jnp.asarray(np.pad(z, ((0, 0), (0, HY_FILT_HID - HY_EMB))))


def _hy_deltas():
    max_decay = math.log(HY_DECAY_TARGET) / HY_FAST_DECAY_PCT
    min_decay = math.log(HY_DECAY_TARGET) / HY_SLOW_DECAY_PCT
    d = np.abs(np.linspace(min_decay, max_decay, HY_DIM, dtype=np.float32))
    return jnp.asarray(np.tile(d, 4)[:, None])


def _lane_rep(a, rows=SUBLANES):
    ch, k = a.shape
    a = jnp.pad(a, ((0, 0), (0, rows - k)))
    return jnp.broadcast_to(a[:, :, None], (ch, rows, LANES))


def _out_even_kernel(x_ref, yat_ref, of_ref, ob_ref, zg_ref, nw_ref, wa_ref, wb_ref, g_ref, b_ref,
                     w1_ref, w2_ref, g2_ref, b2_ref, o_ref):
    mix = _mm(yat_ref[...], wa_ref[...], _TN)
    nw = nw_ref[...]
    for h in range(GDN_HEADS):
        sl = slice(h * GDN_DK, (h + 1) * GDN_DK)
        o = of_ref[:, sl] + ob_ref[:, sl]
        y = o * lax.rsqrt(jnp.mean(o * o, -1, keepdims=True) + RMS_EPS) * nw * _silu(zg_ref[:, sl])
        mix = mix + _mm(y, wb_ref[sl, :])
    x1 = _layer_norm(DEEPNORM_ALPHA * x_ref[...] + mix, g_ref[...], b_ref[...])
    o_ref[...] = _mlp_tail(x1, w1_ref, w2_ref, g2_ref, b2_ref)


def _out_even(x2, yat, of, ob, projm, norm_w, wa, wb, g, b, mlp, tm=512):
    n, d = x2.shape
    tm = min(tm, n)
    row = lambda w: pl.BlockSpec((tm, w), lambda i: (i, 0))
    consts = (norm_w, wa, wb, g, b) + tuple(mlp)
    return pl.pallas_call(
        _out_even_kernel, grid=(n // tm,),
        in_specs=[row(d), pl.BlockSpec((HY_DIM, tm), lambda i: (0, i)), row(512), row(512),
                  pl.BlockSpec((tm, 512), lambda i: (i, 3))] + [_resident(a) for a in consts],
        out_specs=row(d), out_shape=jax.ShapeDtypeStruct((n, d), F32),
        compiler_params=_params("parallel"), name="out_even",
    )(x2, yat, of, ob, projm, *consts)


def _even_layer(x2, bsz, t_len, w_in, hy_conv_w, hy_conv_b, f_w1, f_b1, f_w2, f_b2, f_w3, f_b3,
                f_w4, f_freq, hy_skip, gdn_conv_w, gdn_a_log, gdn_dt_bias, gdn_norm_w, w_out,
                ln_g, ln_b, mlp):
    d = x2.shape[1]
    hyw = 3 * HY_DIM
    wm = jnp.concatenate([w_in[:, hyw:], jnp.zeros((d, LANES - 12), F32)], axis=1).astype(BF16)
    wt = w_in[:, :hyw].T.astype(BF16)
    projm, projt = _project(x2, wm, wt)
    consts = _fft_consts(t_len)
    w1p = jnp.pad(f_w1, ((0, HY_FILT_HID - HY_EMB), (0, 0)))
    filt = _hy_filters(_hy_positions(t_len), w1p, f_b1[None], f_w2, f_b2[None], f_w3, f_b3[None],
                       f_freq[None], f_w4.T, _hy_deltas(), t_len)
    r = t_len // LANES
    hr, hi = _hy_spectra(filt.reshape(4 * HY_DIM, r, LANES), consts)
    pp = _lane_rep(jnp.concatenate([hy_conv_w.T, hy_conv_b[:, None]], axis=1))
    ps = _lane_rep(hy_skip.T)
    yat = _hy_conv(projt.reshape(hyw, bsz, r, LANES), pp, ps, hr, hi, consts)
    yat = yat.reshape(HY_DIM, bsz * t_len)
    alog = jnp.zeros((1, LANES), F32).at[0, :8].set(gdn_a_log.reshape(-1))
    dtb = jnp.zeros((1, LANES), F32).at[0, :8].set(gdn_dt_bias.reshape(-1))
    qn, kn, vn, gates = _gdn_pre(projm, gdn_conv_w, alog, dtb, bsz, t_len)
    of, ob = _gdn_scan(qn, kn, vn, gates, bsz, t_len)
    return _out_even(x2, yat, of, ob, projm, gdn_norm_w[None], w_out[:HY_DIM].astype(BF16),
                     w_out[HY_DIM:].astype(BF16), ln_g[None], ln_b[None], mlp)


def _hg_scan_kernel(qf_ref, ff_ref, vf_ref, qb_ref, fb_ref, vb_ref, low_ref, of_ref, ob_ref,
                    state, q_s, k_s, g_s, *, layer):
    n = pl.program_id(1)

    @pl.when(n == 0)
    def _():
        state[...] = jnp.zeros_like(state)

    tb = qf_ref.shape[0]
    c = HG_CHUNK
    nsub = tb // c
    low = low_ref[...]
    e = jnp.exp(low - jnp.max(low, axis=0, keepdims=True))
    p = e / jnp.sum(e, axis=0, keepdims=True)
    lb = jnp.sum(p[1:layer + 1], axis=0, keepdims=True)
    rb = lax.broadcasted_iota(jnp.int32, (tb, tb), 0)
    cb = lax.broadcasted_iota(jnp.int32, (tb, tb), 1)
    same = (rb // c) == (cb // c)
    hb = c // 2
    rowh = lax.broadcasted_iota(jnp.int32, (hb, 1), 0)
    dirs = ((qf_ref, ff_ref, vf_ref, of_ref), (qb_ref, fb_ref, vb_ref, ob_ref))
    for d, (q_ref, f_ref, v_ref, o_ref) in enumerate(dirs):
        fg = lb + (1.0 - lb) * _sigmoid(f_ref[...])
        tri = (same & ((cb <= rb) if d == 0 else (cb >= rb))).astype(BF16)
        g_s[d] = _sel_mm(tri, jnp.log(fg))
        k_s[d] = 1.0 - fg
        q_s[d] = _silu(q_ref[...])

    def body(s, carry):
        def sub_rows(d, u):
            idx = s * HG_SUBS + u
            sub = idx if d == 0 else nsub - 1 - idx
            return pl.ds(pl.multiple_of(sub * c, c), c)

        def pairwise(d, qb, gkb, gb, vb):
            acc = None
            for j in range(hb):
                dec = jnp.exp(gb - gkb[j:j + 1, :])
                a_col = jnp.sum(qb * dec, axis=1, keepdims=True)
                a_col = jnp.where((rowh >= j) if d == 0 else (rowh <= j), a_col, 0.0)
                term = a_col * vb[j:j + 1, :]
                acc = term if acc is None else acc + term
            return acc

        def chain(d, h):
            v_ref, o_ref = dirs[d][2], dirs[d][3]
            sl = slice(h * HG_DK, (h + 1) * HG_DK)
            st = state[d, h]
            pending = []
            for u in range(HG_SUBS):
                rows = sub_rows(d, u)
                qc, kc, gc, vc = q_s[d, rows, sl], k_s[d, rows, sl], g_s[d, rows, sl], v_ref[rows, sl]
                g_last = gc[c - 1:c, :] if d == 0 else gc[0:1, :]
                inter = _mm(qc * jnp.exp(gc), st, _NT)
                upd = _mm(vc, kc * jnp.exp(g_last - gc), _TN)
                yield
                st = st * jnp.exp(g_last) + upd
                pending.append((rows, qc, kc, gc, vc, inter))
            state[d, h] = st
            lo, hi = slice(0, hb), slice(hb, c)
            src, dst, bnd = (lo, hi, hb - 1) if d == 0 else (hi, lo, hb)
            for rows, qc, kc, gc, vc, inter in pending:
                g_bnd = gc[bnd:bnd + 1, :]
                q_x = qc[dst] * jnp.exp(gc[dst] - g_bnd)
                k_y = kc[src] * jnp.exp(g_bnd - gc[src])
                v_src = vc[src]
                cross = None
                for j in range(hb):
                    term = jnp.sum(q_x * k_y[j:j + 1, :], axis=1, keepdims=True) * v_src[j:j + 1, :]
                    cross = term if cross is None else cross + term
                gk = gc - jnp.log(kc)
                o_lo = pairwise(d, qc[lo], gk[lo], gc[lo], vc[lo])
                o_hi = pairwise(d, qc[hi], gk[hi], gc[hi], vc[hi])
                if d == 0:
                    o_hi = o_hi + cross
                else:
                    o_lo = o_lo + cross
                o_ref[rows, sl] = inter + jnp.concatenate([o_lo, o_hi], axis=0)

        _interleave([chain(d, h) for d in range(2) for h in range(HG_HEADS)])
        return carry

    lax.fori_loop(0, nsub // HG_SUBS, body, 0)


def _hg_scan(proj, hg_lower, layer, bsz, t_len, tb=128):
    n = proj.shape[0]
    nb = t_len // tb

    def spec(colb, rev):
        if rev:
            return pl.BlockSpec((tb, 512), lambda b, i: (b * nb + nb - 1 - i, colb))
        return pl.BlockSpec((tb, 512), lambda b, i: (b * nb + i, colb))

    return pl.pallas_call(
        functools.partial(_hg_scan_kernel, layer=layer), grid=(bsz, nb),
        in_specs=[spec(0, False), spec(1, False), spec(3, False),
                  spec(0, True), spec(2, True), spec(3, True),
                  pl.BlockSpec(hg_lower.shape, lambda b, i: (0, 0))],
        out_specs=[spec(0, False), spec(0, True)],
        out_shape=[jax.ShapeDtypeStruct((n, 512), F32)] * 2,
        scratch_shapes=[pltpu.VMEM((2, HG_HEADS, HG_DK, HG_DK), F32)]
        + [pltpu.VMEM((2, tb, 512), F32)] * 3,
        compiler_params=_params("parallel", "arbitrary"), name="hg_scan",
    )(proj, proj, proj, proj, proj, proj, hg_lower)


def _rw_pre_kernel(rc, rp, rn, kc, kp, kn, vc, vp, vn, lc, lp, ln, mu_ref, w0_ref, w2_ref,
                   a0_ref, a2_ref, g2_ref, kk_ref, ka_ref, bd_ref,
                   r_o, k_o, v_o, kk_o, kka_o, lwf_o, lwb_o, g_o, scr):
    i = pl.program_id(1)
    first = i == 0
    last = i == pl.num_programs(1) - 1
    tb = rc.shape[0]

    def shifted(cur_ref, prev_ref, next_ref, col0):
        w = cur_ref.shape[1]
        cur = cur_ref[...]
        scr[pl.ds(SUBLANES, tb), pl.ds(0, w)] = cur
        scr[pl.ds(0, SUBLANES), pl.ds(0, w)] = jnp.where(first, 0.0, prev_ref[...])
        scr[pl.ds(SUBLANES + tb, SUBLANES), pl.ds(0, w)] = jnp.where(last, 0.0, next_ref[...])
        prev = scr[pl.ds(SUBLANES - 1, tb), pl.ds(0, w)]
        nxt = scr[pl.ds(SUBLANES + 1, tb), pl.ds(0, w)]
        mu = mu_ref[:, col0:col0 + w]
        return cur + mu[0:1] * (prev - cur) + mu[1:2] * (nxt - cur)

    r = shifted(rc, rp, rn, 0)
    k = shifted(kc, kp, kn, RW_DIM)
    v = shifted(vc, vp, vn, 2 * RW_DIM)
    lo = shifted(lc, lp, ln, 3 * RW_DIM)
    lo_wa, lo_g = lo[:, :LANES], lo[:, LANES:]
    th = jnp.tanh(lo_wa)
    for d, o_ref in enumerate((lwf_o, lwb_o)):
        w_raw = w0_ref[d:d + 1, :] + _mm(th, w2_ref[d])
        o_ref[...] = -math.exp(-0.5) * _sigmoid(w_raw)
    a = _sigmoid(a0_ref[...] + _mm(lo_wa, a2_ref[...]))
    g_o[...] = _mm(_sigmoid(lo_g), g2_ref[...]).astype(g_o.dtype)
    kx = k * kk_ref[...]
    kk = kx * lax.rsqrt(_mm(kx * kx, bd_ref[...]) + L2_EPS)
    r_o[...] = r.astype(r_o.dtype)
    v_o[...] = v.astype(v_o.dtype)
    k_o[...] = (k * (1.0 + (a - 1.0) * ka_ref[...])).astype(k_o.dtype)
    kk_o[...] = kk.astype(kk_o.dtype)
    kka_o[...] = (kk * a).astype(kka_o.dtype)


def _rw_pre(proj, mu, w0, w2p, a0, a2p, g2, k_k, k_a, bd, bsz, t_len, tb=256):
    n = proj.shape[0]
    nb = t_len // tb
    base = 2560
    specs = []
    for off, w in ((0, 512), (512, 512), (1024, 512), (1536, 256)):
        specs += _halo_specs(w, (base + off) // w, tb, t_len, bsz)
    full = lambda a: pl.BlockSpec(a.shape, lambda b, i: (0,) * a.ndim)
    out_spec = pl.BlockSpec((tb, RW_DIM), lambda b, i: (b * nb + i, 0))
    consts = (mu, w0, w2p, a0, a2p, g2, k_k, k_a, bd)
    return pl.pallas_call(
        _rw_pre_kernel, grid=(bsz, nb),
        in_specs=specs + [full(a) for a in consts],
        out_specs=[out_spec] * 8,
        out_shape=[jax.ShapeDtypeStruct((n, RW_DIM), dt) for dt in (BF16,) * 5 + (F32, F32, BF16)],
        scratch_shapes=[pltpu.VMEM((tb + 2 * SUBLANES, RW_DIM), F32)],
        compiler_params=_params("parallel", "parallel"), name="rw_pre",
    )(*([proj] * 12), *consts)


def _rw_scan_kernel(rf, kf, vf, bf, af, wf, rb, kb, vb, bb, ab, wb, yf, yb, state):
    n = pl.program_id(1)

    @pl.when(n == 0)
    def _():
        state[...] = jnp.zeros_like(state)

    c = RW_CHUNK
    gw = RW_GROUP * RW_HD
    gc = RW_GROUP * c
    row = lax.broadcasted_iota(jnp.int32, (c, c), 0)
    col = lax.broadcasted_iota(jnp.int32, (c, c), 1)
    rr = lax.broadcasted_iota(jnp.int32, (gc, gc), 0)
    cc = lax.broadcasted_iota(jnp.int32, (gc, gc), 1)
    same = (rr // c) == (cc // c)
    head_lanes = (lax.broadcasted_iota(jnp.int32, (gc, gw), 0) // c
                  == lax.broadcasted_iota(jnp.int32, (gc, gw), 1) // RW_HD)
    rs = lax.broadcasted_iota(jnp.int32, (gw, gw), 0)
    cs = lax.broadcasted_iota(jnp.int32, (gw, gw), 1)
    head_blocks = ((rs // RW_HD) == (cs // RW_HD)).astype(F32)
    row_cat = lax.broadcasted_iota(jnp.int32, (c, gc), 0)
    col_cat = lax.broadcasted_iota(jnp.int32, (c, gc), 1) % c
    eye_cat = (col_cat == row_cat).astype(F32)

    def stack(x):
        return jnp.where(head_lanes, jnp.concatenate([x] * RW_GROUP, axis=0), 0.0)

    dirs = ((rf, kf, vf, bf, af, wf, yf), (rb, kb, vb, bb, ab, wb, yb))
    def chain(ib, d, p):
        r_ref, k_ref, v_ref, b_ref, a_ref, w_ref, y_ref = dirs[d]
        sl = slice(p * gw, (p + 1) * gw)
        incl = (col_cat <= row_cat) if d == 0 else (col_cat >= row_cat)
        strict = (col_cat < row_cat) if d == 0 else (col_cat > row_cat)
        lw = w_ref[ib, :, sl]
        tri = ((col <= row) if d == 0 else (col >= row)).astype(BF16)
        g_in = _sel_mm(tri, lw)
        yield
        g_ex = g_in - lw
        g_end = g_in[c - 1:c, :] if d == 0 else g_in[0:1, :]
        e_neg = jnp.exp(-g_in)
        e_end = jnp.exp(g_end - g_in)
        k, a, v = k_ref[ib, :, sl], a_ref[ib, :, sl], v_ref[ib, :, sl]
        lhs = jnp.concatenate([r_ref[ib, :, sl] * jnp.exp(g_in), b_ref[ib, :, sl] * jnp.exp(g_ex)],
                              axis=0)
        ak = _mm(lhs, stack(k * e_neg), _NT)
        yield
        aa = _mm(lhs, stack(a * e_neg), _NT)
        yield
        a_rk = jnp.where(incl, ak[:c], 0.0)
        a_ra = jnp.where(incl, aa[:c], 0.0)
        a_bk = jnp.where(strict, ak[c:], 0.0)
        a_ba = jnp.where(strict, aa[c:], 0.0)
        t_inv = yield from _unit_tri_inv_cat(a_ba, eye_cat, same, c)
        yield
        m = state[ib, d, p]
        xm = _mm(lhs, m, _NT)
        yield
        av = _mm(jnp.concatenate([a_rk, a_bk], axis=0), stack(v))
        yield
        u = _mm(t_inv, stack(xm[c:] + av[c:]))
        yield
        y_ref[ib, :, sl] = xm[:c] + av[:c] - _mm(a_ra, stack(u))
        yield
        upd = _mm(jnp.concatenate([v.astype(F32), u], axis=0),
                  jnp.concatenate([k * e_end, -(a * e_end)], axis=0), _TN)
        state[ib, d, p] = m * jnp.exp(g_end) + upd * head_blocks

    _interleave([chain(ib, d, p) for ib in range(rf.shape[0]) for d in range(2)
                 for p in range(RW_DIM // gw)])


def _rw_scan(r, k, v, kk, kka, lwf, lwb, bsz, t_len, bb=SCAN_BATCH):
    n = r.shape[0]
    c = RW_CHUNK
    gw = RW_GROUP * RW_HD
    nc = t_len // c
    bb = math.gcd(bb, bsz)
    r, k, v, kk, kka, lwf, lwb =(a.reshape(bsz, t_len, RW_DIM) for a in (r, k, v, kk, kka, lwf, lwb))
    fwd = pl.BlockSpec((bb, c, RW_DIM), lambda g, i: (g, i, 0))
    bwd = pl.BlockSpec((bb, c, RW_DIM), lambda g, i: (g, nc - 1 - i, 0))
    yf, yb = pl.pallas_call(
        _rw_scan_kernel, grid=(bsz // bb, nc),
        in_specs=[fwd] * 6 + [bwd] * 6, out_specs=[fwd, bwd],
        out_shape=[jax.ShapeDtypeStruct((bsz, t_len, RW_DIM), F32)] * 2,
        scratch_shapes=[pltpu.VMEM((bb, 2, RW_DIM // gw, gw, gw), F32)],
        compiler_params=_params("parallel", "arbitrary"), name="rw_scan",
    )(r, k, v, kk, kka, lwf, r, k, v, kk, kka, lwb)
    return yf.reshape(n, RW_DIM), yb.reshape(n, RW_DIM)


def _out_odd_kernel(x_ref, hf_ref, hb_ref, hg_ref, hnw_ref, yf_ref, yb_ref, r_ref, k_ref, v_ref,
                    gate_ref, rk_ref, lw_ref, lb_ref, bd_ref, wc_ref, wd_ref, g_ref, b_ref,
                    w1_ref, w2_ref, g2_ref, b2_ref, o_ref):
    nw = hnw_ref[...]
    mix = None
    for h in range(HG_HEADS):
        sl = slice(h * HG_DK, (h + 1) * HG_DK)
        o = hf_ref[:, sl] + hb_ref[:, sl]
        y = o * lax.rsqrt(jnp.mean(o * o, -1, keepdims=True) + RMS_EPS) * nw * _silu(hg_ref[:, sl])
        part = _mm(y, wc_ref[sl, :])
        mix = part if mix is None else mix + part
    bd = bd_ref[...]
    y = yf_ref[...] + yb_ref[...]
    yc = y - _sel_mm(bd, y, sel_first=False) * (1.0 / RW_HD)
    var = _mm(yc * yc, bd) * (1.0 / RW_HD)
    yn = yc * lax.rsqrt(var + RW_GN_EPS) * lw_ref[...] + lb_ref[...]
    rk = r_ref[...].astype(F32) * k_ref[...] * rk_ref[...]
    bonus = _mm(rk, bd) * v_ref[...]
    mix = mix + _mm((yn + bonus) * gate_ref[...], wd_ref[...])
    x1 = _layer_norm(DEEPNORM_ALPHA * x_ref[...] + mix, g_ref[...], b_ref[...])
    o_ref[...] = _mlp_tail(x1, w1_ref, w2_ref, g2_ref, b2_ref)


def _out_odd(x2, hf, hb, proj, hnw, yf, yb, r, k, v, gate, r_k, lnx_w, lnx_b, bd, wc, wd, g, b,
             mlp, tm=512):
    n, d = x2.shape
    tm = min(tm, n)
    row = lambda w: pl.BlockSpec((tm, w), lambda i: (i, 0))
    consts = (r_k, lnx_w, lnx_b, bd, wc, wd, g, b) + tuple(mlp)
    return pl.pallas_call(
        _out_odd_kernel, grid=(n // tm,),
        in_specs=[row(d), row(512), row(512), pl.BlockSpec((tm, 512), lambda i: (i, 4)),
                  _resident(hnw)] + [row(512)] * 6 + [_resident(a) for a in consts],
        out_specs=row(d), out_shape=jax.ShapeDtypeStruct((n, d), F32),
        compiler_params=_params("parallel"), name="out_odd",
    )(x2, hf, hb, proj, hnw, yf, yb, r, k, v, gate, *consts)


def _odd_layer(x2, bsz, t_len, layer, w_in, hg_lower, hg_norm_w, rw_mu, rw_w0, rw_w2, rw_a0, rw_a2,
               rw_g2, rw_k_k, rw_k_a, rw_r_k, rw_lnx_w, rw_lnx_b, w_out, ln_g, ln_b, mlp):
    proj = _project(x2, w_in.astype(BF16))
    hf, hb = _hg_scan(proj, hg_lower, layer, bsz, t_len)
    idx = np.arange(RW_DIM) // RW_HD
    bd = jnp.asarray(idx[:, None] == idx[None, :], BF16)
    lora = rw_w2.shape[1]
    w2p = jnp.pad(rw_w2, ((0, 0), (0, LANES - lora), (0, 0)))
    a2p = jnp.pad(rw_a2, ((LANES - rw_a2.shape[0], 0), (0, 0)))
    r, k, v, kk, kka, lwf, lwb, gate = _rw_pre(proj, rw_mu, rw_w0, w2p, rw_a0[None], a2p, rw_g2,
                                               rw_k_k[None], rw_k_a[None], bd, bsz, t_len)
    yf, yb = _rw_scan(r, k, v, kk, kka, lwf, lwb, bsz, t_len)
    return _out_odd(x2, hf, hb, proj, hg_norm_w[None], yf, yb, r, k, v, gate,
                    rw_r_k.reshape(1, RW_DIM), rw_lnx_w[None], rw_lnx_b[None], bd,
                    w_out[:512].astype(BF16), w_out[512:].astype(BF16), ln_g[None], ln_b[None], mlp)


def kernel(x, ev_w_in, hy_conv_w, hy_conv_b, hy_filt_w1, hy_filt_b1, hy_filt_w2, hy_filt_b2, hy_filt_w3, hy_filt_b3, hy_filt_w4, hy_filt_freq, hy_skip, gdn_conv_w, gdn_a_log, gdn_dt_bias, gdn_norm_w, ev_w_out, od_w_in, hg_lower, hg_norm_w, rw_mu, rw_w0, rw_w2, rw_a0, rw_a2, rw_g2, rw_k_k, rw_k_a, rw_r_k, rw_lnx_w, rw_lnx_b, od_w_out, ln_g, ln_b, mlp_w1, mlp_w2):
    bsz, t_len, d = x.shape
    x2 = x.reshape(bsz * t_len, d)
    for layer in range(DEPTH):
        i = layer // 2
        mlp = (mlp_w1[layer].astype(BF16), mlp_w2[layer].astype(BF16), ln_g[layer, 1][None],
               ln_b[layer, 1][None])
        if layer % 2 == 0:
            x2 = _even_layer(x2, bsz, t_len, ev_w_in[i], hy_conv_w[i], hy_conv_b[i], hy_filt_w1[i],
                             hy_filt_b1[i], hy_filt_w2[i], hy_filt_b2[i], hy_filt_w3[i],
                             hy_filt_b3[i], hy_filt_w4[i], hy_filt_freq[i], hy_skip[i],
                             gdn_conv_w[i], gdn_a_log[i], gdn_dt_bias[i], gdn_norm_w[i],
                             ev_w_out[i], ln_g[layer, 0], ln_b[layer, 0], mlp)
        else:
            x2 = _odd_layer(x2, bsz, t_len, layer, od_w_in[i], hg_lower, hg_norm_w[i], rw_mu[i],
                            rw_w0[i], rw_w2[i], rw_a0[i], rw_a2[i], rw_g2[i], rw_k_k[i], rw_k_a[i],
                            rw_r_k[i], rw_lnx_w[i], rw_lnx_b[i], od_w_out[i], ln_g[layer, 0],
                            ln_b[layer, 0], mlp)
    return x2.reshape(bsz, t_len, d)
```

```python
import functools
import math

import numpy as np
import jax
import jax.numpy as jnp
from jax import lax
from jax.experimental import pallas as pl
from jax.experimental.pallas import tpu as pltpu

F32 = jnp.float32
BF16 = jnp.bfloat16

D_MODEL = 1024
DEPTH = 2
D_FF = 4 * D_MODEL
DEEPNORM_ALPHA = (2.0 * DEPTH) ** 0.25
LN_EPS = 1e-5
RMS_EPS = 1e-6
L2_EPS = 1e-6

HY_DIM = 512
HY_EMB = 33
HY_BANDS = 16
HY_FILT_HID = 64
HY_DECAY_TARGET = 1e-2
HY_FAST_DECAY_PCT = 0.3
HY_SLOW_DECAY_PCT = 1.5

GDN_HEADS = 4
GDN_DK = 128
GDN_CHUNK = 64

HG_HEADS = 4
HG_DK = 128
HG_CHUNK = 16
HG_SUBS = 4

RW_HEADS = 8
RW_HD = 64
RW_DIM = 512
RW_CHUNK = 64
RW_GROUP = 4
RW_GN_EPS = 64e-5

SCAN_BATCH = 4
SCAN_CHUNKS = 2

LANES = 128
SUBLANES = 8
VMEM_LIMIT = 56 * 1024 * 1024

_NN = (((1,), (0,)), ((), ()))
_NT = (((1,), (1,)), ((), ()))
_TN = (((0,), (0,)), ((), ()))


def _mm(a, b, dims=_NN):
    return lax.dot_general(a.astype(BF16), b.astype(BF16), dims, preferred_element_type=F32)


def _params(*sem):
    return pltpu.CompilerParams(dimension_semantics=sem, vmem_limit_bytes=VMEM_LIMIT)


def _sigmoid(x):
    return 1.0 / (1.0 + jnp.exp(-x))


def _silu(x):
    return x * _sigmoid(x)


def _softplus(x):
    return jnp.maximum(x, 0.0) + jnp.log(1.0 + jnp.exp(-jnp.abs(x)))


def _layer_norm(v, g, b):
    mean = jnp.mean(v, axis=-1, keepdims=True)
    vc = v - mean
    var = jnp.mean(vc * vc, axis=-1, keepdims=True)
    return vc * lax.rsqrt(var + LN_EPS) * g + b


def _col(x, idx, lane):
    return jnp.sum(jnp.where(lane == idx, x, 0.0), axis=1, keepdims=True)


def _split2(a):
    hi = a.astype(BF16)
    return hi, (a - hi.astype(F32)).astype(BF16)


def _mm3(a, b, dims=_NN):
    ah, al = _split2(a)
    bh, bl = _split2(b)
    dot = functools.partial(lax.dot_general, dimension_numbers=dims, preferred_element_type=F32)
    return dot(ah, bh) + dot(ah, bl) + dot(al, bh)


def _sel_mm(sel, x, dims=_NN, sel_first=True):
    hi = x.astype(BF16)
    r1 = x - hi.astype(F32)
    mid = r1.astype(BF16)
    lo = (r1 - mid.astype(F32)).astype(BF16)
    dot = functools.partial(lax.dot_general, dimension_numbers=dims, preferred_element_type=F32)
    if sel_first:
        return dot(sel, hi) + dot(sel, mid) + dot(sel, lo)
    return dot(hi, sel) + dot(mid, sel) + dot(lo, sel)


def _expand_bd(x_cat, same):
    g = x_cat.shape[1] // x_cat.shape[0]
    return jnp.where(same, jnp.concatenate([x_cat] * g, axis=0), 0.0)


def _unit_tri_inv_cat(a_cat, eye_cat, same, depth):
    c = a_cat.shape[0]
    p = -a_cat
    t = eye_cat + p
    p = _mm(p, _expand_bd(p, same))
    yield
    k = 4
    while k < depth:
        both = _mm(jnp.concatenate([p, t], axis=0), _expand_bd(p, same))
        yield
        p, t = both[:c], t + both[c:]
        k *= 2
    return t + _mm(t, _expand_bd(p, same))


def _interleave(chains):
    active = list(chains)
    while active:
        still = []
        for g in active:
            try:
                next(g)
                still.append(g)
            except StopIteration:
                pass
        active = still


def _proj_kernel(x_ref, w_ref, o_ref):
    o_ref[...] = _mm(x_ref[...], w_ref[...])


def _proj_t_kernel(x_ref, w_ref, wt_ref, o_ref, ot_ref):
    xb = x_ref[...].astype(BF16)
    o_ref[...] = _mm(xb, w_ref[...])
    ot_ref[...] = _mm(wt_ref[...], xb, _NT).astype(ot_ref.dtype)


def _project(x2, w, wt=None, tm=512):
    n, d = x2.shape
    m = w.shape[1]
    grid = (n // tm,)
    once = pl.Buffered(1)
    x_spec = pl.BlockSpec((tm, d), lambda i: (i, 0))
    w_spec = pl.BlockSpec((d, m), lambda i: (0, 0), pipeline_mode=once)
    o_spec = pl.BlockSpec((tm, m), lambda i: (i, 0))
    if wt is None:
        return pl.pallas_call(
            _proj_kernel, grid=grid, in_specs=[x_spec, w_spec], out_specs=o_spec,
            out_shape=jax.ShapeDtypeStruct((n, m), F32), compiler_params=_params("parallel"),
            name="proj")(x2, w)
    mt = wt.shape[0]
    return pl.pallas_call(
        _proj_t_kernel, grid=grid,
        in_specs=[x_spec, w_spec, pl.BlockSpec((mt, d), lambda i: (0, 0), pipeline_mode=once)],
        out_specs=[o_spec, pl.BlockSpec((mt, tm), lambda i: (0, i))],
        out_shape=[jax.ShapeDtypeStruct((n, m), F32), jax.ShapeDtypeStruct((mt, n), BF16)],
        compiler_params=_params("parallel"), name="proj_t")(x2, w, wt)


def _mlp_tail(x, w1_ref, w2_ref, g_ref, b_ref):
    h = jnp.maximum(_mm(x, w1_ref[...]), 0.0)
    return _layer_norm(DEEPNORM_ALPHA * x + _mm(h * h, w2_ref[...]), g_ref[...], b_ref[...])


def _resident(a):
    return pl.BlockSpec(a.shape, lambda i: (0,) * a.ndim, pipeline_mode=pl.Buffered(1))


def _fill_halo(scr, cur, prev8, next8, first, last, halo):
    tb = cur.shape[0]
    scr[pl.ds(SUBLANES, tb), :] = cur
    scr[pl.ds(0, SUBLANES), :] = jnp.where(first, 0.0, prev8)
    scr[pl.ds(SUBLANES + tb, SUBLANES), :] = jnp.where(last, 0.0, next8)


def _halo_specs(width, col_block, tb, t_len, bsz):
    nb = t_len // tb
    r8 = tb // SUBLANES
    tot8 = t_len // SUBLANES

    def cur(b, i):
        return (b * nb + i, col_block)

    def prev(b, i):
        return (jnp.maximum(b * tot8 + i * r8 - 1, 0), col_block)

    def nxt(b, i):
        return (jnp.minimum(b * tot8 + (i + 1) * r8, tot8 * bsz - 1), col_block)

    return [pl.BlockSpec((tb, width), cur), pl.BlockSpec((SUBLANES, width), prev),
            pl.BlockSpec((SUBLANES, width), nxt)]


def _gdn_pre_kernel(cur_ref, prev_ref, next_ref, ab_ref, cw_ref, alog_ref, dtb_ref,
                    q_ref, k_ref, v_ref, g_ref, scr):
    i = pl.program_id(1)
    tb = cur_ref.shape[0]
    _fill_halo(scr, cur_ref[...], prev_ref[...], next_ref[...], i == 0,
               i == pl.num_programs(1) - 1, 2)
    width = cw_ref.shape[0]
    acc = None
    for j in range(width):
        term = scr[pl.ds(SUBLANES - width // 2 + j, tb), :] * cw_ref[pl.ds(j, 1), :]
        acc = term if acc is None else acc + term
    qkv = _silu(acc)
    hd = GDN_HEADS * GDN_DK
    for h in range(GDN_HEADS):
        sl = slice(h * GDN_DK, (h + 1) * GDN_DK)
        q = qkv[:, sl]
        q_ref[:, sl] = (q * lax.rsqrt(jnp.sum(q * q, -1, keepdims=True) + L2_EPS)
                        * GDN_DK ** -0.5).astype(q_ref.dtype)
        k = qkv[:, hd + h * GDN_DK: hd + (h + 1) * GDN_DK]
        k_ref[:, sl] = (k * lax.rsqrt(jnp.sum(k * k, -1, keepdims=True) + L2_EPS)).astype(k_ref.dtype)
    v_ref[...] = qkv[:, 2 * hd:].astype(v_ref.dtype)
    ab = ab_ref[...]
    lane = lax.broadcasted_iota(jnp.int32, ab.shape, 1)
    log_g = -jnp.exp(alog_ref[...]) * _softplus(ab + dtb_ref[...])
    g_ref[...] = jnp.where(lane < 2 * GDN_HEADS, log_g, _sigmoid(ab))


def _gdn_pre(proj, conv_w, a_log_row, dt_bias_row, bsz, t_len, tb=512):
    n = proj.shape[0]
    wq = 3 * GDN_HEADS * GDN_DK
    nb = t_len // tb
    specs = _halo_specs(wq, 0, tb, t_len, bsz)
    ab_col = (wq + GDN_HEADS * GDN_DK) // LANES
    out_spec = pl.BlockSpec((tb, 512), lambda b, i: (b * nb + i, 0))
    return pl.pallas_call(
        _gdn_pre_kernel, grid=(bsz, nb),
        in_specs=specs + [pl.BlockSpec((tb, LANES), lambda b, i: (b * nb + i, ab_col)),
                          pl.BlockSpec(conv_w.shape, lambda b, i: (0, 0)),
                          pl.BlockSpec((1, LANES), lambda b, i: (0, 0)),
                          pl.BlockSpec((1, LANES), lambda b, i: (0, 0))],
        out_specs=[out_spec, out_spec, out_spec,
                   pl.BlockSpec((tb, LANES), lambda b, i: (b * nb + i, 0))],
        out_shape=[jax.ShapeDtypeStruct((n, 512), BF16)] * 3 + [jax.ShapeDtypeStruct((n, LANES), F32)],
        scratch_shapes=[pltpu.VMEM((tb + 2 * SUBLANES, wq), F32)],
        compiler_params=_params("parallel", "parallel"), name="gdn_pre",
    )(proj, proj, proj, proj, conv_w, a_log_row, dt_bias_row)


def _gdn_scan_kernel(qf, kf, vf, gf, qb, kb, vb, gb, of, ob, state):
    n = pl.program_id(1)

    @pl.when(n == 0)
    def _():
        state[...] = jnp.zeros_like(state)

    c = GDN_CHUNK
    nh = GDN_HEADS
    hc = nh * c
    wide = nh * GDN_DK
    row = lax.broadcasted_iota(jnp.int32, (c, c), 0)
    col = lax.broadcasted_iota(jnp.int32, (c, c), 1)
    lane = lax.broadcasted_iota(jnp.int32, (c, LANES), 1)
    rr = lax.broadcasted_iota(jnp.int32, (hc, hc), 0)
    cc = lax.broadcasted_iota(jnp.int32, (hc, hc), 1)
    same = (rr // c) == (cc // c)
    head_lanes = (lax.broadcasted_iota(jnp.int32, (hc, wide), 0) // c
                  == lax.broadcasted_iota(jnp.int32, (hc, wide), 1) // GDN_DK)
    lane_blk = lax.broadcasted_iota(jnp.int32, (1, hc), 1) // c
    row_cat = lax.broadcasted_iota(jnp.int32, (c, hc), 0)
    lane_cat = lax.broadcasted_iota(jnp.int32, (c, hc), 1)
    col_cat = lane_cat % c
    blk_cat = lane_cat // c
    eye_cat = (col_cat == row_cat).astype(F32)

    def col_cat_of(x, first):
        out = jnp.zeros((c, hc), F32)
        for h in range(nh):
            out = jnp.where(blk_cat == h, _col(x, first + h, lane), out)
        return out

    def stack_wide(x):
        return jnp.where(head_lanes, jnp.concatenate([x] * nh, axis=0), 0.0)

    def stack_narrow(x):
        return jnp.concatenate([x[:, h * GDN_DK:(h + 1) * GDN_DK] for h in range(nh)], axis=0)

    dirs = ((qf, kf, vf, gf, of), (qb, kb, vb, gb, ob))

    def chain(bb, d, rows):
        q_ref, k_ref, v_ref, g_ref, o_ref = dirs[d]
        incl1 = (col <= row) if d == 0 else (col >= row)
        incl = (col_cat <= row_cat) if d == 0 else (col_cat >= row_cat)
        strict = (col_cat < row_cat) if d == 0 else (col_cat > row_cat)
        gates = g_ref[bb, rows, :]
        tri = incl1.astype(BF16)
        gam_all = _sel_mm(tri, gates)
        tri4 = jnp.concatenate([tri] * nh, axis=0)
        gam_rows = _sel_mm(tri4, gates.T[:2 * nh], _NT, sel_first=False)
        yield
        gam = jnp.concatenate([_col(gam_all, d * nh + h, lane) for h in range(nh)], axis=0)
        beta = jnp.concatenate([_col(gates, 2 * nh + h, lane) for h in range(nh)], axis=0)
        gam_row = jnp.zeros((1, hc), F32)
        for h in range(nh):
            gam_row = jnp.where(lane_blk == h, gam_rows[d * nh + h:d * nh + h + 1, :], gam_row)
        last = c - 1 if d == 0 else 0
        g_last = jnp.concatenate(
            [jnp.broadcast_to(gam[h * c + last:h * c + last + 1, :], (c, 1)) for h in range(nh)], axis=0)
        g_last_s = jnp.concatenate(
            [jnp.broadcast_to(gam[h * c + last:h * c + last + 1, :], (GDN_DK, 1)) for h in range(nh)],
            axis=0)
        diff = col_cat_of(gam_all, d * nh) - gam_row
        decay = jnp.where(incl, jnp.exp(jnp.where(incl, diff, 0.0)), 0.0)
        e_gam = jnp.exp(gam)
        k_in, q_in = k_ref[bb, rows, :], q_ref[bb, rows, :]
        k_w = stack_wide(k_in)
        kq = _mm(jnp.concatenate([k_in, q_in], axis=0), k_w, _NT)
        yield
        a_cat = jnp.where(strict, kq[:c] * decay * col_cat_of(gates, 2 * nh), 0.0)
        attn = _expand_bd(kq[c:] * decay, same)
        t_cat = yield from _unit_tri_inv_cat(a_cat, eye_cat, same, c)
        yield
        rhs = jnp.concatenate([stack_narrow(v_ref[bb, rows, :]) * beta,
                               stack_narrow(k_in) * (beta * e_gam)], axis=1)
        sol = _mm(_expand_bd(t_cat, same), rhs)
        yield
        u, w = sol[:, :GDN_DK], sol[:, GDN_DK:]
        s = state[bb, d]
        w_w = jnp.where(head_lanes, jnp.concatenate([w] * nh, axis=1), 0.0)
        ws = _mm(jnp.concatenate([w_w, stack_wide(q_in) * e_gam], axis=0), s)
        yield
        v_new = u - ws[:hc]
        out = ws[hc:] + _mm(attn, v_new)
        o_ref[bb, rows, :] = jnp.concatenate([out[h * c:(h + 1) * c] for h in range(nh)], axis=1)
        k_dec = k_w * jnp.exp(g_last - gam)
        state[bb, d] = s * jnp.exp(g_last_s) + _mm(k_dec, v_new, _TN)

    def step(j, carry):
        def rows(d):
            return pl.ds(pl.multiple_of((j if d == 0 else SCAN_CHUNKS - 1 - j) * c, c), c)

        _interleave([chain(bb, d, rows(d)) for bb in range(qf.shape[0]) for d in range(2)])
        return carry

    lax.fori_loop(0, SCAN_CHUNKS, step, 0)


def _gdn_scan(q, k, v, gates, bsz, t_len, bb=SCAN_BATCH):
    n = q.shape[0]
    c = GDN_CHUNK * SCAN_CHUNKS
    nc = t_len // c
    bb = math.gcd(bb, bsz)
    q, k, v, gates = (a.reshape(bsz, t_len, a.shape[1]) for a in (q, k, v, gates))

    def fwd(g, i):
        return (g, i, 0)

    def bwd(g, i):
        return (g, nc - 1 - i, 0)

    def specs(idx):
        wide = pl.BlockSpec((bb, c, 512), idx)
        return [wide, wide, wide, pl.BlockSpec((bb, c, LANES), idx)]

    of, ob = pl.pallas_call(
        _gdn_scan_kernel, grid=(bsz // bb, nc),
        in_specs=specs(fwd) + specs(bwd),
        out_specs=[pl.BlockSpec((bb, c, 512), fwd), pl.BlockSpec((bb, c, 512), bwd)],
        out_shape=[jax.ShapeDtypeStruct((bsz, t_len, 512), F32)] * 2,
        scratch_shapes=[pltpu.VMEM((bb, 2, GDN_HEADS * GDN_DK, GDN_DK), F32)],
        compiler_params=_params("parallel", "arbitrary"), name="gdn_scan",
    )(q, k, v, gates, q, k, v, gates)
    return of.reshape(n, 512), ob.reshape(n, 512)


_fft_mm = _mm
HY_CB = 16
HY_CHAINS = 2


def _hy_filter_kernel(z_ref, w1_ref, b1_ref, w2_ref, b2_ref, w3_ref, b3_ref, fr_ref, w4t_ref,
                      dl_ref, o_ref, *, t_len):
    i = pl.program_id(0)
    tl = o_ref.shape[1]
    fr = fr_ref[...]
    h = jnp.sin(fr * (_mm3(z_ref[...], w1_ref[...]) + b1_ref[...]))
    h = jnp.sin(fr * (_mm3(h, w2_ref[...]) + b2_ref[...]))
    h = jnp.sin(fr * (_mm3(h, w3_ref[...]) + b3_ref[...]))
    ht = _mm3(w4t_ref[...], h, _NT)
    pos = (lax.broadcasted_iota(jnp.int32, (1, tl), 1) + i * tl).astype(F32)
    t_norm = pos / float(max(t_len - 1, 1))
    o_ref[...] = ht * jnp.exp(-t_norm * dl_ref[...])


def _hy_filters(z, w1p, b1, w2, b2, w3, b3, freq, w4t, deltas, t_len):
    tl = min(512, t_len)
    nch = w4t.shape[0]
    full = lambda a: pl.BlockSpec(a.shape, lambda i: (0,) * a.ndim)
    return pl.pallas_call(
        functools.partial(_hy_filter_kernel, t_len=t_len), grid=(t_len // tl,),
        in_specs=[pl.BlockSpec((tl, z.shape[1]), lambda i: (i, 0))]
        + [full(a) for a in (w1p, b1, w2, b2, w3, b3, freq, w4t, deltas)],
        out_specs=pl.BlockSpec((nch, tl), lambda i: (0, i)),
        out_shape=jax.ShapeDtypeStruct((nch, t_len), F32),
        compiler_params=_params("parallel"), name="hy_filter",
    )(z, w1p, b1, w2, b2, w3, b3, freq, w4t, deltas)


def _fft_fwd(x3, x3i, fst, twr, twi, wbig):
    cb, r = x3.shape[0], x3.shape[1]
    n1 = twr.shape[0]
    xcat = jnp.concatenate([x3[c] for c in range(cb)], axis=1)
    if x3i is None:
        y = _fft_mm(fst[:, :r], xcat)
    else:
        xicat = jnp.concatenate([x3i[c] for c in range(cb)], axis=1)
        y = _fft_mm(fst, jnp.concatenate([xcat, xicat], axis=0))
    yield
    yr, yi = y[:n1], y[n1:]
    twr_t = jnp.concatenate([twr] * cb, axis=1)
    twi_t = jnp.concatenate([twi] * cb, axis=1)
    tr = yr * twr_t - yi * twi_t
    ti = yr * twi_t + yi * twr_t
    cat = jnp.concatenate(
        [jnp.concatenate([tr[:, c * LANES:(c + 1) * LANES], ti[:, c * LANES:(c + 1) * LANES]], axis=1)
         for c in range(cb)], axis=0)
    z = _fft_mm(cat, wbig)
    yield
    return z[:, :LANES].reshape(cb, n1, LANES), z[:, LANES:].reshape(cb, n1, LANES)


def _fft_inv(zr, zi, gst, twr, twi, wbigc):
    cb, n1 = zr.shape[0], zr.shape[1]
    r = gst.shape[0] // 2
    cat = jnp.concatenate([zr, zi], axis=-1).reshape(cb * n1, 2 * LANES)
    a = _fft_mm(cat, wbigc)
    yield
    ar, ai = a[:, :LANES].reshape(cb, n1, LANES), a[:, LANES:].reshape(cb, n1, LANES)
    tr = ar * twr + ai * twi
    ti = ai * twr - ar * twi
    tt = jnp.concatenate([jnp.concatenate([tr[c] for c in range(cb)], axis=1),
                          jnp.concatenate([ti[c] for c in range(cb)], axis=1)], axis=0)
    out = _fft_mm(gst, tt)
    yield
    return (jnp.stack([out[:r, c * LANES:(c + 1) * LANES] for c in range(cb)]),
            jnp.stack([out[r:, c * LANES:(c + 1) * LANES] for c in range(cb)]))


def _hy_spec_kernel(hf_ref, hb_ref, fst_ref, twr_ref, twi_ref, wbig_ref, hr_ref, hi_ref):
    args = (fst_ref[...], twr_ref[...], twi_ref[...], wbig_ref[...])
    out = {}

    def chain(name, ref):
        out[name] = yield from _fft_fwd(ref[...], None, *args)

    _interleave([chain("f", hf_ref), chain("b", hb_ref)])
    hr_ref[...] = out["f"][0] + out["b"][0]
    hi_ref[...] = out["f"][1] - out["b"][1]


def _hy_spectra(filt3, consts, cb=2 * HY_CB):
    fst, gst, twr, twi, wbig, wbigc = consts
    r = filt3.shape[1]
    n1 = twr.shape[0]
    nj = HY_DIM // cb
    full = lambda a: pl.BlockSpec(a.shape, lambda o, j: (0,) * a.ndim)
    out_spec = pl.BlockSpec((cb, n1, LANES), lambda o, j: (o * nj + j, 0, 0))
    return pl.pallas_call(
        _hy_spec_kernel, grid=(2, nj),
        in_specs=[pl.BlockSpec((cb, r, LANES), lambda o, j: (2 * o * nj + j, 0, 0)),
                  pl.BlockSpec((cb, r, LANES), lambda o, j: ((2 * o + 1) * nj + j, 0, 0)),
                  full(fst), full(twr), full(twi), full(wbig)],
        out_specs=[out_spec, out_spec],
        out_shape=[jax.ShapeDtypeStruct((2 * HY_DIM, n1, LANES), F32)] * 2,
        compiler_params=_params("parallel", "parallel"), name="hy_spec",
    )(filt3, filt3, fst, twr, twi, wbig)


def _tshift(x3, back, lane, rowi):
    r = x3.shape[1]
    if back:
        rl = pltpu.roll(x3, 1, 2)
        rr = pltpu.roll(rl, 1, 1)
        edge = lane == 0
        out = jnp.where(edge, rr, rl)
        return jnp.where(edge & (rowi == 0), 0.0, out)
    rl = pltpu.roll(x3, LANES - 1, 2)
    rr = pltpu.roll(rl, r - 1, 1)
    edge = lane == LANES - 1
    out = jnp.where(edge, rr, rl)
    return jnp.where(edge & (rowi == r - 1), 0.0, out)


def _hy_conv_kernel(x1_ref, x2_ref, v_ref, p1_ref, p2_ref, pv_ref, sk_ref, h1r_ref, h1i_ref,
                    h2r_ref, h2i_ref, fst_ref, gst_ref, twr_ref, twi_ref, wbig_ref, wbigc_ref,
                    o_ref):
    cb, nb, r, _ = x1_ref.shape
    half = cb // HY_CHAINS
    shape = (half, r, LANES)
    lane = lax.broadcasted_iota(jnp.int32, shape, 2)
    rowi = lax.broadcasted_iota(jnp.int32, shape, 1)
    fwd = (fst_ref[...], twr_ref[...], twi_ref[...], wbig_ref[...])
    inv = (gst_ref[...], twr_ref[...], twi_ref[...], wbigc_ref[...])

    def chain(lo):
        ch = pl.ds(lo, half)

        def short_conv(x_ref, p_ref):
            p = p_ref[ch]
            out = []
            for b in range(nb):
                x = x_ref[ch, b].astype(F32)
                out.append(p[:, 0:1, :] * _tshift(x, True, lane, rowi) + p[:, 1:2, :] * x
                           + p[:, 2:3, :] * _tshift(x, False, lane, rowi) + p[:, 3:4, :])
            return out

        x1 = short_conv(x1_ref, p1_ref)
        x2 = short_conv(x2_ref, p2_ref)
        z = short_conv(v_ref, pv_ref)
        sk = sk_ref[ch]
        for o, (gate, hr_ref, hi_ref) in enumerate(((x1, h1r_ref, h1i_ref), (x2, h2r_ref, h2i_ref))):
            zr, zi = yield from _fft_fwd(z[0], z[1], *fwd)
            hr, hi = hr_ref[ch], hi_ref[ch]
            y = yield from _fft_inv(zr * hr - zi * hi, zr * hi + zi * hr, *inv)
            z = [gate[b] * (y[b] + sk[:, o:o + 1, :] * z[b]) for b in range(nb)]
        for b in range(nb):
            o_ref[ch, b] = z[b].astype(o_ref.dtype)

    _interleave([chain(i * half) for i in range(HY_CHAINS)])


def _hy_conv(projt4, pp, ps, hr, hi, consts, cb=HY_CB):
    fst, gst, twr, twi, wbig, wbigc = consts
    _, bsz, r, _ = projt4.shape
    n1 = twr.shape[0]
    nj = HY_DIM // cb
    full = lambda a: pl.BlockSpec(a.shape, lambda j, b: (0,) * a.ndim)

    assert bsz % 2 == 0, "sequences are transformed in pairs"

    def xs(off):
        return pl.BlockSpec((cb, 2, r, LANES), lambda j, b: (off * nj + j, b, 0, 0))

    def ps_(off):
        return pl.BlockSpec((cb, SUBLANES, LANES), lambda j, b: (off * nj + j, 0, 0))

    def hs(off):
        return pl.BlockSpec((cb, n1, LANES), lambda j, b: (off * nj + j, 0, 0))

    return pl.pallas_call(
        _hy_conv_kernel, grid=(nj, bsz // 2),
        in_specs=[xs(0), xs(1), xs(2), ps_(0), ps_(1), ps_(2), ps_(0), hs(0), hs(0), hs(1), hs(1),
                  full(fst), full(gst), full(twr), full(twi), full(wbig), full(wbigc)],
        out_specs=pl.BlockSpec((cb, 2, r, LANES), lambda j, b: (j, b, 0, 0)),
        out_shape=jax.ShapeDtypeStruct((HY_DIM, bsz, r, LANES), BF16),
        compiler_params=_params("parallel", "arbitrary"), name="hy_conv",
    )(projt4, projt4, projt4, pp, pp, pp, ps, hr, hi, hr, hi, fst, gst, twr, twi, wbig, wbigc)


def _fft_consts(t_len):
    n = 2 * t_len
    n1 = n // LANES
    r = t_len // LANES
    k1 = np.arange(n1)[:, None]
    th1 = 2.0 * np.pi * k1 * np.arange(r)[None, :] / n1
    c1, s1 = np.cos(th1), np.sin(th1)
    fst = np.block([[c1, s1], [-s1, c1]])
    gst = np.block([[c1.T, -s1.T], [s1.T, c1.T]]) / n
    tht = 2.0 * np.pi * k1 * np.arange(LANES)[None, :] / n
    twr, twi = np.cos(tht), -np.sin(tht)
    th2 = 2.0 * np.pi * np.arange(LANES)[:, None] * np.arange(LANES)[None, :] / LANES
    cr, ci = np.cos(th2), -np.sin(th2)
    wbig = np.block([[cr, ci], [-ci, cr]])
    wbigc = np.block([[cr, -ci], [ci, cr]])
    return tuple(jnp.asarray(a, F32) for a in (fst, gst, twr, twi, wbig, wbigc))


def _hy_positions(t_len):
    pos = np.arange(t_len, dtype=np.float32)[:, None]
    t = pos / np.float32(max(t_len - 1, 1))
    bands = np.linspace(1e-4, HY_BANDS - 1, HY_BANDS, dtype=np.float32)[None, :]
    ang = bands * np.float32(2.0 * math.pi / t_len) * pos
    z = np.concatenate([t, np.cos(ang), -np.sin(ang)], -1).astype(np.float32)
    return jnp.asarray(np.pad(z, ((0, 0), (0, HY_FILT_HID - HY_EMB))))


def _hy_deltas():
    max_decay = math.log(HY_DECAY_TARGET) / HY_FAST_DECAY_PCT
    min_decay = math.log(HY_DECAY_TARGET) / HY_SLOW_DECAY_PCT
    d = np.abs(np.linspace(min_decay, max_decay, HY_DIM, dtype=np.float32))
    return jnp.asarray(np.tile(d, 4)[:, None])


def _lane_rep(a, rows=SUBLANES):
    ch, k = a.shape
    a = jnp.pad(a, ((0, 0), (0, rows - k)))
    return jnp.broadcast_to(a[:, :, None], (ch, rows, LANES))


def _out_even_kernel(x_ref, yat_ref, of_ref, ob_ref, zg_ref, nw_ref, wa_ref, wb_ref, g_ref, b_ref,
                     w1_ref, w2_ref, g2_ref, b2_ref, o_ref):
    mix = _mm(yat_ref[...], wa_ref[...], _TN)
    nw = nw_ref[...]
    for h in range(GDN_HEADS):
        sl = slice(h * GDN_DK, (h + 1) * GDN_DK)
        o = of_ref[:, sl] + ob_ref[:, sl]
        y = o * lax.rsqrt(jnp.mean(o * o, -1, keepdims=True) + RMS_EPS) * nw * _silu(zg_ref[:, sl])
        mix = mix + _mm(y, wb_ref[sl, :])
    x1 = _layer_norm(DEEPNORM_ALPHA * x_ref[...] + mix, g_ref[...], b_ref[...])
    o_ref[...] = _mlp_tail(x1, w1_ref, w2_ref, g2_ref, b2_ref)


def _out_even(x2, yat, of, ob, projm, norm_w, wa, wb, g, b, mlp, tm=512):
    n, d = x2.shape
    tm = min(tm, n)
    row = lambda w: pl.BlockSpec((tm, w), lambda i: (i, 0))
    consts = (norm_w, wa, wb, g, b) + tuple(mlp)
    return pl.pallas_call(
        _out_even_kernel, grid=(n // tm,),
        in_specs=[row(d), pl.BlockSpec((HY_DIM, tm), lambda i: (0, i)), row(512), row(512),
                  pl.BlockSpec((tm, 512), lambda i: (i, 3))] + [_resident(a) for a in consts],
        out_specs=row(d), out_shape=jax.ShapeDtypeStruct((n, d), F32),
        compiler_params=_params("parallel"), name="out_even",
    )(x2, yat, of, ob, projm, *consts)


def _even_layer(x2, bsz, t_len, w_in, hy_conv_w, hy_conv_b, f_w1, f_b1, f_w2, f_b2, f_w3, f_b3,
                f_w4, f_freq, hy_skip, gdn_conv_w, gdn_a_log, gdn_dt_bias, gdn_norm_w, w_out,
                ln_g, ln_b, mlp):
    d = x2.shape[1]
    hyw = 3 * HY_DIM
    wm = jnp.concatenate([w_in[:, hyw:], jnp.zeros((d, LANES - 12), F32)], axis=1).astype(BF16)
    wt = w_in[:, :hyw].T.astype(BF16)
    projm, projt = _project(x2, wm, wt)
    consts = _fft_consts(t_len)
    w1p = jnp.pad(f_w1, ((0, HY_FILT_HID - HY_EMB), (0, 0)))
    filt = _hy_filters(_hy_positions(t_len), w1p, f_b1[None], f_w2, f_b2[None], f_w3, f_b3[None],
                       f_freq[None], f_w4.T, _hy_deltas(), t_len)
    r = t_len // LANES
    hr, hi = _hy_spectra(filt.reshape(4 * HY_DIM, r, LANES), consts)
    pp = _lane_rep(jnp.concatenate([hy_conv_w.T, hy_conv_b[:, None]], axis=1))
    ps = _lane_rep(hy_skip.T)
    yat = _hy_conv(projt.reshape(hyw, bsz, r, LANES), pp, ps, hr, hi, consts)
    yat = yat.reshape(HY_DIM, bsz * t_len)
    alog = jnp.zeros((1, LANES), F32).at[0, :8].set(gdn_a_log.reshape(-1))
    dtb = jnp.zeros((1, LANES), F32).at[0, :8].set(gdn_dt_bias.reshape(-1))
    qn, kn, vn, gates = _gdn_pre(projm, gdn_conv_w, alog, dtb, bsz, t_len)
    of, ob = _gdn_scan(qn, kn, vn, gates, bsz, t_len)
    return _out_even(x2, yat, of, ob, projm, gdn_norm_w[None], w_out[:HY_DIM].astype(BF16),
                     w_out[HY_DIM:].astype(BF16), ln_g[None], ln_b[None], mlp)


def _hg_scan_kernel(qf_ref, ff_ref, vf_ref, qb_ref, fb_ref, vb_ref, low_ref, of_ref, ob_ref,
                    state, q_s, k_s, g_s, *, layer):
    n = pl.program_id(1)

    @pl.when(n == 0)
    def _():
        state[...] = jnp.zeros_like(state)

    tb = qf_ref.shape[0]
    c = HG_CHUNK
    nsub = tb // c
    low = low_ref[...]
    e = jnp.exp(low - jnp.max(low, axis=0, keepdims=True))
    p = e / jnp.sum(e, axis=0, keepdims=True)
    lb = jnp.sum(p[1:layer + 1], axis=0, keepdims=True)
    rb = lax.broadcasted_iota(jnp.int32, (tb, tb), 0)
    cb = lax.broadcasted_iota(jnp.int32, (tb, tb), 1)
    same = (rb // c) == (cb // c)
    hb = c // 2
    rowh = lax.broadcasted_iota(jnp.int32, (hb, 1), 0)
    dirs = ((qf_ref, ff_ref, vf_ref, of_ref), (qb_ref, fb_ref, vb_ref, ob_ref))
    for d, (q_ref, f_ref, v_ref, o_ref) in enumerate(dirs):
        fg = lb + (1.0 - lb) * _sigmoid(f_ref[...])
        tri = (same & ((cb <= rb) if d == 0 else (cb >= rb))).astype(BF16)
        g_s[d] = _sel_mm(tri, jnp.log(fg))
        k_s[d] = 1.0 - fg
        q_s[d] = _silu(q_ref[...])

    def body(s, carry):
        def sub_rows(d, u):
            idx = s * HG_SUBS + u
            sub = idx if d == 0 else nsub - 1 - idx
            return pl.ds(pl.multiple_of(sub * c, c), c)

        def pairwise(d, qb, gkb, gb, vb):
            acc = None
            for j in range(hb):
                dec = jnp.exp(gb - gkb[j:j + 1, :])
                a_col = jnp.sum(qb * dec, axis=1, keepdims=True)
                a_col = jnp.where((rowh >= j) if d == 0 else (rowh <= j), a_col, 0.0)
                term = a_col * vb[j:j + 1, :]
                acc = term if acc is None else acc + term
            return acc

        def chain(d, h):
            v_ref, o_ref = dirs[d][2], dirs[d][3]
            sl = slice(h * HG_DK, (h + 1) * HG_DK)
            st = state[d, h]
            pending = []
            for u in range(HG_SUBS):
                rows = sub_rows(d, u)
                qc, kc, gc, vc = q_s[d, rows, sl], k_s[d, rows, sl], g_s[d, rows, sl], v_ref[rows, sl]
                g_last = gc[c - 1:c, :] if d == 0 else gc[0:1, :]
                inter = _mm(qc * jnp.exp(gc), st, _NT)
                upd = _mm(vc, kc * jnp.exp(g_last - gc), _TN)
                yield
                st = st * jnp.exp(g_last) + upd
                pending.append((rows, qc, kc, gc, vc, inter))
            state[d, h] = st
            lo, hi = slice(0, hb), slice(hb, c)
            src, dst, bnd = (lo, hi, hb - 1) if d == 0 else (hi, lo, hb)
            for rows, qc, kc, gc, vc, inter in pending:
                g_bnd = gc[bnd:bnd + 1, :]
                q_x = qc[dst] * jnp.exp(gc[dst] - g_bnd)
                k_y = kc[src] * jnp.exp(g_bnd - gc[src])
                v_src = vc[src]
                cross = None
                for j in range(hb):
                    term = jnp.sum(q_x * k_y[j:j + 1, :], axis=1, keepdims=True) * v_src[j:j + 1, :]
                    cross = term if cross is None else cross + term
                gk = gc - jnp.log(kc)
                o_lo = pairwise(d, qc[lo], gk[lo], gc[lo], vc[lo])
                o_hi = pairwise(d, qc[hi], gk[hi], gc[hi], vc[hi])
                if d == 0:
                    o_hi = o_hi + cross
                else:
                    o_lo = o_lo + cross
                o_ref[rows, sl] = inter + jnp.concatenate([o_lo, o_hi], axis=0)

        _interleave([chain(d, h) for d in range(2) for h in range(HG_HEADS)])
        return carry

    lax.fori_loop(0, nsub // HG_SUBS, body, 0)


def _hg_scan(proj, hg_lower, layer, bsz, t_len, tb=256):
    n = proj.shape[0]
    nb = t_len // tb

    def spec(colb, rev):
        if rev:
            return pl.BlockSpec((tb, 512), lambda b, i: (b * nb + nb - 1 - i, colb))
        return pl.BlockSpec((tb, 512), lambda b, i: (b * nb + i, colb))

    return pl.pallas_call(
        functools.partial(_hg_scan_kernel, layer=layer), grid=(bsz, nb),
        in_specs=[spec(0, False), spec(1, False), spec(3, False),
                  spec(0, True), spec(2, True), spec(3, True),
                  pl.BlockSpec(hg_lower.shape, lambda b, i: (0, 0))],
        out_specs=[spec(0, False), spec(0, True)],
        out_shape=[jax.ShapeDtypeStruct((n, 512), F32)] * 2,
        scratch_shapes=[pltpu.VMEM((2, HG_HEADS, HG_DK, HG_DK), F32)]
        + [pltpu.VMEM((2, tb, 512), F32)] * 3,
        compiler_params=_params("parallel", "arbitrary"), name="hg_scan",
    )(proj, proj, proj, proj, proj, proj, hg_lower)


def _rw_pre_kernel(rc, rp, rn, kc, kp, kn, vc, vp, vn, lc, lp, ln, mu_ref, w0_ref, w2_ref,
                   a0_ref, a2_ref, g2_ref, kk_ref, ka_ref, bd_ref,
                   r_o, k_o, v_o, kk_o, kka_o, lwf_o, lwb_o, g_o, scr):
    i = pl.program_id(1)
    first = i == 0
    last = i == pl.num_programs(1) - 1
    tb = rc.shape[0]

    def shifted(cur_ref, prev_ref, next_ref, col0):
        w = cur_ref.shape[1]
        cur = cur_ref[...]
        scr[pl.ds(SUBLANES, tb), pl.ds(0, w)] = cur
        scr[pl.ds(0, SUBLANES), pl.ds(0, w)] = jnp.where(first, 0.0, prev_ref[...])
        scr[pl.ds(SUBLANES + tb, SUBLANES), pl.ds(0, w)] = jnp.where(last, 0.0, next_ref[...])
        prev = scr[pl.ds(SUBLANES - 1, tb), pl.ds(0, w)]
        nxt = scr[pl.ds(SUBLANES + 1, tb), pl.ds(0, w)]
        mu = mu_ref[:, col0:col0 + w]
        return cur + mu[0:1] * (prev - cur) + mu[1:2] * (nxt - cur)

    r = shifted(rc, rp, rn, 0)
    k = shifted(kc, kp, kn, RW_DIM)
    v = shifted(vc, vp, vn, 2 * RW_DIM)
    lo = shifted(lc, lp, ln, 3 * RW_DIM)
    lo_wa, lo_g = lo[:, :LANES], lo[:, LANES:]
    th = jnp.tanh(lo_wa)
    for d, o_ref in enumerate((lwf_o, lwb_o)):
        w_raw = w0_ref[d:d + 1, :] + _mm(th, w2_ref[d])
        o_ref[...] = -math.exp(-0.5) * _sigmoid(w_raw)
    a = _sigmoid(a0_ref[...] + _mm(lo_wa, a2_ref[...]))
    g_o[...] = _mm(_sigmoid(lo_g), g2_ref[...]).astype(g_o.dtype)
    kx = k * kk_ref[...]
    kk = kx * lax.rsqrt(_mm(kx * kx, bd_ref[...]) + L2_EPS)
    r_o[...] = r.astype(r_o.dtype)
    v_o[...] = v.astype(v_o.dtype)
    k_o[...] = (k * (1.0 + (a - 1.0) * ka_ref[...])).astype(k_o.dtype)
    kk_o[...] = kk.astype(kk_o.dtype)
    kka_o[...] = (kk * a).astype(kka_o.dtype)


def _rw_pre(proj, mu, w0, w2p, a0, a2p, g2, k_k, k_a, bd, bsz, t_len, tb=512):
    n = proj.shape[0]
    nb = t_len // tb
    base = 2560
    specs = []
    for off, w in ((0, 512), (512, 512), (1024, 512), (1536, 256)):
        specs += _halo_specs(w, (base + off) // w, tb, t_len, bsz)
    full = lambda a: pl.BlockSpec(a.shape, lambda b, i: (0,) * a.ndim)
    out_spec = pl.BlockSpec((tb, RW_DIM), lambda b, i: (b * nb + i, 0))
    consts = (mu, w0, w2p, a0, a2p, g2, k_k, k_a, bd)
    return pl.pallas_call(
        _rw_pre_kernel, grid=(bsz, nb),
        in_specs=specs + [full(a) for a in consts],
        out_specs=[out_spec] * 8,
        out_shape=[jax.ShapeDtypeStruct((n, RW_DIM), dt) for dt in (BF16,) * 5 + (F32, F32, BF16)],
        scratch_shapes=[pltpu.VMEM((tb + 2 * SUBLANES, RW_DIM), F32)],
        compiler_params=_params("parallel", "parallel"), name="rw_pre",
    )(*([proj] * 12), *consts)


def _rw_scan_kernel(rf, kf, vf, bf, af, wf, rb, kb, vb, bb, ab, wb, yf, yb, state):
    n = pl.program_id(1)

    @pl.when(n == 0)
    def _():
        state[...] = jnp.zeros_like(state)

    c = RW_CHUNK
    gw = RW_GROUP * RW_HD
    gc = RW_GROUP * c
    row = lax.broadcasted_iota(jnp.int32, (c, c), 0)
    col = lax.broadcasted_iota(jnp.int32, (c, c), 1)
    rr = lax.broadcasted_iota(jnp.int32, (gc, gc), 0)
    cc = lax.broadcasted_iota(jnp.int32, (gc, gc), 1)
    same = (rr // c) == (cc // c)
    head_lanes = (lax.broadcasted_iota(jnp.int32, (gc, gw), 0) // c
                  == lax.broadcasted_iota(jnp.int32, (gc, gw), 1) // RW_HD)
    rs = lax.broadcasted_iota(jnp.int32, (gw, gw), 0)
    cs = lax.broadcasted_iota(jnp.int32, (gw, gw), 1)
    head_blocks = ((rs // RW_HD) == (cs // RW_HD)).astype(F32)
    row_cat = lax.broadcasted_iota(jnp.int32, (c, gc), 0)
    col_cat = lax.broadcasted_iota(jnp.int32, (c, gc), 1) % c
    eye_cat = (col_cat == row_cat).astype(F32)

    def stack(x):
        return jnp.where(head_lanes, jnp.concatenate([x] * RW_GROUP, axis=0), 0.0)

    dirs = ((rf, kf, vf, bf, af, wf, yf), (rb, kb, vb, bb, ab, wb, yb))
    def chain(ib, d, p, rows):
        r_ref, k_ref, v_ref, b_ref, a_ref, w_ref, y_ref = dirs[d]
        sl = slice(p * gw, (p + 1) * gw)
        incl = (col_cat <= row_cat) if d == 0 else (col_cat >= row_cat)
        strict = (col_cat < row_cat) if d == 0 else (col_cat > row_cat)
        lw = w_ref[ib, rows, sl]
        tri = ((col <= row) if d == 0 else (col >= row)).astype(BF16)
        g_in = _sel_mm(tri, lw)
        yield
        g_ex = g_in - lw
        g_end = g_in[c - 1:c, :] if d == 0 else g_in[0:1, :]
        e_neg = jnp.exp(-g_in)
        e_end = jnp.exp(g_end - g_in)
        k, a, v = k_ref[ib, rows, sl], a_ref[ib, rows, sl], v_ref[ib, rows, sl]
        lhs = jnp.concatenate([r_ref[ib, rows, sl] * jnp.exp(g_in), b_ref[ib, rows, sl] * jnp.exp(g_ex)],
                              axis=0)
        ak = _mm(lhs, stack(k * e_neg), _NT)
        yield
        aa = _mm(lhs, stack(a * e_neg), _NT)
        yield
        a_rk = jnp.where(incl, ak[:c], 0.0)
        a_ra = jnp.where(incl, aa[:c], 0.0)
        a_bk = jnp.where(strict, ak[c:], 0.0)
        a_ba = jnp.where(strict, aa[c:], 0.0)
        t_inv = yield from _unit_tri_inv_cat(a_ba, eye_cat, same, c)
        yield
        m = state[ib, d, p]
        xm = _mm(lhs, m, _NT)
        yield
        av = _mm(jnp.concatenate([a_rk, a_bk], axis=0), stack(v))
        yield
        u = _mm(t_inv, stack(xm[c:] + av[c:]))
        yield
        y_ref[ib, rows, sl] = xm[:c] + av[:c] - _mm(a_ra, stack(u))
        yield
        upd = _mm(jnp.concatenate([v.astype(F32), u], axis=0),
                  jnp.concatenate([k * e_end, -(a * e_end)], axis=0), _TN)
        state[ib, d, p] = m * jnp.exp(g_end) + upd * head_blocks

    def step(j, carry):
        def rows(d):
            return pl.ds(pl.multiple_of((j if d == 0 else SCAN_CHUNKS - 1 - j) * c, c), c)

        _interleave([chain(ib, d, p, rows(d)) for ib in range(rf.shape[0]) for d in range(2)
                     for p in range(RW_DIM // gw)])
        return carry

    lax.fori_loop(0, SCAN_CHUNKS, step, 0)


def _rw_scan(r, k, v, kk, kka, lwf, lwb, bsz, t_len, bb=SCAN_BATCH):
    n = r.shape[0]
    c = RW_CHUNK * SCAN_CHUNKS
    gw = RW_GROUP * RW_HD
    nc = t_len // c
    bb = math.gcd(bb, bsz)
    r, k, v, kk, kka, lwf, lwb =(a.reshape(bsz, t_len, RW_DIM) for a in (r, k, v, kk, kka, lwf, lwb))
    fwd = pl.BlockSpec((bb, c, RW_DIM), lambda g, i: (g, i, 0))
    bwd = pl.BlockSpec((bb, c, RW_DIM), lambda g, i: (g, nc - 1 - i, 0))
    yf, yb = pl.pallas_call(
        _rw_scan_kernel, grid=(bsz // bb, nc),
        in_specs=[fwd] * 6 + [bwd] * 6, out_specs=[fwd, bwd],
        out_shape=[jax.ShapeDtypeStruct((bsz, t_len, RW_DIM), F32)] * 2,
        scratch_shapes=[pltpu.VMEM((bb, 2, RW_DIM // gw, gw, gw), F32)],
        compiler_params=_params("parallel", "arbitrary"), name="rw_scan",
    )(r, k, v, kk, kka, lwf, r, k, v, kk, kka, lwb)
    return yf.reshape(n, RW_DIM), yb.reshape(n, RW_DIM)


def _out_odd_kernel(x_ref, hf_ref, hb_ref, hg_ref, hnw_ref, yf_ref, yb_ref, r_ref, k_ref, v_ref,
                    gate_ref, rk_ref, lw_ref, lb_ref, bd_ref, wc_ref, wd_ref, g_ref, b_ref,
                    w1_ref, w2_ref, g2_ref, b2_ref, o_ref):
    nw = hnw_ref[...]
    mix = None
    for h in range(HG_HEADS):
        sl = slice(h * HG_DK, (h + 1) * HG_DK)
        o = hf_ref[:, sl] + hb_ref[:, sl]
        y = o * lax.rsqrt(jnp.mean(o * o, -1, keepdims=True) + RMS_EPS) * nw * _silu(hg_ref[:, sl])
        part = _mm(y, wc_ref[sl, :])
        mix = part if mix is None else mix + part
    bd = bd_ref[...]
    y = yf_ref[...] + yb_ref[...]
    yc = y - _sel_mm(bd, y, sel_first=False) * (1.0 / RW_HD)
    var = _mm(yc * yc, bd) * (1.0 / RW_HD)
    yn = yc * lax.rsqrt(var + RW_GN_EPS) * lw_ref[...] + lb_ref[...]
    rk = r_ref[...].astype(F32) * k_ref[...] * rk_ref[...]
    bonus = _mm(rk, bd) * v_ref[...]
    mix = mix + _mm((yn + bonus) * gate_ref[...], wd_ref[...])
    x1 = _layer_norm(DEEPNORM_ALPHA * x_ref[...] + mix, g_ref[...], b_ref[...])
    o_ref[...] = _mlp_tail(x1, w1_ref, w2_ref, g2_ref, b2_ref)


def _out_odd(x2, hf, hb, proj, hnw, yf, yb, r, k, v, gate, r_k, lnx_w, lnx_b, bd, wc, wd, g, b,
             mlp, tm=512):
    n, d = x2.shape
    tm = min(tm, n)
    row = lambda w: pl.BlockSpec((tm, w), lambda i: (i, 0))
    consts = (r_k, lnx_w, lnx_b, bd, wc, wd, g, b) + tuple(mlp)
    return pl.pallas_call(
        _out_odd_kernel, grid=(n // tm,),
        in_specs=[row(d), row(512), row(512), pl.BlockSpec((tm, 512), lambda i: (i, 4)),
                  _resident(hnw)] + [row(512)] * 6 + [_resident(a) for a in consts],
        out_specs=row(d), out_shape=jax.ShapeDtypeStruct((n, d), F32),
        compiler_params=_params("parallel"), name="out_odd",
    )(x2, hf, hb, proj, hnw, yf, yb, r, k, v, gate, *consts)


def _odd_layer(x2, bsz, t_len, layer, w_in, hg_lower, hg_norm_w, rw_mu, rw_w0, rw_w2, rw_a0, rw_a2,
               rw_g2, rw_k_k, rw_k_a, rw_r_k, rw_lnx_w, rw_lnx_b, w_out, ln_g, ln_b, mlp):
    proj = _project(x2, w_in.astype(BF16))
    hf, hb = _hg_scan(proj, hg_lower, layer, bsz, t_len)
    idx = np.arange(RW_DIM) // RW_HD
    bd = jnp.asarray(idx[:, None] == idx[None, :], BF16)
    lora = rw_w2.shape[1]
    w2p = jnp.pad(rw_w2, ((0, 0), (0, LANES - lora), (0, 0)))
    a2p = jnp.pad(rw_a2, ((LANES - rw_a2.shape[0], 0), (0, 0)))
    r, k, v, kk, kka, lwf, lwb, gate = _rw_pre(proj, rw_mu, rw_w0, w2p, rw_a0[None], a2p, rw_g2,
                                               rw_k_k[None], rw_k_a[None], bd, bsz, t_len)
    yf, yb = _rw_scan(r, k, v, kk, kka, lwf, lwb, bsz, t_len)
    return _out_odd(x2, hf, hb, proj, hg_norm_w[None], yf, yb, r, k, v, gate,
                    rw_r_k.reshape(1, RW_DIM), rw_lnx_w[None], rw_lnx_b[None], bd,
                    w_out[:512].astype(BF16), w_out[512:].astype(BF16), ln_g[None], ln_b[None], mlp)


def kernel(x, ev_w_in, hy_conv_w, hy_conv_b, hy_filt_w1, hy_filt_b1, hy_filt_w2, hy_filt_b2, hy_filt_w3, hy_filt_b3, hy_filt_w4, hy_filt_freq, hy_skip, gdn_conv_w, gdn_a_log, gdn_dt_bias, gdn_norm_w, ev_w_out, od_w_in, hg_lower, hg_norm_w, rw_mu, rw_w0, rw_w2, rw_a0, rw_a2, rw_g2, rw_k_k, rw_k_a, rw_r_k, rw_lnx_w, rw_lnx_b, od_w_out, ln_g, ln_b, mlp_w1, mlp_w2):
    bsz, t_len, d = x.shape
    x2 = x.reshape(bsz * t_len, d)
    for layer in range(DEPTH):
        i = layer // 2
        mlp = (mlp_w1[layer].astype(BF16), mlp_w2[layer].astype(BF16), ln_g[layer, 1][None],
               ln_b[layer, 1][None])
        if layer % 2 == 0:
            x2 = _even_layer(x2, bsz, t_len, ev_w_in[i], hy_conv_w[i], hy_conv_b[i], hy_filt_w1[i],
                             hy_filt_b1[i], hy_filt_w2[i], hy_filt_b2[i], hy_filt_w3[i],
                             hy_filt_b3[i], hy_filt_w4[i], hy_filt_freq[i], hy_skip[i],
                             gdn_conv_w[i], gdn_a_log[i], gdn_dt_bias[i], gdn_norm_w[i],
                             ev_w_out[i], ln_g[layer, 0], ln_b[layer, 0], mlp)
        else:
            x2 = _odd_layer(x2, bsz, t_len, layer, od_w_in[i], hg_lower, hg_norm_w[i], rw_mu[i],
                            rw_w0[i], rw_w2[i], rw_a0[i], rw_a2[i], rw_g2[i], rw_k_k[i], rw_k_a[i],
                            rw_r_k[i], rw_lnx_w[i], rw_lnx_b[i], od_w_out[i], ln_g[layer, 0],
                            ln_b[layer, 0], mlp)
    return x2.reshape(bsz, t_len, d)
```

```python
import functools
import math

import numpy as np
import jax
import jax.numpy as jnp
from jax import lax
from jax.experimental import pallas as pl
from jax.experimental.pallas import tpu as pltpu

F32 = jnp.float32
BF16 = jnp.bfloat16

D_MODEL = 1024
DEPTH = 2
DEEPNORM_ALPHA = (2.0 * DEPTH) ** 0.25
LN_EPS = 1e-5
RMS_EPS = 1e-6
L2_EPS = 1e-6

HY_DIM = 512
HY_EMB = 33
HY_BANDS = 16
HY_FILT_HID = 64
HY_DECAY_TARGET = 1e-2
HY_FAST_DECAY_PCT = 0.3
HY_SLOW_DECAY_PCT = 1.5

GDN_HEADS = 4
GDN_DK = 128
GDN_CHUNK = 64

HG_HEADS = 4
HG_DK = 128
HG_CHUNK = 16
HG_SUBS = 8

RW_HD = 64
RW_DIM = 512
RW_CHUNK = 64
RW_GROUP = 4
RW_GN_EPS = 64e-5

SCAN_BATCH = 4
SCAN_CHUNKS = 2

LANES = 128
SUBLANES = 8
VMEM_LIMIT = 56 * 1024 * 1024

_NN = (((1,), (0,)), ((), ()))
_NT = (((1,), (1,)), ((), ()))
_TN = (((0,), (0,)), ((), ()))


def _mm(a, b, dims=_NN):
    return lax.dot_general(a.astype(BF16), b.astype(BF16), dims, preferred_element_type=F32)


def _params(*sem):
    return pltpu.CompilerParams(dimension_semantics=sem, vmem_limit_bytes=VMEM_LIMIT)


def _sigmoid(x):
    return 1.0 / (1.0 + jnp.exp(-x))


def _silu(x):
    return x * _sigmoid(x)


def _softplus(x):
    return jnp.maximum(x, 0.0) + jnp.log(1.0 + jnp.exp(-jnp.abs(x)))


def _layer_norm(v, g, b):
    mean = jnp.mean(v, axis=-1, keepdims=True)
    vc = v - mean
    var = jnp.mean(vc * vc, axis=-1, keepdims=True)
    return vc * lax.rsqrt(var + LN_EPS) * g + b


def _col(x, idx, lane):
    return jnp.sum(jnp.where(lane == idx, x, 0.0), axis=1, keepdims=True)


def _split2(a):
    hi = a.astype(BF16)
    return hi, (a - hi.astype(F32)).astype(BF16)


def _mm3(a, b, dims=_NN):
    ah, al = _split2(a)
    bh, bl = _split2(b)
    dot = functools.partial(lax.dot_general, dimension_numbers=dims, preferred_element_type=F32)
    return dot(ah, bh) + dot(ah, bl) + dot(al, bh)


def _sel_mm(sel, x, dims=_NN, sel_first=True):
    hi = x.astype(BF16)
    r1 = x - hi.astype(F32)
    mid = r1.astype(BF16)
    lo = (r1 - mid.astype(F32)).astype(BF16)
    dot = functools.partial(lax.dot_general, dimension_numbers=dims, preferred_element_type=F32)
    if sel_first:
        return dot(sel, hi) + dot(sel, mid) + dot(sel, lo)
    return dot(hi, sel) + dot(mid, sel) + dot(lo, sel)


def _expand_bd(x_cat, same):
    g = x_cat.shape[1] // x_cat.shape[0]
    return jnp.where(same, jnp.concatenate([x_cat] * g, axis=0), 0.0)


def _unit_tri_inv_cat(a_cat, eye_cat, same, depth):
    c = a_cat.shape[0]
    p = -a_cat
    t = eye_cat + p
    p = _mm(p, _expand_bd(p, same))
    yield
    k = 4
    while k < depth:
        both = _mm(jnp.concatenate([p, t], axis=0), _expand_bd(p, same))
        yield
        p, t = both[:c], t + both[c:]
        k *= 2
    return t + _mm(t, _expand_bd(p, same))


def _interleave(chains):
    active = list(chains)
    while active:
        still = []
        for g in active:
            try:
                next(g)
                still.append(g)
            except StopIteration:
                pass
        active = still


def _proj_kernel(x_ref, w_ref, o_ref):
    o_ref[...] = _mm(x_ref[...], w_ref[...])


def _proj_t_kernel(x_ref, w_ref, wt_ref, o_ref, ot_ref):
    xb = x_ref[...].astype(BF16)
    o_ref[...] = _mm(xb, w_ref[...])
    ot_ref[...] = _mm(wt_ref[...], xb, _NT).astype(ot_ref.dtype)


def _project(x2, w, wt=None, tm=512):
    n, d = x2.shape
    m = w.shape[1]
    grid = (n // tm,)
    once = pl.Buffered(1)
    x_spec = pl.BlockSpec((tm, d), lambda i: (i, 0))
    w_spec = pl.BlockSpec((d, m), lambda i: (0, 0), pipeline_mode=once)
    o_spec = pl.BlockSpec((tm, m), lambda i: (i, 0))
    if wt is None:
        return pl.pallas_call(
            _proj_kernel, grid=grid, in_specs=[x_spec, w_spec], out_specs=o_spec,
            out_shape=jax.ShapeDtypeStruct((n, m), F32), compiler_params=_params("parallel"),
            name="proj")(x2, w)
    mt = wt.shape[0]
    return pl.pallas_call(
        _proj_t_kernel, grid=grid,
        in_specs=[x_spec, w_spec, pl.BlockSpec((mt, d), lambda i: (0, 0), pipeline_mode=once)],
        out_specs=[o_spec, pl.BlockSpec((mt, tm), lambda i: (0, i))],
        out_shape=[jax.ShapeDtypeStruct((n, m), F32), jax.ShapeDtypeStruct((mt, n), BF16)],
        compiler_params=_params("parallel"), name="proj_t")(x2, w, wt)


def _mlp_tail(x, w1_ref, w2_ref, g_ref, b_ref):
    h = jnp.maximum(_mm(x, w1_ref[...]), 0.0)
    return _layer_norm(DEEPNORM_ALPHA * x + _mm(h * h, w2_ref[...]), g_ref[...], b_ref[...])


def _resident(a):
    return pl.BlockSpec(a.shape, lambda i: (0,) * a.ndim, pipeline_mode=pl.Buffered(1))


def _fill_halo(scr, cur, prev8, next8, first, last):
    tb = cur.shape[0]
    scr[pl.ds(SUBLANES, tb), :] = cur
    scr[pl.ds(0, SUBLANES), :] = jnp.where(first, 0.0, prev8)
    scr[pl.ds(SUBLANES + tb, SUBLANES), :] = jnp.where(last, 0.0, next8)


def _halo_specs(width, col_block, tb, t_len, bsz):
    nb = t_len // tb
    r8 = tb // SUBLANES
    tot8 = t_len // SUBLANES

    def cur(b, i):
        return (b * nb + i, col_block)

    def prev(b, i):
        return (jnp.maximum(b * tot8 + i * r8 - 1, 0), col_block)

    def nxt(b, i):
        return (jnp.minimum(b * tot8 + (i + 1) * r8, tot8 * bsz - 1), col_block)

    return [pl.BlockSpec((tb, width), cur), pl.BlockSpec((SUBLANES, width), prev),
            pl.BlockSpec((SUBLANES, width), nxt)]


def _gdn_pre_kernel(cur_ref, prev_ref, next_ref, ab_ref, cw_ref, alog_ref, dtb_ref,
                    q_ref, k_ref, v_ref, g_ref, scr):
    i = pl.program_id(1)
    tb = cur_ref.shape[0]
    _fill_halo(scr, cur_ref[...], prev_ref[...], next_ref[...], i == 0,
               i == pl.num_programs(1) - 1)
    width = cw_ref.shape[0]
    acc = None
    for j in range(width):
        term = scr[pl.ds(SUBLANES - width // 2 + j, tb), :] * cw_ref[pl.ds(j, 1), :]
        acc = term if acc is None else acc + term
    qkv = _silu(acc)
    hd = GDN_HEADS * GDN_DK
    for h in range(GDN_HEADS):
        sl = slice(h * GDN_DK, (h + 1) * GDN_DK)
        q = qkv[:, sl]
        q_ref[:, sl] = (q * lax.rsqrt(jnp.sum(q * q, -1, keepdims=True) + L2_EPS)
                        * GDN_DK ** -0.5).astype(q_ref.dtype)
        k = qkv[:, hd + h * GDN_DK: hd + (h + 1) * GDN_DK]
        k_ref[:, sl] = (k * lax.rsqrt(jnp.sum(k * k, -1, keepdims=True) + L2_EPS)).astype(k_ref.dtype)
    v_ref[...] = qkv[:, 2 * hd:].astype(v_ref.dtype)
    ab = ab_ref[...]
    lane = lax.broadcasted_iota(jnp.int32, ab.shape, 1)
    log_g = -jnp.exp(alog_ref[...]) * _softplus(ab + dtb_ref[...])
    g_ref[...] = jnp.where(lane < 2 * GDN_HEADS, log_g, _sigmoid(ab))


def _gdn_pre(proj, conv_w, a_log_row, dt_bias_row, bsz, t_len, tb=512):
    n = proj.shape[0]
    wq = 3 * GDN_HEADS * GDN_DK
    nb = t_len // tb
    specs = _halo_specs(wq, 0, tb, t_len, bsz)
    ab_col = (wq + GDN_HEADS * GDN_DK) // LANES
    out_spec = pl.BlockSpec((tb, 512), lambda b, i: (b * nb + i, 0))
    return pl.pallas_call(
        _gdn_pre_kernel, grid=(bsz, nb),
        in_specs=specs + [pl.BlockSpec((tb, LANES), lambda b, i: (b * nb + i, ab_col)),
                          pl.BlockSpec(conv_w.shape, lambda b, i: (0, 0)),
                          pl.BlockSpec((1, LANES), lambda b, i: (0, 0)),
                          pl.BlockSpec((1, LANES), lambda b, i: (0, 0))],
        out_specs=[out_spec, out_spec, out_spec,
                   pl.BlockSpec((tb, LANES), lambda b, i: (b * nb + i, 0))],
        out_shape=[jax.ShapeDtypeStruct((n, 512), BF16)] * 3 + [jax.ShapeDtypeStruct((n, LANES), F32)],
        scratch_shapes=[pltpu.VMEM((tb + 2 * SUBLANES, wq), F32)],
        compiler_params=_params("parallel", "parallel"), name="gdn_pre",
    )(proj, proj, proj, proj, conv_w, a_log_row, dt_bias_row)


def _gdn_scan_kernel(qf, kf, vf, gf, qb, kb, vb, gb, of, ob, state):
    n = pl.program_id(1)

    @pl.when(n == 0)
    def _():
        state[...] = jnp.zeros_like(state)

    c = GDN_CHUNK
    nh = GDN_HEADS
    hc = nh * c
    wide = nh * GDN_DK
    row = lax.broadcasted_iota(jnp.int32, (c, c), 0)
    col = lax.broadcasted_iota(jnp.int32, (c, c), 1)
    lane = lax.broadcasted_iota(jnp.int32, (c, LANES), 1)
    rr = lax.broadcasted_iota(jnp.int32, (hc, hc), 0)
    cc = lax.broadcasted_iota(jnp.int32, (hc, hc), 1)
    same = (rr // c) == (cc // c)
    head_lanes = (lax.broadcasted_iota(jnp.int32, (hc, wide), 0) // c
                  == lax.broadcasted_iota(jnp.int32, (hc, wide), 1) // GDN_DK)
    lane_blk = lax.broadcasted_iota(jnp.int32, (1, hc), 1) // c
    row_cat = lax.broadcasted_iota(jnp.int32, (c, hc), 0)
    lane_cat = lax.broadcasted_iota(jnp.int32, (c, hc), 1)
    col_cat = lane_cat % c
    blk_cat = lane_cat // c
    eye_cat = (col_cat == row_cat).astype(F32)

    def col_cat_of(x, first):
        out = jnp.zeros((c, hc), F32)
        for h in range(nh):
            out = jnp.where(blk_cat == h, _col(x, first + h, lane), out)
        return out

    def stack_wide(x):
        return jnp.where(head_lanes, jnp.concatenate([x] * nh, axis=0), 0.0)

    def stack_narrow(x):
        return jnp.concatenate([x[:, h * GDN_DK:(h + 1) * GDN_DK] for h in range(nh)], axis=0)

    dirs = ((qf, kf, vf, gf, of), (qb, kb, vb, gb, ob))

    def chain(bb, d, rows):
        q_ref, k_ref, v_ref, g_ref, o_ref = dirs[d]
        incl1 = (col <= row) if d == 0 else (col >= row)
        incl = (col_cat <= row_cat) if d == 0 else (col_cat >= row_cat)
        strict = (col_cat < row_cat) if d == 0 else (col_cat > row_cat)
        gates = g_ref[bb, rows, :]
        tri = incl1.astype(BF16)
        gam_all = _sel_mm(tri, gates)
        tri4 = jnp.concatenate([tri] * nh, axis=0)
        gam_rows = _sel_mm(tri4, gates.T[:2 * nh], _NT, sel_first=False)
        yield
        gam = jnp.concatenate([_col(gam_all, d * nh + h, lane) for h in range(nh)], axis=0)
        beta = jnp.concatenate([_col(gates, 2 * nh + h, lane) for h in range(nh)], axis=0)
        gam_row = jnp.zeros((1, hc), F32)
        for h in range(nh):
            gam_row = jnp.where(lane_blk == h, gam_rows[d * nh + h:d * nh + h + 1, :], gam_row)
        last = c - 1 if d == 0 else 0
        g_last = jnp.concatenate(
            [jnp.broadcast_to(gam[h * c + last:h * c + last + 1, :], (c, 1)) for h in range(nh)], axis=0)
        g_last_s = jnp.concatenate(
            [jnp.broadcast_to(gam[h * c + last:h * c + last + 1, :], (GDN_DK, 1)) for h in range(nh)],
            axis=0)
        diff = col_cat_of(gam_all, d * nh) - gam_row
        decay = jnp.where(incl, jnp.exp(jnp.where(incl, diff, 0.0)), 0.0)
        e_gam = jnp.exp(gam)
        k_in, q_in = k_ref[bb, rows, :], q_ref[bb, rows, :]
        k_w = stack_wide(k_in)
        kq = _mm(jnp.concatenate([k_in, q_in], axis=0), k_w, _NT)
        yield
        a_cat = jnp.where(strict, kq[:c] * decay * col_cat_of(gates, 2 * nh), 0.0)
        attn = _expand_bd(kq[c:] * decay, same)
        t_cat = yield from _unit_tri_inv_cat(a_cat, eye_cat, same, c)
        yield
        rhs = jnp.concatenate([stack_narrow(v_ref[bb, rows, :]) * beta,
                               stack_narrow(k_in) * (beta * e_gam)], axis=1)
        sol = _mm(_expand_bd(t_cat, same), rhs)
        yield
        u, w = sol[:, :GDN_DK], sol[:, GDN_DK:]
        s = state[bb, d]
        w_w = jnp.where(head_lanes, jnp.concatenate([w] * nh, axis=1), 0.0)
        ws = _mm(jnp.concatenate([w_w, stack_wide(q_in) * e_gam], axis=0), s)
        yield
        v_new = u - ws[:hc]
        out = ws[hc:] + _mm(attn, v_new)
        o_ref[bb, rows, :] = jnp.concatenate([out[h * c:(h + 1) * c] for h in range(nh)], axis=1)
        k_dec = k_w * jnp.exp(g_last - gam)
        state[bb, d] = s * jnp.exp(g_last_s) + _mm(k_dec, v_new, _TN)

    def step(j, carry):
        def rows(d):
            return pl.ds(pl.multiple_of((j if d == 0 else SCAN_CHUNKS - 1 - j) * c, c), c)

        _interleave([chain(bb, d, rows(d)) for bb in range(qf.shape[0]) for d in range(2)])
        return carry

    lax.fori_loop(0, SCAN_CHUNKS, step, 0)


def _gdn_scan(q, k, v, gates, bsz, t_len, bb=SCAN_BATCH):
    n = q.shape[0]
    c = GDN_CHUNK * SCAN_CHUNKS
    nc = t_len // c
    bb = math.gcd(bb, bsz)
    q, k, v, gates = (a.reshape(bsz, t_len, a.shape[1]) for a in (q, k, v, gates))

    def fwd(g, i):
        return (g, i, 0)

    def bwd(g, i):
        return (g, nc - 1 - i, 0)

    def specs(idx):
        wide = pl.BlockSpec((bb, c, 512), idx)
        return [wide, wide, wide, pl.BlockSpec((bb, c, LANES), idx)]

    of, ob = pl.pallas_call(
        _gdn_scan_kernel, grid=(bsz // bb, nc),
        in_specs=specs(fwd) + specs(bwd),
        out_specs=[pl.BlockSpec((bb, c, 512), fwd), pl.BlockSpec((bb, c, 512), bwd)],
        out_shape=[jax.ShapeDtypeStruct((bsz, t_len, 512), F32)] * 2,
        scratch_shapes=[pltpu.VMEM((bb, 2, GDN_HEADS * GDN_DK, GDN_DK), F32)],
        compiler_params=_params("parallel", "arbitrary"), name="gdn_scan",
    )(q, k, v, gates, q, k, v, gates)
    return of.reshape(n, 512), ob.reshape(n, 512)


_fft_mm = _mm
HY_CB = 16
HY_CHAINS = 2


def _hy_filter_kernel(z_ref, w1_ref, b1_ref, w2_ref, b2_ref, w3_ref, b3_ref, fr_ref, w4t_ref,
                      dl_ref, o_ref, *, t_len):
    i = pl.program_id(0)
    tl = o_ref.shape[1]
    fr = fr_ref[...]
    h = jnp.sin(fr * (_mm3(z_ref[...], w1_ref[...]) + b1_ref[...]))
    h = jnp.sin(fr * (_mm3(h, w2_ref[...]) + b2_ref[...]))
    h = jnp.sin(fr * (_mm3(h, w3_ref[...]) + b3_ref[...]))
    ht = _mm3(w4t_ref[...], h, _NT)
    pos = (lax.broadcasted_iota(jnp.int32, (1, tl), 1) + i * tl).astype(F32)
    t_norm = pos / float(max(t_len - 1, 1))
    o_ref[...] = ht * jnp.exp(-t_norm * dl_ref[...])


def _hy_filters(z, w1p, b1, w2, b2, w3, b3, freq, w4t, deltas, t_len):
    tl = min(512, t_len)
    nch = w4t.shape[0]
    full = lambda a: pl.BlockSpec(a.shape, lambda i: (0,) * a.ndim)
    return pl.pallas_call(
        functools.partial(_hy_filter_kernel, t_len=t_len), grid=(t_len // tl,),
        in_specs=[pl.BlockSpec((tl, z.shape[1]), lambda i: (i, 0))]
        + [full(a) for a in (w1p, b1, w2, b2, w3, b3, freq, w4t, deltas)],
        out_specs=pl.BlockSpec((nch, tl), lambda i: (0, i)),
        out_shape=jax.ShapeDtypeStruct((nch, t_len), F32),
        compiler_params=_params("parallel"), name="hy_filter",
    )(z, w1p, b1, w2, b2, w3, b3, freq, w4t, deltas)


def _fft_fwd(x3, x3i, fst, twr, twi, wbig):
    cb, r = x3.shape[0], x3.shape[1]
    n1 = twr.shape[0]
    xcat = jnp.concatenate([x3[c] for c in range(cb)], axis=1)
    if x3i is None:
        y = _fft_mm(fst[:, :r], xcat)
    else:
        xicat = jnp.concatenate([x3i[c] for c in range(cb)], axis=1)
        y = _fft_mm(fst, jnp.concatenate([xcat, xicat], axis=0))
    yield
    yr, yi = y[:n1], y[n1:]
    twr_t = jnp.concatenate([twr] * cb, axis=1)
    twi_t = jnp.concatenate([twi] * cb, axis=1)
    tr = yr * twr_t - yi * twi_t
    ti = yr * twi_t + yi * twr_t
    cat = jnp.concatenate(
        [jnp.concatenate([tr[:, c * LANES:(c + 1) * LANES], ti[:, c * LANES:(c + 1) * LANES]], axis=1)
         for c in range(cb)], axis=0)
    z = _fft_mm(cat, wbig)
    yield
    return z[:, :LANES].reshape(cb, n1, LANES), z[:, LANES:].reshape(cb, n1, LANES)


def _fft_inv(zr, zi, gst, twr, twi, wbigc):
    cb, n1 = zr.shape[0], zr.shape[1]
    r = gst.shape[0] // 2
    cat = jnp.concatenate([zr, zi], axis=-1).reshape(cb * n1, 2 * LANES)
    a = _fft_mm(cat, wbigc)
    yield
    ar, ai = a[:, :LANES].reshape(cb, n1, LANES), a[:, LANES:].reshape(cb, n1, LANES)
    tr = ar * twr + ai * twi
    ti = ai * twr - ar * twi
    tt = jnp.concatenate([jnp.concatenate([tr[c] for c in range(cb)], axis=1),
                          jnp.concatenate([ti[c] for c in range(cb)], axis=1)], axis=0)
    out = _fft_mm(gst, tt)
    yield
    return (jnp.stack([out[:r, c * LANES:(c + 1) * LANES] for c in range(cb)]),
            jnp.stack([out[r:, c * LANES:(c + 1) * LANES] for c in range(cb)]))


def _hy_spec_kernel(hf_ref, hb_ref, fst_ref, twr_ref, twi_ref, wbig_ref, hr_ref, hi_ref):
    args = (fst_ref[...], twr_ref[...], twi_ref[...], wbig_ref[...])
    out = {}

    def chain(name, ref):
        out[name] = yield from _fft_fwd(ref[...], None, *args)

    _interleave([chain("f", hf_ref), chain("b", hb_ref)])
    hr_ref[...] = out["f"][0] + out["b"][0]
    hi_ref[...] = out["f"][1] - out["b"][1]


def _hy_spectra(filt3, consts, cb=2 * HY_CB):
    fst, gst, twr, twi, wbig, wbigc = consts
    r = filt3.shape[1]
    n1 = twr.shape[0]
    nj = HY_DIM // cb
    full = lambda a: pl.BlockSpec(a.shape, lambda o, j: (0,) * a.ndim)
    out_spec = pl.BlockSpec((cb, n1, LANES), lambda o, j: (o * nj + j, 0, 0))
    return pl.pallas_call(
        _hy_spec_kernel, grid=(2, nj),
        in_specs=[pl.BlockSpec((cb, r, LANES), lambda o, j: (2 * o * nj + j, 0, 0)),
                  pl.BlockSpec((cb, r, LANES), lambda o, j: ((2 * o + 1) * nj + j, 0, 0)),
                  full(fst), full(twr), full(twi), full(wbig)],
        out_specs=[out_spec, out_spec],
        out_shape=[jax.ShapeDtypeStruct((2 * HY_DIM, n1, LANES), F32)] * 2,
        compiler_params=_params("parallel", "parallel"), name="hy_spec",
    )(filt3, filt3, fst, twr, twi, wbig)


def _tshift(x3, back, lane, rowi):
    r = x3.shape[1]
    if back:
        rl = pltpu.roll(x3, 1, 2)
        rr = pltpu.roll(rl, 1, 1)
        edge = lane == 0
        out = jnp.where(edge, rr, rl)
        return jnp.where(edge & (rowi == 0), 0.0, out)
    rl = pltpu.roll(x3, LANES - 1, 2)
    rr = pltpu.roll(rl, r - 1, 1)
    edge = lane == LANES - 1
    out = jnp.where(edge, rr, rl)
    return jnp.where(edge & (rowi == r - 1), 0.0, out)


def _hy_conv_kernel(x1_ref, x2_ref, v_ref, p1_ref, p2_ref, pv_ref, sk_ref, h1r_ref, h1i_ref,
                    h2r_ref, h2i_ref, fst_ref, gst_ref, twr_ref, twi_ref, wbig_ref, wbigc_ref,
                    o_ref):
    cb, nb, r, _ = x1_ref.shape
    half = cb // HY_CHAINS
    shape = (half, r, LANES)
    lane = lax.broadcasted_iota(jnp.int32, shape, 2)
    rowi = lax.broadcasted_iota(jnp.int32, shape, 1)
    fwd = (fst_ref[...], twr_ref[...], twi_ref[...], wbig_ref[...])
    inv = (gst_ref[...], twr_ref[...], twi_ref[...], wbigc_ref[...])

    def chain(lo):
        ch = pl.ds(lo, half)

        def short_conv(x_ref, p_ref):
            p = p_ref[ch]
            out = []
            for b in range(nb):
                x = x_ref[ch, b].astype(F32)
                out.append(p[:, 0:1, :] * _tshift(x, True, lane, rowi) + p[:, 1:2, :] * x
                           + p[:, 2:3, :] * _tshift(x, False, lane, rowi) + p[:, 3:4, :])
            return out

        x1 = short_conv(x1_ref, p1_ref)
        x2 = short_conv(x2_ref, p2_ref)
        z = short_conv(v_ref, pv_ref)
        sk = sk_ref[ch]
        for o, (gate, hr_ref, hi_ref) in enumerate(((x1, h1r_ref, h1i_ref), (x2, h2r_ref, h2i_ref))):
            zr, zi = yield from _fft_fwd(z[0], z[1], *fwd)
            hr, hi = hr_ref[ch], hi_ref[ch]
            y = yield from _fft_inv(zr * hr - zi * hi, zr * hi + zi * hr, *inv)
            z = [gate[b] * (y[b] + sk[:, o:o + 1, :] * z[b]) for b in range(nb)]
        for b in range(nb):
            o_ref[ch, b] = z[b].astype(o_ref.dtype)

    _interleave([chain(i * half) for i in range(HY_CHAINS)])


def _hy_conv(projt4, pp, ps, hr, hi, consts, cb=HY_CB):
    fst, gst, twr, twi, wbig, wbigc = consts
    _, bsz, r, _ = projt4.shape
    n1 = twr.shape[0]
    nj = HY_DIM // cb
    full = lambda a: pl.BlockSpec(a.shape, lambda j, b: (0,) * a.ndim)

    assert bsz % 2 == 0, "sequences are transformed in pairs"

    def xs(off):
        return pl.BlockSpec((cb, 2, r, LANES), lambda j, b: (off * nj + j, b, 0, 0))

    def ps_(off):
        return pl.BlockSpec((cb, SUBLANES, LANES), lambda j, b: (off * nj + j, 0, 0))

    def hs(off):
        return pl.BlockSpec((cb, n1, LANES), lambda j, b: (off * nj + j, 0, 0))

    return pl.pallas_call(
        _hy_conv_kernel, grid=(nj, bsz // 2),
        in_specs=[xs(0), xs(1), xs(2), ps_(0), ps_(1), ps_(2), ps_(0), hs(0), hs(0), hs(1), hs(1),
                  full(fst), full(gst), full(twr), full(twi), full(wbig), full(wbigc)],
        out_specs=pl.BlockSpec((cb, 2, r, LANES), lambda j, b: (j, b, 0, 0)),
        out_shape=jax.ShapeDtypeStruct((HY_DIM, bsz, r, LANES), BF16),
        compiler_params=_params("parallel", "arbitrary"), name="hy_conv",
    )(projt4, projt4, projt4, pp, pp, pp, ps, hr, hi, hr, hi, fst, gst, twr, twi, wbig, wbigc)


def _fft_consts(t_len):
    n = 2 * t_len
    n1 = n // LANES
    r = t_len // LANES
    k1 = np.arange(n1)[:, None]
    th1 = 2.0 * np.pi * k1 * np.arange(r)[None, :] / n1
    c1, s1 = np.cos(th1), np.sin(th1)
    fst = np.block([[c1, s1], [-s1, c1]])
    gst = np.block([[c1.T, -s1.T], [s1.T, c1.T]]) / n
    tht = 2.0 * np.pi * k1 * np.arange(LANES)[None, :] / n
    twr, twi = np.cos(tht), -np.sin(tht)
    th2 = 2.0 * np.pi * np.arange(LANES)[:, None] * np.arange(LANES)[None, :] / LANES
    cr, ci = np.cos(th2), -np.sin(th2)
    wbig = np.block([[cr, ci], [-ci, cr]])
    wbigc = np.block([[cr, -ci], [ci, cr]])
    return tuple(jnp.asarray(a, F32) for a in (fst, gst, twr, twi, wbig, wbigc))


def _hy_positions(t_len):
    pos = np.arange(t_len, dtype=np.float32)[:, None]
    t = pos / np.float32(max(t_len - 1, 1))
    bands = np.linspace(1e-4, HY_BANDS - 1, HY_BANDS, dtype=np.float32)[None, :]
    ang = bands * np.float32(2.0 * math.pi / t_len) * pos
    z = np.concatenate([t, np.cos(ang), -np.sin(ang)], -1).astype(np.float32)
    return jnp.asarray(np.pad(z, ((0, 0), (0, HY_FILT_HID - HY_EMB))))


def _hy_deltas():
    max_decay = math.log(HY_DECAY_TARGET) / HY_FAST_DECAY_PCT
    min_decay = math.log(HY_DECAY_TARGET) / HY_SLOW_DECAY_PCT
    d = np.abs(np.linspace(min_decay, max_decay, HY_DIM, dtype=np.float32))
    return jnp.asarray(np.tile(d, 4)[:, None])


def _lane_rep(a, rows=SUBLANES):
    ch, k = a.shape
    a = jnp.pad(a, ((0, 0), (0, rows - k)))
    return jnp.broadcast_to(a[:, :, None], (ch, rows, LANES))


def _out_even_kernel(x_ref, yat_ref, of_ref, ob_ref, zg_ref, nw_ref, wa_ref, wb_ref, g_ref, b_ref,
                     w1_ref, w2_ref, g2_ref, b2_ref, o_ref):
    mix = _mm(yat_ref[...], wa_ref[...], _TN)
    nw = nw_ref[...]
    for h in range(GDN_HEADS):
        sl = slice(h * GDN_DK, (h + 1) * GDN_DK)
        o = of_ref[:, sl] + ob_ref[:, sl]
        y = o * lax.rsqrt(jnp.mean(o * o, -1, keepdims=True) + RMS_EPS) * nw * _silu(zg_ref[:, sl])
        mix = mix + _mm(y, wb_ref[sl, :])
    x1 = _layer_norm(DEEPNORM_ALPHA * x_ref[...] + mix, g_ref[...], b_ref[...])
    o_ref[...] = _mlp_tail(x1, w1_ref, w2_ref, g2_ref, b2_ref)


def _out_even(x2, yat, of, ob, projm, norm_w, wa, wb, g, b, mlp, tm=512):
    n, d = x2.shape
    tm = min(tm, n)
    row = lambda w: pl.BlockSpec((tm, w), lambda i: (i, 0))
    consts = (norm_w, wa, wb, g, b) + tuple(mlp)
    return pl.pallas_call(
        _out_even_kernel, grid=(n // tm,),
        in_specs=[row(d), pl.BlockSpec((HY_DIM, tm), lambda i: (0, i)), row(512), row(512),
                  pl.BlockSpec((tm, 512), lambda i: (i, 3))] + [_resident(a) for a in consts],
        out_specs=row(d), out_shape=jax.ShapeDtypeStruct((n, d), F32),
        compiler_params=_params("parallel"), name="out_even",
    )(x2, yat, of, ob, projm, *consts)


def _even_layer(x2, bsz, t_len, w_in, hy_conv_w, hy_conv_b, f_w1, f_b1, f_w2, f_b2, f_w3, f_b3,
                f_w4, f_freq, hy_skip, gdn_conv_w, gdn_a_log, gdn_dt_bias, gdn_norm_w, w_out,
                ln_g, ln_b, mlp):
    d = x2.shape[1]
    hyw = 3 * HY_DIM
    wm = jnp.concatenate([w_in[:, hyw:], jnp.zeros((d, LANES - 12), F32)], axis=1).astype(BF16)
    wt = w_in[:, :hyw].T.astype(BF16)
    projm, projt = _project(x2, wm, wt)
    consts = _fft_consts(t_len)
    w1p = jnp.pad(f_w1, ((0, HY_FILT_HID - HY_EMB), (0, 0)))
    filt = _hy_filters(_hy_positions(t_len), w1p, f_b1[None], f_w2, f_b2[None], f_w3, f_b3[None],
                       f_freq[None], f_w4.T, _hy_deltas(), t_len)
    r = t_len // LANES
    hr, hi = _hy_spectra(filt.reshape(4 * HY_DIM, r, LANES), consts)
    pp = _lane_rep(jnp.concatenate([hy_conv_w.T, hy_conv_b[:, None]], axis=1))
    ps = _lane_rep(hy_skip.T)
    yat = _hy_conv(projt.reshape(hyw, bsz, r, LANES), pp, ps, hr, hi, consts)
    yat = yat.reshape(HY_DIM, bsz * t_len)
    alog = jnp.zeros((1, LANES), F32).at[0, :8].set(gdn_a_log.reshape(-1))
    dtb = jnp.zeros((1, LANES), F32).at[0, :8].set(gdn_dt_bias.reshape(-1))
    qn, kn, vn, gates = _gdn_pre(projm, gdn_conv_w, alog, dtb, bsz, t_len)
    of, ob = _gdn_scan(qn, kn, vn, gates, bsz, t_len)
    return _out_even(x2, yat, of, ob, projm, gdn_norm_w[None], w_out[:HY_DIM].astype(BF16),
                     w_out[HY_DIM:].astype(BF16), ln_g[None], ln_b[None], mlp)


def _hg_scan_kernel(qf_ref, ff_ref, vf_ref, qb_ref, fb_ref, vb_ref, low_ref, of_ref, ob_ref,
                    state, q_s, k_s, g_s, *, layer):
    n = pl.program_id(1)

    @pl.when(n == 0)
    def _():
        state[...] = jnp.zeros_like(state)

    tb = qf_ref.shape[0]
    c = HG_CHUNK
    nsub = tb // c
    low = low_ref[...]
    e = jnp.exp(low - jnp.max(low, axis=0, keepdims=True))
    p = e / jnp.sum(e, axis=0, keepdims=True)
    lb = jnp.sum(p[1:layer + 1], axis=0, keepdims=True)
    rb = lax.broadcasted_iota(jnp.int32, (tb, tb), 0)
    cb = lax.broadcasted_iota(jnp.int32, (tb, tb), 1)
    same = (rb // c) == (cb // c)
    hb = c // 2
    rowh = lax.broadcasted_iota(jnp.int32, (hb, 1), 0)
    dirs = ((qf_ref, ff_ref, vf_ref, of_ref), (qb_ref, fb_ref, vb_ref, ob_ref))
    for d, (q_ref, f_ref, v_ref, o_ref) in enumerate(dirs):
        fg = lb + (1.0 - lb) * _sigmoid(f_ref[...])
        tri = (same & ((cb <= rb) if d == 0 else (cb >= rb))).astype(BF16)
        g_s[d] = _sel_mm(tri, jnp.log(fg))
        k_s[d] = 1.0 - fg
        q_s[d] = _silu(q_ref[...])

    def body(s, carry):
        def sub_rows(d, u):
            idx = s * HG_SUBS + u
            sub = idx if d == 0 else nsub - 1 - idx
            return pl.ds(pl.multiple_of(sub * c, c), c)

        def pairwise(d, qb, gkb, gb, vb):
            acc = None
            for j in range(hb):
                dec = jnp.exp(gb - gkb[j:j + 1, :])
                a_col = jnp.sum(qb * dec, axis=1, keepdims=True)
                a_col = jnp.where((rowh >= j) if d == 0 else (rowh <= j), a_col, 0.0)
                term = a_col * vb[j:j + 1, :]
                acc = term if acc is None else acc + term
            return acc

        def chain(d, h):
            v_ref, o_ref = dirs[d][2], dirs[d][3]
            sl = slice(h * HG_DK, (h + 1) * HG_DK)
            st = state[d, h]
            pending = []
            for u in range(HG_SUBS):
                rows = sub_rows(d, u)
                qc, kc, gc, vc = q_s[d, rows, sl], k_s[d, rows, sl], g_s[d, rows, sl], v_ref[rows, sl]
                g_last = gc[c - 1:c, :] if d == 0 else gc[0:1, :]
                inter = _mm(qc * jnp.exp(gc), st, _NT)
                upd = _mm(vc, kc * jnp.exp(g_last - gc), _TN)
                yield
                st = st * jnp.exp(g_last) + upd
                pending.append((rows, qc, kc, gc, vc, inter))
            state[d, h] = st
            lo, hi = slice(0, hb), slice(hb, c)
            src, dst, bnd = (lo, hi, hb - 1) if d == 0 else (hi, lo, hb)
            for rows, qc, kc, gc, vc, inter in pending:
                g_bnd = gc[bnd:bnd + 1, :]
                q_x = qc[dst] * jnp.exp(gc[dst] - g_bnd)
                k_y = kc[src] * jnp.exp(g_bnd - gc[src])
                v_src = vc[src]
                cross = None
                for j in range(hb):
                    term = jnp.sum(q_x * k_y[j:j + 1, :], axis=1, keepdims=True) * v_src[j:j + 1, :]
                    cross = term if cross is None else cross + term
                gk = gc - jnp.log(kc)
                o_lo = pairwise(d, qc[lo], gk[lo], gc[lo], vc[lo])
                o_hi = pairwise(d, qc[hi], gk[hi], gc[hi], vc[hi])
                if d == 0:
                    o_hi = o_hi + cross
                else:
                    o_lo = o_lo + cross
                o_ref[rows, sl] = inter + jnp.concatenate([o_lo, o_hi], axis=0)

        _interleave([chain(d, h) for d in range(2) for h in range(HG_HEADS)])
        return carry

    lax.fori_loop(0, nsub // HG_SUBS, body, 0)


def _hg_scan(proj, hg_lower, layer, bsz, t_len, tb=256):
    n = proj.shape[0]
    nb = t_len // tb

    def spec(colb, rev):
        if rev:
            return pl.BlockSpec((tb, 512), lambda b, i: (b * nb + nb - 1 - i, colb))
        return pl.BlockSpec((tb, 512), lambda b, i: (b * nb + i, colb))

    return pl.pallas_call(
        functools.partial(_hg_scan_kernel, layer=layer), grid=(bsz, nb),
        in_specs=[spec(0, False), spec(1, False), spec(3, False),
                  spec(0, True), spec(2, True), spec(3, True),
                  pl.BlockSpec(hg_lower.shape, lambda b, i: (0, 0))],
        out_specs=[spec(0, False), spec(0, True)],
        out_shape=[jax.ShapeDtypeStruct((n, 512), F32)] * 2,
        scratch_shapes=[pltpu.VMEM((2, HG_HEADS, HG_DK, HG_DK), F32)]
        + [pltpu.VMEM((2, tb, 512), F32)] * 3,
        compiler_params=_params("parallel", "arbitrary"), name="hg_scan",
    )(proj, proj, proj, proj, proj, proj, hg_lower)


def _rw_pre_kernel(rc, rp, rn, kc, kp, kn, vc, vp, vn, lc, lp, ln, mu_ref, w0_ref, w2_ref,
                   a0_ref, a2_ref, g2_ref, kk_ref, ka_ref, bd_ref,
                   r_o, k_o, v_o, kk_o, kka_o, lwf_o, lwb_o, g_o, scr):
    i = pl.program_id(1)
    first = i == 0
    last = i == pl.num_programs(1) - 1
    tb = rc.shape[0]

    def shifted(cur_ref, prev_ref, next_ref, col0):
        w = cur_ref.shape[1]
        cur = cur_ref[...]
        scr[pl.ds(SUBLANES, tb), pl.ds(0, w)] = cur
        scr[pl.ds(0, SUBLANES), pl.ds(0, w)] = jnp.where(first, 0.0, prev_ref[...])
        scr[pl.ds(SUBLANES + tb, SUBLANES), pl.ds(0, w)] = jnp.where(last, 0.0, next_ref[...])
        prev = scr[pl.ds(SUBLANES - 1, tb), pl.ds(0, w)]
        nxt = scr[pl.ds(SUBLANES + 1, tb), pl.ds(0, w)]
        mu = mu_ref[:, col0:col0 + w]
        return cur + mu[0:1] * (prev - cur) + mu[1:2] * (nxt - cur)

    r = shifted(rc, rp, rn, 0)
    k = shifted(kc, kp, kn, RW_DIM)
    v = shifted(vc, vp, vn, 2 * RW_DIM)
    lo = shifted(lc, lp, ln, 3 * RW_DIM)
    lo_wa, lo_g = lo[:, :LANES], lo[:, LANES:]
    th = jnp.tanh(lo_wa)
    for d, o_ref in enumerate((lwf_o, lwb_o)):
        w_raw = w0_ref[d:d + 1, :] + _mm(th, w2_ref[d])
        o_ref[...] = -math.exp(-0.5) * _sigmoid(w_raw)
    a = _sigmoid(a0_ref[...] + _mm(lo_wa, a2_ref[...]))
    g_o[...] = _mm(_sigmoid(lo_g), g2_ref[...]).astype(g_o.dtype)
    kx = k * kk_ref[...]
    kk = kx * lax.rsqrt(_mm(kx * kx, bd_ref[...]) + L2_EPS)
    r_o[...] = r.astype(r_o.dtype)
    v_o[...] = v.astype(v_o.dtype)
    k_o[...] = (k * (1.0 + (a - 1.0) * ka_ref[...])).astype(k_o.dtype)
    kk_o[...] = kk.astype(kk_o.dtype)
    kka_o[...] = (kk * a).astype(kka_o.dtype)


def _rw_pre(proj, mu, w0, w2p, a0, a2p, g2, k_k, k_a, bd, bsz, t_len, tb=512):
    n = proj.shape[0]
    nb = t_len // tb
    base = 2560
    specs = []
    for off, w in ((0, 512), (512, 512), (1024, 512), (1536, 256)):
        specs += _halo_specs(w, (base + off) // w, tb, t_len, bsz)
    full = lambda a: pl.BlockSpec(a.shape, lambda b, i: (0,) * a.ndim)
    out_spec = pl.BlockSpec((tb, RW_DIM), lambda b, i: (b * nb + i, 0))
    consts = (mu, w0, w2p, a0, a2p, g2, k_k, k_a, bd)
    return pl.pallas_call(
        _rw_pre_kernel, grid=(bsz, nb),
        in_specs=specs + [full(a) for a in consts],
        out_specs=[out_spec] * 8,
        out_shape=[jax.ShapeDtypeStruct((n, RW_DIM), dt) for dt in (BF16,) * 5 + (F32, F32, BF16)],
        scratch_shapes=[pltpu.VMEM((tb + 2 * SUBLANES, RW_DIM), F32)],
        compiler_params=_params("parallel", "parallel"), name="rw_pre",
    )(*([proj] * 12), *consts)


def _rw_scan_kernel(rf, kf, vf, bf, af, wf, rb, kb, vb, bb, ab, wb, yf, yb, state):
    n = pl.program_id(1)

    @pl.when(n == 0)
    def _():
        state[...] = jnp.zeros_like(state)

    c = RW_CHUNK
    gw = RW_GROUP * RW_HD
    gc = RW_GROUP * c
    row = lax.broadcasted_iota(jnp.int32, (c, c), 0)
    col = lax.broadcasted_iota(jnp.int32, (c, c), 1)
    rr = lax.broadcasted_iota(jnp.int32, (gc, gc), 0)
    cc = lax.broadcasted_iota(jnp.int32, (gc, gc), 1)
    same = (rr // c) == (cc // c)
    head_lanes = (lax.broadcasted_iota(jnp.int32, (gc, gw), 0) // c
                  == lax.broadcasted_iota(jnp.int32, (gc, gw), 1) // RW_HD)
    rs = lax.broadcasted_iota(jnp.int32, (gw, gw), 0)
    cs = lax.broadcasted_iota(jnp.int32, (gw, gw), 1)
    head_blocks = ((rs // RW_HD) == (cs // RW_HD)).astype(F32)
    row_cat = lax.broadcasted_iota(jnp.int32, (c, gc), 0)
    col_cat = lax.broadcasted_iota(jnp.int32, (c, gc), 1) % c
    eye_cat = (col_cat == row_cat).astype(F32)

    def stack(x):
        return jnp.where(head_lanes, jnp.concatenate([x] * RW_GROUP, axis=0), 0.0)

    dirs = ((rf, kf, vf, bf, af, wf, yf), (rb, kb, vb, bb, ab, wb, yb))
    def chain(ib, d, p, rows):
        r_ref, k_ref, v_ref, b_ref, a_ref, w_ref, y_ref = dirs[d]
        sl = slice(p * gw, (p + 1) * gw)
        incl = (col_cat <= row_cat) if d == 0 else (col_cat >= row_cat)
        strict = (col_cat < row_cat) if d == 0 else (col_cat > row_cat)
        lw = w_ref[ib, rows, sl]
        tri = ((col <= row) if d == 0 else (col >= row)).astype(BF16)
        g_in = _sel_mm(tri, lw)
        yield
        g_ex = g_in - lw
        g_end = g_in[c - 1:c, :] if d == 0 else g_in[0:1, :]
        e_neg = jnp.exp(-g_in)
        e_end = jnp.exp(g_end - g_in)
        k, a, v = k_ref[ib, rows, sl], a_ref[ib, rows, sl], v_ref[ib, rows, sl]
        lhs = jnp.concatenate([r_ref[ib, rows, sl] * jnp.exp(g_in), b_ref[ib, rows, sl] * jnp.exp(g_ex)],
                              axis=0)
        ak = _mm(lhs, stack(k * e_neg), _NT)
        yield
        aa = _mm(lhs, stack(a * e_neg), _NT)
        yield
        a_rk = jnp.where(incl, ak[:c], 0.0)
        a_ra = jnp.where(incl, aa[:c], 0.0)
        a_bk = jnp.where(strict, ak[c:], 0.0)
        a_ba = jnp.where(strict, aa[c:], 0.0)
        t_inv = yield from _unit_tri_inv_cat(a_ba, eye_cat, same, c)
        yield
        m = state[ib, d, p]
        xm = _mm(lhs, m, _NT)
        yield
        av = _mm(jnp.concatenate([a_rk, a_bk], axis=0), stack(v))
        yield
        u = _mm(t_inv, stack(xm[c:] + av[c:]))
        yield
        y_ref[ib, rows, sl] = xm[:c] + av[:c] - _mm(a_ra, stack(u))
        yield
        upd = _mm(jnp.concatenate([v.astype(F32), u], axis=0),
                  jnp.concatenate([k * e_end, -(a * e_end)], axis=0), _TN)
        state[ib, d, p] = m * jnp.exp(g_end) + upd * head_blocks

    def step(j, carry):
        def rows(d):
            return pl.ds(pl.multiple_of((j if d == 0 else SCAN_CHUNKS - 1 - j) * c, c), c)

        _interleave([chain(ib, d, p, rows(d)) for ib in range(rf.shape[0]) for d in range(2)
                     for p in range(RW_DIM // gw)])
        return carry

    lax.fori_loop(0, SCAN_CHUNKS, step, 0)


def _rw_scan(r, k, v, kk, kka, lwf, lwb, bsz, t_len, bb=SCAN_BATCH):
    n = r.shape[0]
    c = RW_CHUNK * SCAN_CHUNKS
    gw = RW_GROUP * RW_HD
    nc = t_len // c
    bb = math.gcd(bb, bsz)
    r, k, v, kk, kka, lwf, lwb =(a.reshape(bsz, t_len, RW_DIM) for a in (r, k, v, kk, kka, lwf, lwb))
    fwd = pl.BlockSpec((bb, c, RW_DIM), lambda g, i: (g, i, 0))
    bwd = pl.BlockSpec((bb, c, RW_DIM), lambda g, i: (g, nc - 1 - i, 0))
    yf, yb = pl.pallas_call(
        _rw_scan_kernel, grid=(bsz // bb, nc),
        in_specs=[fwd] * 6 + [bwd] * 6, out_specs=[fwd, bwd],
        out_shape=[jax.ShapeDtypeStruct((bsz, t_len, RW_DIM), F32)] * 2,
        scratch_shapes=[pltpu.VMEM((bb, 2, RW_DIM // gw, gw, gw), F32)],
        compiler_params=_params("parallel", "arbitrary"), name="rw_scan",
    )(r, k, v, kk, kka, lwf, r, k, v, kk, kka, lwb)
    return yf.reshape(n, RW_DIM), yb.reshape(n, RW_DIM)


def _out_odd_kernel(x_ref, hf_ref, hb_ref, hg_ref, hnw_ref, yf_ref, yb_ref, r_ref, k_ref, v_ref,
                    gate_ref, rk_ref, lw_ref, lb_ref, bd_ref, wc_ref, wd_ref, g_ref, b_ref,
                    w1_ref, w2_ref, g2_ref, b2_ref, o_ref):
    nw = hnw_ref[...]
    mix = None
    for h in range(HG_HEADS):
        sl = slice(h * HG_DK, (h + 1) * HG_DK)
        o = hf_ref[:, sl] + hb_ref[:, sl]
        y = o * lax.rsqrt(jnp.mean(o * o, -1, keepdims=True) + RMS_EPS) * nw * _silu(hg_ref[:, sl])
        part = _mm(y, wc_ref[sl, :])
        mix = part if mix is None else mix + part
    bd = bd_ref[...]
    y = yf_ref[...] + yb_ref[...]
    yc = y - _sel_mm(bd, y, sel_first=False) * (1.0 / RW_HD)
    var = _mm(yc * yc, bd) * (1.0 / RW_HD)
    yn = yc * lax.rsqrt(var + RW_GN_EPS) * lw_ref[...] + lb_ref[...]
    rk = r_ref[...].astype(F32) * k_ref[...] * rk_ref[...]
    bonus = _mm(rk, bd) * v_ref[...]
    mix = mix + _mm((yn + bonus) * gate_ref[...], wd_ref[...])
    x1 = _layer_norm(DEEPNORM_ALPHA * x_ref[...] + mix, g_ref[...], b_ref[...])
    o_ref[...] = _mlp_tail(x1, w1_ref, w2_ref, g2_ref, b2_ref)


def _out_odd(x2, hf, hb, proj, hnw, yf, yb, r, k, v, gate, r_k, lnx_w, lnx_b, bd, wc, wd, g, b,
             mlp, tm=512):
    n, d = x2.shape
    tm = min(tm, n)
    row = lambda w: pl.BlockSpec((tm, w), lambda i: (i, 0))
    consts = (r_k, lnx_w, lnx_b, bd, wc, wd, g, b) + tuple(mlp)
    return pl.pallas_call(
        _out_odd_kernel, grid=(n // tm,),
        in_specs=[row(d), row(512), row(512), pl.BlockSpec((tm, 512), lambda i: (i, 4)),
                  _resident(hnw)] + [row(512)] * 6 + [_resident(a) for a in consts],
        out_specs=row(d), out_shape=jax.ShapeDtypeStruct((n, d), F32),
        compiler_params=_params("parallel"), name="out_odd",
    )(x2, hf, hb, proj, hnw, yf, yb, r, k, v, gate, *consts)


def _odd_layer(x2, bsz, t_len, layer, w_in, hg_lower, hg_norm_w, rw_mu, rw_w0, rw_w2, rw_a0, rw_a2,
               rw_g2, rw_k_k, rw_k_a, rw_r_k, rw_lnx_w, rw_lnx_b, w_out, ln_g, ln_b, mlp):
    proj = _project(x2, w_in.astype(BF16))
    hf, hb = _hg_scan(proj, hg_lower, layer, bsz, t_len)
    idx = np.arange(RW_DIM) // RW_HD
    bd = jnp.asarray(idx[:, None] == idx[None, :], BF16)
    lora = rw_w2.shape[1]
    w2p = jnp.pad(rw_w2, ((0, 0), (0, LANES - lora), (0, 0)))
    a2p = jnp.pad(rw_a2, ((LANES - rw_a2.shape[0], 0), (0, 0)))
    r, k, v, kk, kka, lwf, lwb, gate = _rw_pre(proj, rw_mu, rw_w0, w2p, rw_a0[None], a2p, rw_g2,
                                               rw_k_k[None], rw_k_a[None], bd, bsz, t_len)
    yf, yb = _rw_scan(r, k, v, kk, kka, lwf, lwb, bsz, t_len)
    return _out_odd(x2, hf, hb, proj, hg_norm_w[None], yf, yb, r, k, v, gate,
                    rw_r_k.reshape(1, RW_DIM), rw_lnx_w[None], rw_lnx_b[None], bd,
                    w_out[:512].astype(BF16), w_out[512:].astype(BF16), ln_g[None], ln_b[None], mlp)


def kernel(x, ev_w_in, hy_conv_w, hy_conv_b, hy_filt_w1, hy_filt_b1, hy_filt_w2, hy_filt_b2, hy_filt_w3, hy_filt_b3, hy_filt_w4, hy_filt_freq, hy_skip, gdn_conv_w, gdn_a_log, gdn_dt_bias, gdn_norm_w, ev_w_out, od_w_in, hg_lower, hg_norm_w, rw_mu, rw_w0, rw_w2, rw_a0, rw_a2, rw_g2, rw_k_k, rw_k_a, rw_r_k, rw_lnx_w, rw_lnx_b, od_w_out, ln_g, ln_b, mlp_w1, mlp_w2):
    bsz, t_len, d = x.shape
    x2 = x.reshape(bsz * t_len, d)
    for layer in range(DEPTH):
        i = layer // 2
        mlp = (mlp_w1[layer].astype(BF16), mlp_w2[layer].astype(BF16), ln_g[layer, 1][None],
               ln_b[layer, 1][None])
        if layer % 2 == 0:
            x2 = _even_layer(x2, bsz, t_len, ev_w_in[i], hy_conv_w[i], hy_conv_b[i], hy_filt_w1[i],
                             hy_filt_b1[i], hy_filt_w2[i], hy_filt_b2[i], hy_filt_w3[i],
                             hy_filt_b3[i], hy_filt_w4[i], hy_filt_freq[i], hy_skip[i],
                             gdn_conv_w[i], gdn_a_log[i], gdn_dt_bias[i], gdn_norm_w[i],
                             ev_w_out[i], ln_g[layer, 0], ln_b[layer, 0], mlp)
        else:
            x2 = _odd_layer(x2, bsz, t_len, layer, od_w_in[i], hg_lower, hg_norm_w[i], rw_mu[i],
                            rw_w0[i], rw_w2[i], rw_a0[i], rw_a2[i], rw_g2[i], rw_k_k[i], rw_k_a[i],
                            rw_r_k[i], rw_lnx_w[i], rw_lnx_b[i], od_w_out[i], ln_g[layer, 0],
                            ln_b[layer, 0], mlp)
    return x2.reshape(bsz, t_len, d)
```

```python
import functools
import math

import numpy as np
import jax
import jax.numpy as jnp
from jax import lax
from jax.experimental import pallas as pl
from jax.experimental.pallas import tpu as pltpu

F32 = jnp.float32
BF16 = jnp.bfloat16

D_MODEL = 1024
DEPTH = 2
DEEPNORM_ALPHA = (2.0 * DEPTH) ** 0.25
LN_EPS = 1e-5
RMS_EPS = 1e-6
L2_EPS = 1e-6

HY_DIM = 512
HY_EMB = 33
HY_BANDS = 16
HY_FILT_HID = 64
HY_DECAY_TARGET = 1e-2
HY_FAST_DECAY_PCT = 0.3
HY_SLOW_DECAY_PCT = 1.5

GDN_HEADS = 4
GDN_DK = 128
GDN_CHUNK = 64

HG_HEADS = 4
HG_DK = 128
HG_CHUNK = 16
HG_SUBS = 8

RW_HD = 64
RW_DIM = 512
RW_CHUNK = 64
RW_GROUP = 4
RW_GN_EPS = 64e-5

SCAN_BATCH = 4
SCAN_CHUNKS = 2

LANES = 128
SUBLANES = 8
VMEM_LIMIT = 56 * 1024 * 1024

_NN = (((1,), (0,)), ((), ()))
_NT = (((1,), (1,)), ((), ()))
_TN = (((0,), (0,)), ((), ()))


def _mm(a, b, dims=_NN):
    return lax.dot_general(a.astype(BF16), b.astype(BF16), dims, preferred_element_type=F32)


def _params(*sem):
    return pltpu.CompilerParams(dimension_semantics=sem, vmem_limit_bytes=VMEM_LIMIT)


def _sigmoid(x):
    return 1.0 / (1.0 + jnp.exp(-x))


def _silu(x):
    return x * _sigmoid(x)


def _softplus(x):
    return jnp.maximum(x, 0.0) + jnp.log(1.0 + jnp.exp(-jnp.abs(x)))


def _layer_norm(v, g, b):
    mean = jnp.mean(v, axis=-1, keepdims=True)
    vc = v - mean
    var = jnp.mean(vc * vc, axis=-1, keepdims=True)
    return vc * lax.rsqrt(var + LN_EPS) * g + b


def _col(x, idx, lane):
    return jnp.sum(jnp.where(lane == idx, x, 0.0), axis=1, keepdims=True)


def _split2(a):
    hi = a.astype(BF16)
    return hi, (a - hi.astype(F32)).astype(BF16)


def _mm3(a, b, dims=_NN):
    ah, al = _split2(a)
    bh, bl = _split2(b)
    dot = functools.partial(lax.dot_general, dimension_numbers=dims, preferred_element_type=F32)
    return dot(ah, bh) + dot(ah, bl) + dot(al, bh)


def _sel_mm(sel, x, dims=_NN, sel_first=True):
    hi = x.astype(BF16)
    r1 = x - hi.astype(F32)
    mid = r1.astype(BF16)
    lo = (r1 - mid.astype(F32)).astype(BF16)
    dot = functools.partial(lax.dot_general, dimension_numbers=dims, preferred_element_type=F32)
    if sel_first:
        return dot(sel, hi) + dot(sel, mid) + dot(sel, lo)
    return dot(hi, sel) + dot(mid, sel) + dot(lo, sel)


def _expand_bd(x_cat, same):
    g = x_cat.shape[1] // x_cat.shape[0]
    return jnp.where(same, jnp.concatenate([x_cat] * g, axis=0), 0.0)


def _unit_tri_inv_cat(a_cat, eye_cat, same, depth):
    c = a_cat.shape[0]
    p = -a_cat
    t = eye_cat + p
    p = _mm(p, _expand_bd(p, same))
    yield
    k = 4
    while k < depth:
        both = _mm(jnp.concatenate([p, t], axis=0), _expand_bd(p, same))
        yield
        p, t = both[:c], t + both[c:]
        k *= 2
    return t + _mm(t, _expand_bd(p, same))


def _interleave(chains):
    active = list(chains)
    while active:
        still = []
        for g in active:
            try:
                next(g)
                still.append(g)
            except StopIteration:
                pass
        active = still


def _proj_kernel(x_ref, w_ref, o_ref):
    o_ref[...] = _mm(x_ref[...], w_ref[...])


def _proj_t_kernel(x_ref, w_ref, wt_ref, o_ref, ot_ref):
    xb = x_ref[...].astype(BF16)
    o_ref[...] = _mm(xb, w_ref[...])
    ot_ref[...] = _mm(wt_ref[...], xb, _NT).astype(ot_ref.dtype)


def _project(x2, w, wt=None, tm=512):
    n, d = x2.shape
    m = w.shape[1]
    grid = (n // tm,)
    once = pl.Buffered(1)
    x_spec = pl.BlockSpec((tm, d), lambda i: (i, 0))
    w_spec = pl.BlockSpec((d, m), lambda i: (0, 0), pipeline_mode=once)
    o_spec = pl.BlockSpec((tm, m), lambda i: (i, 0))
    if wt is None:
        return pl.pallas_call(
            _proj_kernel, grid=grid, in_specs=[x_spec, w_spec], out_specs=o_spec,
            out_shape=jax.ShapeDtypeStruct((n, m), F32), compiler_params=_params("parallel"),
            name="proj")(x2, w)
    mt = wt.shape[0]
    return pl.pallas_call(
        _proj_t_kernel, grid=grid,
        in_specs=[x_spec, w_spec, pl.BlockSpec((mt, d), lambda i: (0, 0), pipeline_mode=once)],
        out_specs=[o_spec, pl.BlockSpec((mt, tm), lambda i: (0, i))],
        out_shape=[jax.ShapeDtypeStruct((n, m), F32), jax.ShapeDtypeStruct((mt, n), BF16)],
        compiler_params=_params("parallel"), name="proj_t")(x2, w, wt)


def _mlp_tail(x, w1_ref, w2_ref, g_ref, b_ref):
    h = jnp.maximum(_mm(x, w1_ref[...]), 0.0)
    return _layer_norm(DEEPNORM_ALPHA * x + _mm(h * h, w2_ref[...]), g_ref[...], b_ref[...])


def _resident(a):
    return pl.BlockSpec(a.shape, lambda i: (0,) * a.ndim, pipeline_mode=pl.Buffered(1))


def _fill_halo(scr, cur, prev8, next8, first, last):
    tb = cur.shape[0]
    scr[pl.ds(SUBLANES, tb), :] = cur
    scr[pl.ds(0, SUBLANES), :] = jnp.where(first, 0.0, prev8)
    scr[pl.ds(SUBLANES + tb, SUBLANES), :] = jnp.where(last, 0.0, next8)


def _halo_specs(width, col_block, tb, t_len, bsz):
    nb = t_len // tb
    r8 = tb // SUBLANES
    tot8 = t_len // SUBLANES

    def cur(b, i):
        return (b * nb + i, col_block)

    def prev(b, i):
        return (jnp.maximum(b * tot8 + i * r8 - 1, 0), col_block)

    def nxt(b, i):
        return (jnp.minimum(b * tot8 + (i + 1) * r8, tot8 * bsz - 1), col_block)

    return [pl.BlockSpec((tb, width), cur), pl.BlockSpec((SUBLANES, width), prev),
            pl.BlockSpec((SUBLANES, width), nxt)]


def _gdn_pre_kernel(cur_ref, prev_ref, next_ref, ab_ref, cw_ref, alog_ref, dtb_ref,
                    q_ref, k_ref, v_ref, g_ref, scr):
    i = pl.program_id(1)
    tb = cur_ref.shape[0]
    _fill_halo(scr, cur_ref[...], prev_ref[...], next_ref[...], i == 0,
               i == pl.num_programs(1) - 1)
    width = cw_ref.shape[0]
    acc = None
    for j in range(width):
        term = scr[pl.ds(SUBLANES - width // 2 + j, tb), :] * cw_ref[pl.ds(j, 1), :]
        acc = term if acc is None else acc + term
    qkv = _silu(acc)
    hd = GDN_HEADS * GDN_DK
    for h in range(GDN_HEADS):
        sl = slice(h * GDN_DK, (h + 1) * GDN_DK)
        q = qkv[:, sl]
        q_ref[:, sl] = (q * lax.rsqrt(jnp.sum(q * q, -1, keepdims=True) + L2_EPS)
                        * GDN_DK ** -0.5).astype(q_ref.dtype)
        k = qkv[:, hd + h * GDN_DK: hd + (h + 1) * GDN_DK]
        k_ref[:, sl] = (k * lax.rsqrt(jnp.sum(k * k, -1, keepdims=True) + L2_EPS)).astype(k_ref.dtype)
    v_ref[...] = qkv[:, 2 * hd:].astype(v_ref.dtype)
    ab = ab_ref[...]
    lane = lax.broadcasted_iota(jnp.int32, ab.shape, 1)
    log_g = -jnp.exp(alog_ref[...]) * _softplus(ab + dtb_ref[...])
    g_ref[...] = jnp.where(lane < 2 * GDN_HEADS, log_g, _sigmoid(ab))


def _gdn_pre(proj, conv_w, a_log_row, dt_bias_row, bsz, t_len, tb=512):
    n = proj.shape[0]
    wq = 3 * GDN_HEADS * GDN_DK
    nb = t_len // tb
    specs = _halo_specs(wq, 0, tb, t_len, bsz)
    ab_col = (wq + GDN_HEADS * GDN_DK) // LANES
    out_spec = pl.BlockSpec((tb, 512), lambda b, i: (b * nb + i, 0))
    return pl.pallas_call(
        _gdn_pre_kernel, grid=(bsz, nb),
        in_specs=specs + [pl.BlockSpec((tb, LANES), lambda b, i: (b * nb + i, ab_col)),
                          pl.BlockSpec(conv_w.shape, lambda b, i: (0, 0)),
                          pl.BlockSpec((1, LANES), lambda b, i: (0, 0)),
                          pl.BlockSpec((1, LANES), lambda b, i: (0, 0))],
        out_specs=[out_spec, out_spec, out_spec,
                   pl.BlockSpec((tb, LANES), lambda b, i: (b * nb + i, 0))],
        out_shape=[jax.ShapeDtypeStruct((n, 512), BF16)] * 3 + [jax.ShapeDtypeStruct((n, LANES), F32)],
        scratch_shapes=[pltpu.VMEM((tb + 2 * SUBLANES, wq), F32)],
        compiler_params=_params("parallel", "parallel"), name="gdn_pre",
    )(proj, proj, proj, proj, conv_w, a_log_row, dt_bias_row)


def _gdn_scan_kernel(qf, kf, vf, gf, qb, kb, vb, gb, of, ob, state):
    n = pl.program_id(1)

    @pl.when(n == 0)
    def _():
        state[...] = jnp.zeros_like(state)

    c = GDN_CHUNK
    nh = GDN_HEADS
    hc = nh * c
    wide = nh * GDN_DK
    row = lax.broadcasted_iota(jnp.int32, (c, c), 0)
    col = lax.broadcasted_iota(jnp.int32, (c, c), 1)
    lane = lax.broadcasted_iota(jnp.int32, (c, LANES), 1)
    rr = lax.broadcasted_iota(jnp.int32, (hc, hc), 0)
    cc = lax.broadcasted_iota(jnp.int32, (hc, hc), 1)
    same = (rr // c) == (cc // c)
    head_lanes = (lax.broadcasted_iota(jnp.int32, (hc, wide), 0) // c
                  == lax.broadcasted_iota(jnp.int32, (hc, wide), 1) // GDN_DK)
    lane_blk = lax.broadcasted_iota(jnp.int32, (1, hc), 1) // c
    row_cat = lax.broadcasted_iota(jnp.int32, (c, hc), 0)
    lane_cat = lax.broadcasted_iota(jnp.int32, (c, hc), 1)
    col_cat = lane_cat % c
    blk_cat = lane_cat // c
    eye_cat = (col_cat == row_cat).astype(F32)

    def col_cat_of(x, first):
        out = jnp.zeros((c, hc), F32)
        for h in range(nh):
            out = jnp.where(blk_cat == h, _col(x, first + h, lane), out)
        return out

    def stack_wide(x):
        return jnp.where(head_lanes, jnp.concatenate([x] * nh, axis=0), 0.0)

    def stack_narrow(x):
        return jnp.concatenate([x[:, h * GDN_DK:(h + 1) * GDN_DK] for h in range(nh)], axis=0)

    dirs = ((qf, kf, vf, gf, of), (qb, kb, vb, gb, ob))

    def chain(bb, d, rows):
        q_ref, k_ref, v_ref, g_ref, o_ref = dirs[d]
        incl1 = (col <= row) if d == 0 else (col >= row)
        incl = (col_cat <= row_cat) if d == 0 else (col_cat >= row_cat)
        strict = (col_cat < row_cat) if d == 0 else (col_cat > row_cat)
        gates = g_ref[bb, rows, :]
        tri = incl1.astype(BF16)
        gam_all = _sel_mm(tri, gates)
        tri4 = jnp.concatenate([tri] * nh, axis=0)
        gam_rows = _sel_mm(tri4, gates.T[:2 * nh], _NT, sel_first=False)
        yield
        gam = jnp.concatenate([_col(gam_all, d * nh + h, lane) for h in range(nh)], axis=0)
        beta = jnp.concatenate([_col(gates, 2 * nh + h, lane) for h in range(nh)], axis=0)
        gam_row = jnp.zeros((1, hc), F32)
        for h in range(nh):
            gam_row = jnp.where(lane_blk == h, gam_rows[d * nh + h:d * nh + h + 1, :], gam_row)
        last = c - 1 if d == 0 else 0
        g_last = jnp.concatenate(
            [jnp.broadcast_to(gam[h * c + last:h * c + last + 1, :], (c, 1)) for h in range(nh)], axis=0)
        g_last_s = jnp.concatenate(
            [jnp.broadcast_to(gam[h * c + last:h * c + last + 1, :], (GDN_DK, 1)) for h in range(nh)],
            axis=0)
        diff = col_cat_of(gam_all, d * nh) - gam_row
        decay = jnp.where(incl, jnp.exp(jnp.where(incl, diff, 0.0)), 0.0)
        e_gam = jnp.exp(gam)
        k_in, q_in = k_ref[bb, rows, :], q_ref[bb, rows, :]
        k_w = stack_wide(k_in)
        kq = _mm(jnp.concatenate([k_in, q_in], axis=0), k_w, _NT)
        yield
        a_cat = jnp.where(strict, kq[:c] * decay * col_cat_of(gates, 2 * nh), 0.0)
        attn = _expand_bd(kq[c:] * decay, same)
        t_cat = yield from _unit_tri_inv_cat(a_cat, eye_cat, same, c)
        yield
        rhs = jnp.concatenate([stack_narrow(v_ref[bb, rows, :]) * beta,
                               stack_narrow(k_in) * (beta * e_gam)], axis=1)
        sol = _mm(_expand_bd(t_cat, same), rhs)
        yield
        u, w = sol[:, :GDN_DK], sol[:, GDN_DK:]
        s = state[bb, d]
        w_w = jnp.where(head_lanes, jnp.concatenate([w] * nh, axis=1), 0.0)
        ws = _mm(jnp.concatenate([w_w, stack_wide(q_in) * e_gam], axis=0), s)
        yield
        v_new = u - ws[:hc]
        out = ws[hc:] + _mm(attn, v_new)
        o_ref[bb, rows, :] = jnp.concatenate([out[h * c:(h + 1) * c] for h in range(nh)], axis=1)
        k_dec = k_w * jnp.exp(g_last - gam)
        state[bb, d] = s * jnp.exp(g_last_s) + _mm(k_dec, v_new, _TN)

    def step(j, carry):
        def rows(d):
            return pl.ds(pl.multiple_of((j if d == 0 else SCAN_CHUNKS - 1 - j) * c, c), c)

        _interleave([chain(bb, d, rows(d)) for bb in range(qf.shape[0]) for d in range(2)])
        return carry

    lax.fori_loop(0, SCAN_CHUNKS, step, 0)


def _gdn_scan(q, k, v, gates, bsz, t_len, bb=SCAN_BATCH):
    n = q.shape[0]
    c = GDN_CHUNK * SCAN_CHUNKS
    nc = t_len // c
    bb = math.gcd(bb, bsz)
    q, k, v, gates = (a.reshape(bsz, t_len, a.shape[1]) for a in (q, k, v, gates))

    def fwd(g, i):
        return (g, i, 0)

    def bwd(g, i):
        return (g, nc - 1 - i, 0)

    def specs(idx):
        wide = pl.BlockSpec((bb, c, 512), idx)
        return [wide, wide, wide, pl.BlockSpec((bb, c, LANES), idx)]

    of, ob = pl.pallas_call(
        _gdn_scan_kernel, grid=(bsz // bb, nc),
        in_specs=specs(fwd) + specs(bwd),
        out_specs=[pl.BlockSpec((bb, c, 512), fwd), pl.BlockSpec((bb, c, 512), bwd)],
        out_shape=[jax.ShapeDtypeStruct((bsz, t_len, 512), F32)] * 2,
        scratch_shapes=[pltpu.VMEM((bb, 2, GDN_HEADS * GDN_DK, GDN_DK), F32)],
        compiler_params=_params("parallel", "arbitrary"), name="gdn_scan",
    )(q, k, v, gates, q, k, v, gates)
    return of.reshape(n, 512), ob.reshape(n, 512)


_fft_mm = _mm
HY_CB = 16
HY_CHAINS = 2


def _hy_filter_kernel(z_ref, w1_ref, b1_ref, w2_ref, b2_ref, w3_ref, b3_ref, fr_ref, w4t_ref,
                      dl_ref, o_ref, *, t_len):
    i = pl.program_id(0)
    tl = o_ref.shape[1]
    fr = fr_ref[...]
    h = jnp.sin(fr * (_mm3(z_ref[...], w1_ref[...]) + b1_ref[...]))
    h = jnp.sin(fr * (_mm3(h, w2_ref[...]) + b2_ref[...]))
    h = jnp.sin(fr * (_mm3(h, w3_ref[...]) + b3_ref[...]))
    ht = _mm3(w4t_ref[...], h, _NT)
    pos = (lax.broadcasted_iota(jnp.int32, (1, tl), 1) + i * tl).astype(F32)
    t_norm = pos / float(max(t_len - 1, 1))
    o_ref[...] = ht * jnp.exp(-t_norm * dl_ref[...])


def _hy_filters(z, w1p, b1, w2, b2, w3, b3, freq, w4t, deltas, t_len):
    tl = min(512, t_len)
    nch = w4t.shape[0]
    full = lambda a: pl.BlockSpec(a.shape, lambda i: (0,) * a.ndim)
    return pl.pallas_call(
        functools.partial(_hy_filter_kernel, t_len=t_len), grid=(t_len // tl,),
        in_specs=[pl.BlockSpec((tl, z.shape[1]), lambda i: (i, 0))]
        + [full(a) for a in (w1p, b1, w2, b2, w3, b3, freq, w4t, deltas)],
        out_specs=pl.BlockSpec((nch, tl), lambda i: (0, i)),
        out_shape=jax.ShapeDtypeStruct((nch, t_len), F32),
        compiler_params=_params("parallel"), name="hy_filter",
    )(z, w1p, b1, w2, b2, w3, b3, freq, w4t, deltas)


def _fft_fwd(x3, x3i, fst, twr, twi, wbig):
    cb, r = x3.shape[0], x3.shape[1]
    n1 = twr.shape[0]
    xcat = jnp.concatenate([x3[c] for c in range(cb)], axis=1)
    if x3i is None:
        y = _fft_mm(fst[:, :r], xcat)
    else:
        xicat = jnp.concatenate([x3i[c] for c in range(cb)], axis=1)
        y = _fft_mm(fst, jnp.concatenate([xcat, xicat], axis=0))
    yield
    yr, yi = y[:n1], y[n1:]
    twr_t = jnp.concatenate([twr] * cb, axis=1)
    twi_t = jnp.concatenate([twi] * cb, axis=1)
    tr = yr * twr_t - yi * twi_t
    ti = yr * twi_t + yi * twr_t
    cat = jnp.concatenate(
        [jnp.concatenate([tr[:, c * LANES:(c + 1) * LANES], ti[:, c * LANES:(c + 1) * LANES]], axis=1)
         for c in range(cb)], axis=0)
    z = _fft_mm(cat, wbig)
    yield
    return z[:, :LANES].reshape(cb, n1, LANES), z[:, LANES:].reshape(cb, n1, LANES)


def _fft_inv(zr, zi, gst, twr, twi, wbigc):
    cb, n1 = zr.shape[0], zr.shape[1]
    r = gst.shape[0] // 2
    cat = jnp.concatenate([zr, zi], axis=-1).reshape(cb * n1, 2 * LANES)
    a = _fft_mm(cat, wbigc)
    yield
    ar, ai = a[:, :LANES].reshape(cb, n1, LANES), a[:, LANES:].reshape(cb, n1, LANES)
    tr = ar * twr + ai * twi
    ti = ai * twr - ar * twi
    tt = jnp.concatenate([jnp.concatenate([tr[c] for c in range(cb)], axis=1),
                          jnp.concatenate([ti[c] for c in range(cb)], axis=1)], axis=0)
    out = _fft_mm(gst, tt)
    yield
    return (jnp.stack([out[:r, c * LANES:(c + 1) * LANES] for c in range(cb)]),
            jnp.stack([out[r:, c * LANES:(c + 1) * LANES] for c in range(cb)]))


def _hy_spec_kernel(hf_ref, hb_ref, fst_ref, twr_ref, twi_ref, wbig_ref, hr_ref, hi_ref):
    args = (fst_ref[...], twr_ref[...], twi_ref[...], wbig_ref[...])
    out = {}

    def chain(name, ref):
        out[name] = yield from _fft_fwd(ref[...], None, *args)

    _interleave([chain("f", hf_ref), chain("b", hb_ref)])
    hr_ref[...] = out["f"][0] + out["b"][0]
    hi_ref[...] = out["f"][1] - out["b"][1]


def _hy_spectra(filt3, consts, cb=2 * HY_CB):
    fst, gst, twr, twi, wbig, wbigc = consts
    r = filt3.shape[1]
    n1 = twr.shape[0]
    nj = HY_DIM // cb
    full = lambda a: pl.BlockSpec(a.shape, lambda o, j: (0,) * a.ndim)
    out_spec = pl.BlockSpec((cb, n1, LANES), lambda o, j: (o * nj + j, 0, 0))
    return pl.pallas_call(
        _hy_spec_kernel, grid=(2, nj),
        in_specs=[pl.BlockSpec((cb, r, LANES), lambda o, j: (2 * o * nj + j, 0, 0)),
                  pl.BlockSpec((cb, r, LANES), lambda o, j: ((2 * o + 1) * nj + j, 0, 0)),
                  full(fst), full(twr), full(twi), full(wbig)],
        out_specs=[out_spec, out_spec],
        out_shape=[jax.ShapeDtypeStruct((2 * HY_DIM, n1, LANES), F32)] * 2,
        compiler_params=_params("parallel", "parallel"), name="hy_spec",
    )(filt3, filt3, fst, twr, twi, wbig)


def _tshift(x3, back, lane, rowi):
    r = x3.shape[1]
    if back:
        rl = pltpu.roll(x3, 1, 2)
        rr = pltpu.roll(rl, 1, 1)
        edge = lane == 0
        out = jnp.where(edge, rr, rl)
        return jnp.where(edge & (rowi == 0), 0.0, out)
    rl = pltpu.roll(x3, LANES - 1, 2)
    rr = pltpu.roll(rl, r - 1, 1)
    edge = lane == LANES - 1
    out = jnp.where(edge, rr, rl)
    return jnp.where(edge & (rowi == r - 1), 0.0, out)


def _hy_conv_kernel(x1_ref, x2_ref, v_ref, p1_ref, p2_ref, pv_ref, sk_ref, h1r_ref, h1i_ref,
                    h2r_ref, h2i_ref, fst_ref, gst_ref, twr_ref, twi_ref, wbig_ref, wbigc_ref,
                    o_ref):
    cb, nb, r, _ = x1_ref.shape
    half = cb // HY_CHAINS
    shape = (half, r, LANES)
    lane = lax.broadcasted_iota(jnp.int32, shape, 2)
    rowi = lax.broadcasted_iota(jnp.int32, shape, 1)
    fwd = (fst_ref[...], twr_ref[...], twi_ref[...], wbig_ref[...])
    inv = (gst_ref[...], twr_ref[...], twi_ref[...], wbigc_ref[...])

    def chain(lo):
        ch = pl.ds(lo, half)

        def short_conv(x_ref, p_ref):
            p = p_ref[ch]
            out = []
            for b in range(nb):
                x = x_ref[ch, b].astype(F32)
                out.append(p[:, 0:1, :] * _tshift(x, True, lane, rowi) + p[:, 1:2, :] * x
                           + p[:, 2:3, :] * _tshift(x, False, lane, rowi) + p[:, 3:4, :])
            return out

        x1 = short_conv(x1_ref, p1_ref)
        x2 = short_conv(x2_ref, p2_ref)
        z = short_conv(v_ref, pv_ref)
        sk = sk_ref[ch]
        for o, (gate, hr_ref, hi_ref) in enumerate(((x1, h1r_ref, h1i_ref), (x2, h2r_ref, h2i_ref))):
            zr, zi = yield from _fft_fwd(z[0], z[1], *fwd)
            hr, hi = hr_ref[ch], hi_ref[ch]
            y = yield from _fft_inv(zr * hr - zi * hi, zr * hi + zi * hr, *inv)
            z = [gate[b] * (y[b] + sk[:, o:o + 1, :] * z[b]) for b in range(nb)]
        for b in range(nb):
            o_ref[ch, b] = z[b].astype(o_ref.dtype)

    _interleave([chain(i * half) for i in range(HY_CHAINS)])


def _hy_conv(projt4, pp, ps, hr, hi, consts, cb=HY_CB):
    fst, gst, twr, twi, wbig, wbigc = consts
    _, bsz, r, _ = projt4.shape
    n1 = twr.shape[0]
    nj = HY_DIM // cb
    full = lambda a: pl.BlockSpec(a.shape, lambda j, b: (0,) * a.ndim)

    assert bsz % 2 == 0, "sequences are transformed in pairs"

    def xs(off):
        return pl.BlockSpec((cb, 2, r, LANES), lambda j, b: (off * nj + j, b, 0, 0))

    def ps_(off):
        return pl.BlockSpec((cb, SUBLANES, LANES), lambda j, b: (off * nj + j, 0, 0))

    def hs(off):
        return pl.BlockSpec((cb, n1, LANES), lambda j, b: (off * nj + j, 0, 0))

    return pl.pallas_call(
        _hy_conv_kernel, grid=(nj, bsz // 2),
        in_specs=[xs(0), xs(1), xs(2), ps_(0), ps_(1), ps_(2), ps_(0), hs(0), hs(0), hs(1), hs(1),
                  full(fst), full(gst), full(twr), full(twi), full(wbig), full(wbigc)],
        out_specs=pl.BlockSpec((cb, 2, r, LANES), lambda j, b: (j, b, 0, 0)),
        out_shape=jax.ShapeDtypeStruct((HY_DIM, bsz, r, LANES), BF16),
        compiler_params=_params("parallel", "arbitrary"), name="hy_conv",
    )(projt4, projt4, projt4, pp, pp, pp, ps, hr, hi, hr, hi, fst, gst, twr, twi, wbig, wbigc)


def _fft_consts(t_len):
    n = 2 * t_len
    n1 = n // LANES
    r = t_len // LANES
    k1 = np.arange(n1)[:, None]
    th1 = 2.0 * np.pi * k1 * np.arange(r)[None, :] / n1
    c1, s1 = np.cos(th1), np.sin(th1)
    fst = np.block([[c1, s1], [-s1, c1]])
    gst = np.block([[c1.T, -s1.T], [s1.T, c1.T]]) / n
    tht = 2.0 * np.pi * k1 * np.arange(LANES)[None, :] / n
    twr, twi = np.cos(tht), -np.sin(tht)
    th2 = 2.0 * np.pi * np.arange(LANES)[:, None] * np.arange(LANES)[None, :] / LANES
    cr, ci = np.cos(th2), -np.sin(th2)
    wbig = np.block([[cr, ci], [-ci, cr]])
    wbigc = np.block([[cr, -ci], [ci, cr]])
    return tuple(jnp.asarray(a, F32) for a in (fst, gst, twr, twi, wbig, wbigc))


def _hy_positions(t_len):
    pos = np.arange(t_len, dtype=np.float32)[:, None]
    t = pos / np.float32(max(t_len - 1, 1))
    bands = np.linspace(1e-4, HY_BANDS - 1, HY_BANDS, dtype=np.float32)[None, :]
    ang = bands * np.float32(2.0 * math.pi / t_len) * pos
    z = np.concatenate([t, np.cos(ang), -np.sin(ang)], -1).astype(np.float32)
    return jnp.asarray(np.pad(z, ((0, 0), (0, HY_FILT_HID - HY_EMB))))


def _hy_deltas():
    max_decay = math.log(HY_DECAY_TARGET) / HY_FAST_DECAY_PCT
    min_decay = math.log(HY_DECAY_TARGET) / HY_SLOW_DECAY_PCT
    d = np.abs(np.linspace(min_decay, max_decay, HY_DIM, dtype=np.float32))
    return jnp.asarray(np.tile(d, 4)[:, None])


def _lane_rep(a, rows=SUBLANES):
    ch, k = a.shape
    a = jnp.pad(a, ((0, 0), (0, rows - k)))
    return jnp.broadcast_to(a[:, :, None], (ch, rows, LANES))


def _out_even_kernel(x_ref, yat_ref, of_ref, ob_ref, zg_ref, nw_ref, wa_ref, wb_ref, g_ref, b_ref,
                     w1_ref, w2_ref, g2_ref, b2_ref, o_ref):
    mix = _mm(yat_ref[...], wa_ref[...], _TN)
    nw = nw_ref[...]
    for h in range(GDN_HEADS):
        sl = slice(h * GDN_DK, (h + 1) * GDN_DK)
        o = of_ref[:, sl] + ob_ref[:, sl]
        y = o * lax.rsqrt(jnp.mean(o * o, -1, keepdims=True) + RMS_EPS) * nw * _silu(zg_ref[:, sl])
        mix = mix + _mm(y, wb_ref[sl, :])
    x1 = _layer_norm(DEEPNORM_ALPHA * x_ref[...] + mix, g_ref[...], b_ref[...])
    o_ref[...] = _mlp_tail(x1, w1_ref, w2_ref, g2_ref, b2_ref)


def _out_even(x2, yat, of, ob, projm, norm_w, wa, wb, g, b, mlp, tm=512):
    n, d = x2.shape
    tm = min(tm, n)
    row = lambda w: pl.BlockSpec((tm, w), lambda i: (i, 0))
    consts = (norm_w, wa, wb, g, b) + tuple(mlp)
    return pl.pallas_call(
        _out_even_kernel, grid=(n // tm,),
        in_specs=[row(d), pl.BlockSpec((HY_DIM, tm), lambda i: (0, i)), row(512), row(512),
                  pl.BlockSpec((tm, 512), lambda i: (i, 3))] + [_resident(a) for a in consts],
        out_specs=row(d), out_shape=jax.ShapeDtypeStruct((n, d), F32),
        compiler_params=_params("parallel"), name="out_even",
    )(x2, yat, of, ob, projm, *consts)


def _even_layer(x2, bsz, t_len, w_in, hy_conv_w, hy_conv_b, f_w1, f_b1, f_w2, f_b2, f_w3, f_b3,
                f_w4, f_freq, hy_skip, gdn_conv_w, gdn_a_log, gdn_dt_bias, gdn_norm_w, w_out,
                ln_g, ln_b, mlp):
    d = x2.shape[1]
    hyw = 3 * HY_DIM
    wm = jnp.concatenate([w_in[:, hyw:], jnp.zeros((d, LANES - 12), F32)], axis=1).astype(BF16)
    wt = w_in[:, :hyw].T.astype(BF16)
    projm, projt = _project(x2, wm, wt)
    consts = _fft_consts(t_len)
    w1p = jnp.pad(f_w1, ((0, HY_FILT_HID - HY_EMB), (0, 0)))
    filt = _hy_filters(_hy_positions(t_len), w1p, f_b1[None], f_w2, f_b2[None], f_w3, f_b3[None],
                       f_freq[None], f_w4.T, _hy_deltas(), t_len)
    r = t_len // LANES
    hr, hi = _hy_spectra(filt.reshape(4 * HY_DIM, r, LANES), consts)
    pp = _lane_rep(jnp.concatenate([hy_conv_w.T, hy_conv_b[:, None]], axis=1))
    ps = _lane_rep(hy_skip.T)
    yat = _hy_conv(projt.reshape(hyw, bsz, r, LANES), pp, ps, hr, hi, consts)
    yat = yat.reshape(HY_DIM, bsz * t_len)
    alog = jnp.zeros((1, LANES), F32).at[0, :8].set(gdn_a_log.reshape(-1))
    dtb = jnp.zeros((1, LANES), F32).at[0, :8].set(gdn_dt_bias.reshape(-1))
    qn, kn, vn, gates = _gdn_pre(projm, gdn_conv_w, alog, dtb, bsz, t_len)
    of, ob = _gdn_scan(qn, kn, vn, gates, bsz, t_len)
    return _out_even(x2, yat, of, ob, projm, gdn_norm_w[None], w_out[:HY_DIM].astype(BF16),
                     w_out[HY_DIM:].astype(BF16), ln_g[None], ln_b[None], mlp)


def _hg_scan_kernel(qf_ref, ff_ref, vf_ref, qb_ref, fb_ref, vb_ref, low_ref, of_ref, ob_ref,
                    state, q_s, k_s, g_s, *, layer):
    n = pl.program_id(1)

    @pl.when(n == 0)
    def _():
        state[...] = jnp.zeros_like(state)

    tb = qf_ref.shape[0]
    c = HG_CHUNK
    nsub = tb // c
    low = low_ref[...]
    e = jnp.exp(low - jnp.max(low, axis=0, keepdims=True))
    p = e / jnp.sum(e, axis=0, keepdims=True)
    lb = jnp.sum(p[1:layer + 1], axis=0, keepdims=True)
    rb = lax.broadcasted_iota(jnp.int32, (tb, tb), 0)
    cb = lax.broadcasted_iota(jnp.int32, (tb, tb), 1)
    same = (rb // c) == (cb // c)
    hb = c // 2
    rowh = lax.broadcasted_iota(jnp.int32, (hb, 1), 0)
    dirs = ((qf_ref, ff_ref, vf_ref, of_ref), (qb_ref, fb_ref, vb_ref, ob_ref))
    for d, (q_ref, f_ref, v_ref, o_ref) in enumerate(dirs):
        fg = lb + (1.0 - lb) * _sigmoid(f_ref[...])
        tri = (same & ((cb <= rb) if d == 0 else (cb >= rb))).astype(BF16)
        g_s[d] = _sel_mm(tri, jnp.log(fg))
        k_s[d] = 1.0 - fg
        q_s[d] = _silu(q_ref[...])

    def body(s, carry):
        def sub_rows(d, u):
            idx = s * HG_SUBS + u
            sub = idx if d == 0 else nsub - 1 - idx
            return pl.ds(pl.multiple_of(sub * c, c), c)

        def pairwise(d, qb, gkb, gb, vb):
            acc = None
            for j in range(hb):
                dec = jnp.exp(gb - gkb[j:j + 1, :])
                a_col = jnp.sum(qb * dec, axis=1, keepdims=True)
                a_col = jnp.where((rowh >= j) if d == 0 else (rowh <= j), a_col, 0.0)
                term = a_col * vb[j:j + 1, :]
                acc = term if acc is None else acc + term
            return acc

        def chain(d, h):
            v_ref, o_ref = dirs[d][2], dirs[d][3]
            sl = slice(h * HG_DK, (h + 1) * HG_DK)
            st = state[d, h]
            pending = []
            for u in range(HG_SUBS):
                rows = sub_rows(d, u)
                qc, kc, gc, vc = q_s[d, rows, sl], k_s[d, rows, sl], g_s[d, rows, sl], v_ref[rows, sl]
                g_last = gc[c - 1:c, :] if d == 0 else gc[0:1, :]
                inter = _mm(qc * jnp.exp(gc), st, _NT)
                upd = _mm(vc, kc * jnp.exp(g_last - gc), _TN)
                yield
                st = st * jnp.exp(g_last) + upd
                pending.append((rows, qc, kc, gc, vc, inter))
            state[d, h] = st
            lo, hi = slice(0, hb), slice(hb, c)
            src, dst, bnd = (lo, hi, hb - 1) if d == 0 else (hi, lo, hb)
            for rows, qc, kc, gc, vc, inter in pending:
                g_bnd = gc[bnd:bnd + 1, :]
                q_x = qc[dst] * jnp.exp(gc[dst] - g_bnd)
                k_y = kc[src] * jnp.exp(g_bnd - gc[src])
                v_src = vc[src]
                cross = None
                for j in range(hb):
                    term = jnp.sum(q_x * k_y[j:j + 1, :], axis=1, keepdims=True) * v_src[j:j + 1, :]
                    cross = term if cross is None else cross + term
                gk = gc - jnp.log(kc)
                o_lo = pairwise(d, qc[lo], gk[lo], gc[lo], vc[lo])
                o_hi = pairwise(d, qc[hi], gk[hi], gc[hi], vc[hi])
                if d == 0:
                    o_hi = o_hi + cross
                else:
                    o_lo = o_lo + cross
                o_ref[rows, sl] = inter + jnp.concatenate([o_lo, o_hi], axis=0)

        _interleave([chain(d, h) for d in range(2) for h in range(HG_HEADS)])
        return carry

    lax.fori_loop(0, nsub // HG_SUBS, body, 0)


def _hg_scan(proj, hg_lower, layer, bsz, t_len, tb=256):
    n = proj.shape[0]
    nb = t_len // tb

    def spec(colb, rev):
        if rev:
            return pl.BlockSpec((tb, 512), lambda b, i: (b * nb + nb - 1 - i, colb))
        return pl.BlockSpec((tb, 512), lambda b, i: (b * nb + i, colb))

    return pl.pallas_call(
        functools.partial(_hg_scan_kernel, layer=layer), grid=(bsz, nb),
        in_specs=[spec(0, False), spec(1, False), spec(3, False),
                  spec(0, True), spec(2, True), spec(3, True),
                  pl.BlockSpec(hg_lower.shape, lambda b, i: (0, 0))],
        out_specs=[spec(0, False), spec(0, True)],
        out_shape=[jax.ShapeDtypeStruct((n, 512), F32)] * 2,
        scratch_shapes=[pltpu.VMEM((2, HG_HEADS, HG_DK, HG_DK), F32)]
        + [pltpu.VMEM((2, tb, 512), F32)] * 3,
        compiler_params=_params("parallel", "arbitrary"), name="hg_scan",
    )(proj, proj, proj, proj, proj, proj, hg_lower)


def _rw_pre_kernel(rc, rp, rn, kc, kp, kn, vc, vp, vn, lc, lp, ln, mu_ref, w0_ref, w2_ref,
                   a0_ref, a2_ref, g2_ref, kk_ref, ka_ref, bd_ref,
                   r_o, k_o, v_o, kk_o, kka_o, lwf_o, lwb_o, g_o, scr):
    i = pl.program_id(1)
    first = i == 0
    last = i == pl.num_programs(1) - 1
    tb = rc.shape[0]

    def shifted(cur_ref, prev_ref, next_ref, col0):
        w = cur_ref.shape[1]
        cur = cur_ref[...]
        scr[pl.ds(SUBLANES, tb), pl.ds(0, w)] = cur
        scr[pl.ds(0, SUBLANES), pl.ds(0, w)] = jnp.where(first, 0.0, prev_ref[...])
        scr[pl.ds(SUBLANES + tb, SUBLANES), pl.ds(0, w)] = jnp.where(last, 0.0, next_ref[...])
        prev = scr[pl.ds(SUBLANES - 1, tb), pl.ds(0, w)]
        nxt = scr[pl.ds(SUBLANES + 1, tb), pl.ds(0, w)]
        mu = mu_ref[:, col0:col0 + w]
        return cur + mu[0:1] * (prev - cur) + mu[1:2] * (nxt - cur)

    r = shifted(rc, rp, rn, 0)
    k = shifted(kc, kp, kn, RW_DIM)
    v = shifted(vc, vp, vn, 2 * RW_DIM)
    lo = shifted(lc, lp, ln, 3 * RW_DIM)
    lo_wa, lo_g = lo[:, :LANES], lo[:, LANES:]
    th = jnp.tanh(lo_wa)
    for d, o_ref in enumerate((lwf_o, lwb_o)):
        w_raw = w0_ref[d:d + 1, :] + _mm(th, w2_ref[d])
        o_ref[...] = -math.exp(-0.5) * _sigmoid(w_raw)
    a = _sigmoid(a0_ref[...] + _mm(lo_wa, a2_ref[...]))
    g_o[...] = _mm(_sigmoid(lo_g), g2_ref[...]).astype(g_o.dtype)
    kx = k * kk_ref[...]
    kk = kx * lax.rsqrt(_mm(kx * kx, bd_ref[...]) + L2_EPS)
    r_o[...] = r.astype(r_o.dtype)
    v_o[...] = v.astype(v_o.dtype)
    k_o[...] = (k * (1.0 + (a - 1.0) * ka_ref[...])).astype(k_o.dtype)
    kk_o[...] = kk.astype(kk_o.dtype)
    kka_o[...] = (kk * a).astype(kka_o.dtype)


def _rw_pre(proj, mu, w0, w2p, a0, a2p, g2, k_k, k_a, bd, bsz, t_len, tb=512):
    n = proj.shape[0]
    nb = t_len // tb
    base = 2560
    specs = []
    for off, w in ((0, 512), (512, 512), (1024, 512), (1536, 256)):
        specs += _halo_specs(w, (base + off) // w, tb, t_len, bsz)
    full = lambda a: pl.BlockSpec(a.shape, lambda b, i: (0,) * a.ndim)
    out_spec = pl.BlockSpec((tb, RW_DIM), lambda b, i: (b * nb + i, 0))
    consts = (mu, w0, w2p, a0, a2p, g2, k_k, k_a, bd)
    return pl.pallas_call(
        _rw_pre_kernel, grid=(bsz, nb),
        in_specs=specs + [full(a) for a in consts],
        out_specs=[out_spec] * 8,
        out_shape=[jax.ShapeDtypeStruct((n, RW_DIM), dt) for dt in (BF16,) * 5 + (F32, F32, BF16)],
        scratch_shapes=[pltpu.VMEM((tb + 2 * SUBLANES, RW_DIM), F32)],
        compiler_params=_params("parallel", "parallel"), name="rw_pre",
    )(*([proj] * 12), *consts)


def _rw_scan_kernel(rf, kf, vf, bf, af, wf, rb, kb, vb, bb, ab, wb, yf, yb, state):
    n = pl.program_id(1)

    @pl.when(n == 0)
    def _():
        state[...] = jnp.zeros_like(state)

    c = RW_CHUNK
    gw = RW_GROUP * RW_HD
    gc = RW_GROUP * c
    row = lax.broadcasted_iota(jnp.int32, (c, c), 0)
    col = lax.broadcasted_iota(jnp.int32, (c, c), 1)
    rr = lax.broadcasted_iota(jnp.int32, (gc, gc), 0)
    cc = lax.broadcasted_iota(jnp.int32, (gc, gc), 1)
    same = (rr // c) == (cc // c)
    head_lanes = (lax.broadcasted_iota(jnp.int32, (gc, gw), 0) // c
                  == lax.broadcasted_iota(jnp.int32, (gc, gw), 1) // RW_HD)
    rs = lax.broadcasted_iota(jnp.int32, (gw, gw), 0)
    cs = lax.broadcasted_iota(jnp.int32, (gw, gw), 1)
    head_blocks = ((rs // RW_HD) == (cs // RW_HD)).astype(F32)
    row_cat = lax.broadcasted_iota(jnp.int32, (c, gc), 0)
    col_cat = lax.broadcasted_iota(jnp.int32, (c, gc), 1) % c
    eye_cat = (col_cat == row_cat).astype(F32)

    def stack(x):
        return jnp.where(head_lanes, jnp.concatenate([x] * RW_GROUP, axis=0), 0.0)

    dirs = ((rf, kf, vf, bf, af, wf, yf), (rb, kb, vb, bb, ab, wb, yb))
    def chain(ib, d, p, rows):
        r_ref, k_ref, v_ref, b_ref, a_ref, w_ref, y_ref = dirs[d]
        sl = slice(p * gw, (p + 1) * gw)
        incl = (col_cat <= row_cat) if d == 0 else (col_cat >= row_cat)
        strict = (col_cat < row_cat) if d == 0 else (col_cat > row_cat)
        lw = w_ref[ib, rows, sl]
        tri = ((col <= row) if d == 0 else (col >= row)).astype(BF16)
        g_in = _sel_mm(tri, lw)
        yield
        g_ex = g_in - lw
        g_end = g_in[c - 1:c, :] if d == 0 else g_in[0:1, :]
        e_neg = jnp.exp(-g_in)
        lhs = jnp.concatenate([r_ref[ib, rows, sl] * jnp.exp(g_in), b_ref[ib, rows, sl] * jnp.exp(g_ex)],
                              axis=0)
        ak = _mm(lhs, stack(k_ref[ib, rows, sl] * e_neg), _NT)
        yield
        aa = _mm(lhs, stack(a_ref[ib, rows, sl] * e_neg), _NT)
        yield
        m = state[ib, d, p]
        xm = _mm(lhs, m, _NT)
        yield
        a_rk = jnp.where(incl, ak[:c], 0.0)
        a_bk = jnp.where(strict, ak[c:], 0.0)
        av = _mm(jnp.concatenate([a_rk, a_bk], axis=0), stack(v_ref[ib, rows, sl]))
        yield
        rhs_u = xm[c:] + av[c:]
        y_part = xm[:c] + av[:c]
        a_ra = jnp.where(incl, aa[:c], 0.0)
        a_ba = jnp.where(strict, aa[c:], 0.0)
        t_inv = yield from _unit_tri_inv_cat(a_ba, eye_cat, same, c)
        yield
        u = _mm(t_inv, stack(rhs_u))
        yield
        y_ref[ib, rows, sl] = y_part - _mm(a_ra, stack(u))
        yield
        e_end = jnp.exp(g_end - g_in)
        upd = _mm(jnp.concatenate([v_ref[ib, rows, sl].astype(F32), u], axis=0),
                  jnp.concatenate([k_ref[ib, rows, sl] * e_end, -(a_ref[ib, rows, sl] * e_end)], axis=0),
                  _TN)
        state[ib, d, p] = m * jnp.exp(g_end) + upd * head_blocks

    def step(j, carry):
        def rows(d):
            return pl.ds(pl.multiple_of((j if d == 0 else SCAN_CHUNKS - 1 - j) * c, c), c)

        _interleave([chain(ib, d, p, rows(d)) for ib in range(rf.shape[0]) for d in range(2)
                     for p in range(RW_DIM // gw)])
        return carry

    lax.fori_loop(0, SCAN_CHUNKS, step, 0)


def _rw_scan(r, k, v, kk, kka, lwf, lwb, bsz, t_len, bb=SCAN_BATCH):
    n = r.shape[0]
    c = RW_CHUNK * SCAN_CHUNKS
    gw = RW_GROUP * RW_HD
    nc = t_len // c
    bb = math.gcd(bb, bsz)
    r, k, v, kk, kka, lwf, lwb =(a.reshape(bsz, t_len, RW_DIM) for a in (r, k, v, kk, kka, lwf, lwb))
    fwd = pl.BlockSpec((bb, c, RW_DIM), lambda g, i: (g, i, 0))
    bwd = pl.BlockSpec((bb, c, RW_DIM), lambda g, i: (g, nc - 1 - i, 0))
    yf, yb = pl.pallas_call(
        _rw_scan_kernel, grid=(bsz // bb, nc),
        in_specs=[fwd] * 6 + [bwd] * 6, out_specs=[fwd, bwd],
        out_shape=[jax.ShapeDtypeStruct((bsz, t_len, RW_DIM), F32)] * 2,
        scratch_shapes=[pltpu.VMEM((bb, 2, RW_DIM // gw, gw, gw), F32)],
        compiler_params=_params("parallel", "arbitrary"), name="rw_scan",
    )(r, k, v, kk, kka, lwf, r, k, v, kk, kka, lwb)
    return yf.reshape(n, RW_DIM), yb.reshape(n, RW_DIM)


def _out_odd_kernel(x_ref, hf_ref, hb_ref, hg_ref, hnw_ref, yf_ref, yb_ref, r_ref, k_ref, v_ref,
                    gate_ref, rk_ref, lw_ref, lb_ref, bd_ref, wc_ref, wd_ref, g_ref, b_ref,
                    w1_ref, w2_ref, g2_ref, b2_ref, o_ref):
    nw = hnw_ref[...]
    mix = None
    for h in range(HG_HEADS):
        sl = slice(h * HG_DK, (h + 1) * HG_DK)
        o = hf_ref[:, sl] + hb_ref[:, sl]
        y = o * lax.rsqrt(jnp.mean(o * o, -1, keepdims=True) + RMS_EPS) * nw * _silu(hg_ref[:, sl])
        part = _mm(y, wc_ref[sl, :])
        mix = part if mix is None else mix + part
    bd = bd_ref[...]
    y = yf_ref[...] + yb_ref[...]
    yc = y - _sel_mm(bd, y, sel_first=False) * (1.0 / RW_HD)
    var = _mm(yc * yc, bd) * (1.0 / RW_HD)
    yn = yc * lax.rsqrt(var + RW_GN_EPS) * lw_ref[...] + lb_ref[...]
    rk = r_ref[...].astype(F32) * k_ref[...] * rk_ref[...]
    bonus = _mm(rk, bd) * v_ref[...]
    mix = mix + _mm((yn + bonus) * gate_ref[...], wd_ref[...])
    x1 = _layer_norm(DEEPNORM_ALPHA * x_ref[...] + mix, g_ref[...], b_ref[...])
    o_ref[...] = _mlp_tail(x1, w1_ref, w2_ref, g2_ref, b2_ref)


def _out_odd(x2, hf, hb, proj, hnw, yf, yb, r, k, v, gate, r_k, lnx_w, lnx_b, bd, wc, wd, g, b,
             mlp, tm=512):
    n, d = x2.shape
    tm = min(tm, n)
    row = lambda w: pl.BlockSpec((tm, w), lambda i: (i, 0))
    consts = (r_k, lnx_w, lnx_b, bd, wc, wd, g, b) + tuple(mlp)
    return pl.pallas_call(
        _out_odd_kernel, grid=(n // tm,),
        in_specs=[row(d), row(512), row(512), pl.BlockSpec((tm, 512), lambda i: (i, 4)),
                  _resident(hnw)] + [row(512)] * 6 + [_resident(a) for a in consts],
        out_specs=row(d), out_shape=jax.ShapeDtypeStruct((n, d), F32),
        compiler_params=_params("parallel"), name="out_odd",
    )(x2, hf, hb, proj, hnw, yf, yb, r, k, v, gate, *consts)


def _odd_layer(x2, bsz, t_len, layer, w_in, hg_lower, hg_norm_w, rw_mu, rw_w0, rw_w2, rw_a0, rw_a2,
               rw_g2, rw_k_k, rw_k_a, rw_r_k, rw_lnx_w, rw_lnx_b, w_out, ln_g, ln_b, mlp):
    proj = _project(x2, w_in.astype(BF16))
    hf, hb = _hg_scan(proj, hg_lower, layer, bsz, t_len)
    idx = np.arange(RW_DIM) // RW_HD
    bd = jnp.asarray(idx[:, None] == idx[None, :], BF16)
    lora = rw_w2.shape[1]
    w2p = jnp.pad(rw_w2, ((0, 0), (0, LANES - lora), (0, 0)))
    a2p = jnp.pad(rw_a2, ((LANES - rw_a2.shape[0], 0), (0, 0)))
    r, k, v, kk, kka, lwf, lwb, gate = _rw_pre(proj, rw_mu, rw_w0, w2p, rw_a0[None], a2p, rw_g2,
                                               rw_k_k[None], rw_k_a[None], bd, bsz, t_len)
    yf, yb = _rw_scan(r, k, v, kk, kka, lwf, lwb, bsz, t_len)
    return _out_odd(x2, hf, hb, proj, hg_norm_w[None], yf, yb, r, k, v, gate,
                    rw_r_k.reshape(1, RW_DIM), rw_lnx_w[None], rw_lnx_b[None], bd,
                    w_out[:512].astype(BF16), w_out[512:].astype(BF16), ln_g[None], ln_b[None], mlp)


def kernel(x, ev_w_in, hy_conv_w, hy_conv_b, hy_filt_w1, hy_filt_b1, hy_filt_w2, hy_filt_b2, hy_filt_w3, hy_filt_b3, hy_filt_w4, hy_filt_freq, hy_skip, gdn_conv_w, gdn_a_log, gdn_dt_bias, gdn_norm_w, ev_w_out, od_w_in, hg_lower, hg_norm_w, rw_mu, rw_w0, rw_w2, rw_a0, rw_a2, rw_g2, rw_k_k, rw_k_a, rw_r_k, rw_lnx_w, rw_lnx_b, od_w_out, ln_g, ln_b, mlp_w1, mlp_w2):
    bsz, t_len, d = x.shape
    x2 = x.reshape(bsz * t_len, d)
    for layer in range(DEPTH):
        i = layer // 2
        mlp = (mlp_w1[layer].astype(BF16), mlp_w2[layer].astype(BF16), ln_g[layer, 1][None],
               ln_b[layer, 1][None])
        if layer % 2 == 0:
            x2 = _even_layer(x2, bsz, t_len, ev_w_in[i], hy_conv_w[i], hy_conv_b[i], hy_filt_w1[i],
                             hy_filt_b1[i], hy_filt_w2[i], hy_filt_b2[i], hy_filt_w3[i],
                             hy_filt_b3[i], hy_filt_w4[i], hy_filt_freq[i], hy_skip[i],
                             gdn_conv_w[i], gdn_a_log[i], gdn_dt_bias[i], gdn_norm_w[i],
                             ev_w_out[i], ln_g[layer, 0], ln_b[layer, 0], mlp)
        else:
            x2 = _odd_layer(x2, bsz, t_len, layer, od_w_in[i], hg_lower, hg_norm_w[i], rw_mu[i],
                            rw_w0[i], rw_w2[i], rw_a0[i], rw_a2[i], rw_g2[i], rw_k_k[i], rw_k_a[i],
                            rw_r_k[i], rw_lnx_w[i], rw_lnx_b[i], od_w_out[i], ln_g[layer, 0],
                            ln_b[layer, 0], mlp)
    return x2.reshape(bsz, t_len, d)
```
